```python
import jax, jax.numpy as jnp
from jax import lax
import numpy as np

D_MODEL = 1024
BATCH = 4
SEQ = 4096
DEPTH = 2

CHUNK = 64
N_EVEN = (DEPTH + 1) // 2
N_ODD = DEPTH // 2
EPS = 1e-6
ROPE_THETA = 10000.0

GM_GROUPS = 4
GM_GROUP_DIM = 128
GM_WIDTH = GM_GROUPS * GM_GROUP_DIM
GM_BLOCK = 128
MLA_HEADS = 8
MLA_Q_RANK = 384
MLA_KV_RANK = 256
MLA_NOPE = 64
MLA_ROPE = 32
MLA_V = 64
MLA_QK = MLA_NOPE + MLA_ROPE
ATTN_BLOCK = 128
EVEN_IN = 2 * GM_WIDTH + MLA_Q_RANK + MLA_KV_RANK + MLA_ROPE
EVEN_MIX = GM_WIDTH + MLA_HEADS * MLA_V
RET_HEADS = 4
RET_QK = 256
RET_V = 512
ODD_IN = 2 * RET_HEADS * RET_QK + 2 * RET_HEADS * RET_V
ODD_MIX = RET_HEADS * RET_V
D_FF = -(-8 * D_MODEL // (3 * 256)) * 256

kernel_name = "hybrid_gmlp_mla_retention_trunk"


def rmsnorm(x, g):
    xf = x.astype(jnp.float32)
    y = xf * lax.rsqrt(jnp.mean(xf * xf, axis=-1, keepdims=True) + EPS)
    return (y * g.astype(jnp.float32)).astype(x.dtype)


def rope(x, pos):
    half = x.shape[-1] // 2
    inv = ROPE_THETA ** (-jnp.arange(half, dtype=jnp.float32) / half)
    ang = pos.astype(jnp.float32)[:, None] * inv[None, :]
    cos = jnp.cos(ang)[None, :, None, :]
    sin = jnp.sin(ang)[None, :, None, :]
    xf = x.astype(jnp.float32)
    x1, x2 = xf[..., :half], xf[..., half:]
    return jnp.concatenate([x1 * cos - x2 * sin, x2 * cos + x1 * sin], axis=-1).astype(x.dtype)


def spatial_gating(u, v, v_norm, w_s, b_s):
    B, S, _ = u.shape
    n = S // GM_BLOCK
    v = rmsnorm(v.reshape(B, S, GM_GROUPS, GM_GROUP_DIM), v_norm)
    cid = jnp.arange(GM_BLOCK) // CHUNK
    w = jnp.where((cid[None, :] <= cid[:, None])[None], w_s, 0)
    vb = v.reshape(B, n, GM_BLOCK, GM_GROUPS, GM_GROUP_DIM)
    s = jnp.einsum('gts,bnsgc->bntgc', w, vb) + b_s.T[:, :, None]
    ub = u.reshape(B, n, GM_BLOCK, GM_GROUPS, GM_GROUP_DIM)
    return (ub * s).reshape(B, S, GM_WIDTH)


def block_causal_attention(q, k, v):
    S = q.shape[1]
    scale = MLA_QK ** -0.5
    chunk_id = jnp.arange(S) // CHUNK
    outs = []
    for start in range(0, S, ATTN_BLOCK):
        end = start + ATTN_BLOCK
        s = jnp.einsum('bqhd,bkhd->bhqk', q[:, start:end], k[:, :end]).astype(jnp.float32) * scale
        mask = chunk_id[None, :end] <= chunk_id[start:end, None]
        p = jax.nn.softmax(jnp.where(mask, s, -jnp.inf), axis=-1).astype(v.dtype)
        outs.append(jnp.einsum('bhqk,bkhd->bqhd', p, v[:, :end]))
    return jnp.concatenate(outs, axis=1)


def mla(c_q, c_kv, k_pe, q_a_norm, w_uq, kv_a_norm, w_ukv, q_norm, k_norm, pos):
    B, S, _ = c_q.shape
    q = (rmsnorm(c_q, q_a_norm) @ w_uq).reshape(B, S, MLA_HEADS, MLA_QK)
    kv = (rmsnorm(c_kv, kv_a_norm) @ w_ukv).reshape(B, S, MLA_HEADS, MLA_NOPE + MLA_V)
    k = jnp.concatenate([kv[..., :MLA_NOPE],
                         jnp.broadcast_to(k_pe[:, :, None, :], (B, S, MLA_HEADS, MLA_ROPE))], axis=-1)
    v = kv[..., MLA_NOPE:]
    q = rmsnorm(q, q_norm)
    k = rmsnorm(k, k_norm)
    q = jnp.concatenate([q[..., :MLA_NOPE], rope(q[..., MLA_NOPE:], pos)], axis=-1)
    k = jnp.concatenate([k[..., :MLA_NOPE], rope(k[..., MLA_NOPE:], pos)], axis=-1)
    o = block_causal_attention(q, k, v)
    return o.reshape(B, S, MLA_HEADS * MLA_V)


def even_mixer(h, w_in, gm_v_norm, gm_w_s, gm_b_s, q_a_norm, w_uq, kv_a_norm, w_ukv,
               q_norm, k_norm, w_out, pos):
    z = h @ w_in
    o1 = GM_WIDTH
    o2 = 2 * GM_WIDTH
    o3 = o2 + MLA_Q_RANK
    o4 = o3 + MLA_KV_RANK
    u = jax.nn.gelu(z[..., :o1])
    v = jax.nn.gelu(z[..., o1:o2])
    a = spatial_gating(u, v, gm_v_norm, gm_w_s, gm_b_s)
    b = mla(z[..., o2:o3], z[..., o3:o4], z[..., o4:], q_a_norm, w_uq, kv_a_norm, w_ukv,
            q_norm, k_norm, pos)
    return jnp.concatenate([a, b], axis=-1) @ w_out


def retention(q, k, v):
    B, S, H, dk = q.shape
    dv = v.shape[-1]
    n = S // CHUNK
    lg = jnp.log(1.0 - 2.0 ** (-5.0 - jnp.arange(H, dtype=jnp.float32)))
    j = jnp.arange(CHUNK, dtype=jnp.float32)
    diff = j[:, None] - j[None, :]
    dmask = jnp.where(diff >= 0, jnp.exp(lg[:, None, None] * jnp.maximum(diff, 0.0)), 0.0)
    xi = jnp.exp(lg[:, None] * (j + 1.0))[None, :, :, None]
    zeta = jnp.exp(lg[:, None] * (CHUNK - 1.0 - j))[None, :, :, None]
    g_chunk = jnp.exp(lg * CHUNK)[None, :, None, None]

    def to_chunks(t):
        return t.astype(jnp.float32).reshape(B, n, CHUNK, H, -1).transpose(1, 0, 3, 2, 4)

    qc = to_chunks(q)
    kc = to_chunks(k) * (dk ** -0.5)
    vc = to_chunks(v)

    def step(state, inp):
        qi, ki, vi = inp
        a = jnp.einsum('bhjd,bhmd->bhjm', qi, ki) * dmask
        o = jnp.einsum('bhjm,bhme->bhje', a, vi) + jnp.einsum('bhjd,bhde->bhje', qi, state) * xi
        state = state * g_chunk + jnp.einsum('bhmd,bhme->bhde', ki * zeta, vi)
        return state, o

    state0 = jnp.zeros((B, H, dk, dv), jnp.float32)
    _, o = lax.scan(step, state0, (qc, kc, vc))
    return o.transpose(1, 0, 3, 2, 4).reshape(B, S, H, dv)


def odd_mixer(h, w_in, out_norm, w_out, pos):
    B, S, _ = h.shape
    nq = RET_HEADS * RET_QK
    nv = RET_HEADS * RET_V
    z = h @ w_in
    q = rope(z[..., :nq].reshape(B, S, RET_HEADS, RET_QK), pos)
    k = rope(z[..., nq:2 * nq].reshape(B, S, RET_HEADS, RET_QK), pos)
    v = z[..., 2 * nq:2 * nq + nv].reshape(B, S, RET_HEADS, RET_V)
    g = z[..., 2 * nq + nv:]
    o = rmsnorm(retention(q, k, v), out_norm)
    o = o.reshape(B, S, nv).astype(h.dtype) * jax.nn.silu(g)
    return o @ w_out


def swiglu(h, w_gate, w_up, w_down):
    return (jax.nn.silu(h @ w_gate) * (h @ w_up)) @ w_down


def setup_inputs(seed: int = 0) -> dict:
    key = jax.random.key(seed)
    ks = jax.random.split(key, 24)
    f32 = jnp.float32

    def w(k, shape, fan_in):
        return jax.random.normal(k, shape, f32) * (fan_in ** -0.5)

    def gain(k, shape):
        return 1.0 + 0.1 * jax.random.normal(k, shape, f32)

    return {
        "x": jax.random.normal(ks[0], (BATCH, SEQ, D_MODEL), f32),
        "norm_mix": gain(ks[1], (DEPTH, D_MODEL)),
        "norm_ffn": gain(ks[2], (DEPTH, D_MODEL)),
        "even_w_in": w(ks[3], (N_EVEN, D_MODEL, EVEN_IN), D_MODEL),
        "gm_v_norm": gain(ks[4], (N_EVEN, GM_GROUPS, GM_GROUP_DIM)),
        "gm_w_s": w(ks[5], (N_EVEN, GM_GROUPS, GM_BLOCK, GM_BLOCK), GM_BLOCK),
        "gm_b_s": gain(ks[6], (N_EVEN, GM_GROUPS, GM_BLOCK)),
        "mla_q_a_norm": gain(ks[7], (N_EVEN, MLA_Q_RANK)),
        "mla_w_uq": w(ks[8], (N_EVEN, MLA_Q_RANK, MLA_HEADS * MLA_QK), MLA_Q_RANK),
        "mla_kv_a_norm": gain(ks[9], (N_EVEN, MLA_KV_RANK)),
        "mla_w_ukv": w(ks[10], (N_EVEN, MLA_KV_RANK, MLA_HEADS * (MLA_NOPE + MLA_V)), MLA_KV_RANK),
        "mla_q_norm": gain(ks[11], (N_EVEN, MLA_QK)),
        "mla_k_norm": gain(ks[12], (N_EVEN, MLA_QK)),
        "even_w_out": w(ks[13], (N_EVEN, EVEN_MIX, D_MODEL), EVEN_MIX),
        "odd_w_in": w(ks[14], (N_ODD, D_MODEL, ODD_IN), D_MODEL),
        "ret_out_norm": gain(ks[15], (N_ODD, RET_HEADS, RET_V)),
        "odd_w_out": w(ks[16], (N_ODD, ODD_MIX, D_MODEL), ODD_MIX),
        "ffn_w_gate": w(ks[17], (DEPTH, D_MODEL, D_FF), D_MODEL),
        "ffn_w_up": w(ks[18], (DEPTH, D_MODEL, D_FF), D_MODEL),
        "ffn_w_down": w(ks[19], (DEPTH, D_FF, D_MODEL), D_FF),
    }


def reference(x, norm_mix, norm_ffn, even_w_in, gm_v_norm, gm_w_s, gm_b_s,
              mla_q_a_norm, mla_w_uq, mla_kv_a_norm, mla_w_ukv, mla_q_norm, mla_k_norm,
              even_w_out, odd_w_in, ret_out_norm, odd_w_out,
              ffn_w_gate, ffn_w_up, ffn_w_down):
    pos = jnp.arange(x.shape[1])
    for l in range(DEPTH):
        i = l // 2
        h = rmsnorm(x, norm_mix[l])
        if l % 2 == 0:
            mix = even_mixer(h, even_w_in[i], gm_v_norm[i], gm_w_s[i], gm_b_s[i],
                             mla_q_a_norm[i], mla_w_uq[i], mla_kv_a_norm[i], mla_w_ukv[i],
                             mla_q_norm[i], mla_k_norm[i], even_w_out[i], pos)
        else:
            mix = odd_mixer(h, odd_w_in[i], ret_out_norm[i], odd_w_out[i], pos)
        x = x + mix
        h = rmsnorm(x, norm_ffn[l])
        x = x + swiglu(h, ffn_w_gate[l], ffn_w_up[l], ffn_w_down[l])
    return x
```

```python
import functools
import math

import jax
import jax.numpy as jnp
from jax import lax
from jax.experimental import pallas as pl
from jax.experimental.pallas import tpu as pltpu

F32 = jnp.float32
BF16 = jnp.bfloat16

EPS = 1e-6
ROPE_THETA = 10000.0
CHUNK = 64

GM_GROUPS = 4
GM_GROUP_DIM = 128
GM_WIDTH = GM_GROUPS * GM_GROUP_DIM
GM_BLOCK = 128
MLA_HEADS = 8
MLA_Q_RANK = 384
MLA_KV_RANK = 256
MLA_NOPE = 64
MLA_ROPE = 32
MLA_V = 64
MLA_QK = MLA_NOPE + MLA_ROPE
RET_HEADS = 4
RET_QK = 256
RET_V = 512

LANES = 128
V7X_VMEM_LIMIT = 56 * 1024 * 1024

EVEN_IN_ROWS = 256
ATTN_Q_ROWS = 512
ATTN_KV_ROWS = 512
FFN_ROWS = 512
ODD_IN_ROWS = 512
RET_ROWS = 512
FF_CHUNKS = (768, 768, 768, 512)


def _rms_scale(t):
    return lax.rsqrt(jnp.mean(t * t, axis=-1, keepdims=True) + EPS)


def _gelu_tanh(t):
    return 0.5 * t * (1.0 + jnp.tanh(math.sqrt(2.0 / math.pi) * (t + 0.044715 * (t * t * t))))


def _silu(t):
    return t * (1.0 / (1.0 + jnp.exp(-t)))


def _const_spec(shape):
    nd = len(shape)
    return pl.BlockSpec(shape, lambda *_: (0,) * nd, pipeline_mode=pl.Buffered(1))


def _params(sem):
    return pltpu.CompilerParams(dimension_semantics=sem, vmem_limit_bytes=V7X_VMEM_LIMIT)


def _head_norm_rope(t, gain, cosm, sina, sinb):
    ss = jnp.sum(t * t, axis=-1, keepdims=True) * (1.0 / MLA_QK)
    n = t * lax.rsqrt(ss + EPS) * gain
    half = MLA_ROPE // 2
    return n * cosm + pltpu.roll(n, LANES - half, 1) * sina + pltpu.roll(n, half, 1) * sinb


def _even_in_kernel(x_ref, nmix_ref, win_ref, gvn_ref, ws_ref, bs_ref, qan_ref, wuq_ref,
                    kvan_ref, wukv_ref, qn_ref, kn_ref, cosm_ref, sina_ref, sinb_ref,
                    a_ref, q_ref, k_ref, v_ref):
    rows = x_ref.shape[0]
    x = x_ref[...]
    h = (x * _rms_scale(x) * nmix_ref[...]).astype(BF16)
    z = jnp.dot(h, win_ref[...], preferred_element_type=F32)

    o1, o2 = GM_WIDTH, 2 * GM_WIDTH
    o3 = o2 + MLA_Q_RANK
    o4 = o3 + MLA_KV_RANK

    u = _gelu_tanh(z[:, :o1])
    v = _gelu_tanh(z[:, o1:o2])
    t_out = lax.broadcasted_iota(jnp.int32, (GM_BLOCK, GM_BLOCK), 0) // CHUNK
    t_in = lax.broadcasted_iota(jnp.int32, (GM_BLOCK, GM_BLOCK), 1) // CHUNK
    causal = t_in <= t_out
    for g in range(GM_GROUPS):
        cs = slice(g * GM_GROUP_DIM, (g + 1) * GM_GROUP_DIM)
        vg = v[:, cs]
        vn = (vg * _rms_scale(vg) * gvn_ref[:, cs]).astype(BF16)
        wg = jnp.where(causal, ws_ref[g], 0.0).astype(BF16)
        bias = bs_ref[:, g:g + 1]
        for j in range(rows // GM_BLOCK):
            rs = slice(j * GM_BLOCK, (j + 1) * GM_BLOCK)
            s = jnp.dot(wg, vn[rs], preferred_element_type=F32) + bias
            a_ref[rs, cs] = (u[rs, cs] * s).astype(a_ref.dtype)

    cq = z[:, o2:o3]
    cqn = (cq * _rms_scale(cq) * qan_ref[...]).astype(BF16)
    q = jnp.dot(cqn, wuq_ref[...], preferred_element_type=F32)
    ckv = z[:, o3:o4]
    ckvn = (ckv * _rms_scale(ckv) * kvan_ref[...]).astype(BF16)
    kv = jnp.dot(ckvn, wukv_ref[...], preferred_element_type=F32)
    kpe = z[:, o4:o4 + LANES]

    cosm, sina, sinb = cosm_ref[...], sina_ref[...], sinb_ref[...]
    qgain = qn_ref[...] * (MLA_QK ** -0.5)
    kgain = kn_ref[...]
    for hd in range(MLA_HEADS):
        hs = slice(hd * LANES, (hd + 1) * LANES)
        q_ref[:, hs] = _head_norm_rope(q[:, hs], qgain, cosm, sina, sinb).astype(q_ref.dtype)
        k_ref[:, hs] = _head_norm_rope(kv[:, hs] + kpe, kgain, cosm, sina, sinb).astype(k_ref.dtype)
    v_ref[...] = kv[:, MLA_HEADS * LANES:].astype(v_ref.dtype)


def _even_in(x2, nmix, win, gvn, ws, bs_t, qan, wuq, kvan, wukv, qn, kn, cosm, sina, sinb, seq):
    tokens, d = x2.shape
    rows = EVEN_IN_ROWS
    per_seq = seq // rows
    row_spec = lambda w: pl.BlockSpec((rows, w), lambda i: (i, 0))
    tab_spec = pl.BlockSpec((rows, LANES), lambda i: (i % per_seq, 0))
    hw = MLA_HEADS * LANES
    return pl.pallas_call(
        _even_in_kernel,
        grid=(tokens // rows,),
        in_specs=[row_spec(d), _const_spec(nmix.shape), _const_spec(win.shape), _const_spec(gvn.shape),
                  _const_spec(ws.shape), _const_spec(bs_t.shape), _const_spec(qan.shape),
                  _const_spec(wuq.shape), _const_spec(kvan.shape), _const_spec(wukv.shape),
                  _const_spec(qn.shape), _const_spec(kn.shape), tab_spec, tab_spec, tab_spec],
        out_specs=[row_spec(GM_WIDTH), row_spec(hw), row_spec(hw), row_spec(MLA_HEADS * MLA_V)],
        out_shape=[jax.ShapeDtypeStruct((tokens, GM_WIDTH), BF16),
                   jax.ShapeDtypeStruct((tokens, hw), BF16),
                   jax.ShapeDtypeStruct((tokens, hw), BF16),
                   jax.ShapeDtypeStruct((tokens, MLA_HEADS * MLA_V), BF16)],
        compiler_params=_params(("parallel",)),
        name="even_in",
    )(x2, nmix, win, gvn, ws, bs_t, qan, wuq, kvan, wukv, qn, kn, cosm, sina, sinb)


def _attn_kernel(q_ref, k_ref, v_ref, o_ref):
    tq = q_ref.shape[0]
    tk = ATTN_KV_ROWS
    i = pl.program_id(2)
    qs = [q_ref[:, hh * LANES:(hh + 1) * LANES] for hh in range(2)]

    def step(j, carry, masked):
        start = pl.multiple_of(j * tk, tk)
        vb = v_ref[pl.ds(start, tk), :]
        out = []
        for hh in range(2):
            m, l, acc = carry[hh]
            kb = k_ref[pl.ds(start, tk), hh * LANES:(hh + 1) * LANES]
            s = lax.dot_general(qs[hh], kb, (((1,), (1,)), ((), ())), preferred_element_type=F32)
            if masked:
                rq = lax.broadcasted_iota(jnp.int32, (tq, tk), 0) // CHUNK
                ck = lax.broadcasted_iota(jnp.int32, (tq, tk), 1) // CHUNK
                s = jnp.where(ck <= rq, s, -jnp.inf)
            m_new = jnp.maximum(m, jnp.max(s, axis=-1, keepdims=True))
            alpha = jnp.exp(m - m_new)
            p = jnp.exp(s - m_new)
            l = alpha * l + jnp.sum(p, axis=-1, keepdims=True)
            acc = alpha * acc + jnp.dot(p.astype(BF16), vb, preferred_element_type=F32)
            out.append((m_new, l, acc))
        return tuple(out)

    init = tuple((jnp.full((tq, 1), -jnp.inf, F32), jnp.zeros((tq, 1), F32), jnp.zeros((tq, LANES), F32))
                 for _ in range(2))
    carry = lax.fori_loop(0, i * (tq // tk), functools.partial(step, masked=False), init)
    carry = step(i, carry, True)
    o0 = carry[0][2] * (1.0 / carry[0][1])
    o1 = carry[1][2] * (1.0 / carry[1][1])
    lane = lax.broadcasted_iota(jnp.int32, (tq, LANES), 1)
    o_ref[...] = jnp.where(lane < MLA_V, o0, o1).astype(o_ref.dtype)


def _attention(q3, k3, v3):
    b, s, _ = q3.shape
    tq = ATTN_Q_ROWS
    assert tq == ATTN_KV_ROWS
    pairs = MLA_HEADS // 2
    return pl.pallas_call(
        _attn_kernel,
        grid=(b, pairs, s // tq),
        in_specs=[pl.BlockSpec((None, tq, 2 * LANES), lambda bi, p, i: (bi, i, p)),
                  pl.BlockSpec((None, s, 2 * LANES), lambda bi, p, i: (bi, 0, p)),
                  pl.BlockSpec((None, s, LANES), lambda bi, p, i: (bi, 0, p))],
        out_specs=pl.BlockSpec((None, tq, LANES), lambda bi, p, i: (bi, i, p)),
        out_shape=jax.ShapeDtypeStruct((b, s, MLA_HEADS * MLA_V), BF16),
        compiler_params=_params(("parallel", "parallel", "parallel")),
        name="attention",
    )(q3, k3, v3)


def _proj_ffn_kernel(*refs, n_in):
    in_refs = refs[:n_in]
    wout_ref, x_ref, nffn_ref, wg_ref, wu_ref, wd_ref, out_ref = refs[n_in:]
    mix = None
    off = 0
    for r in in_refs:
        kdim = r.shape[1]
        part = jnp.dot(r[...], wout_ref[off:off + kdim, :], preferred_element_type=F32)
        mix = part if mix is None else mix + part
        off += kdim
    x1 = x_ref[...] + mix
    h = (x1 * _rms_scale(x1) * nffn_ref[...]).astype(BF16)
    acc = x1
    c0 = 0
    for c in FF_CHUNKS:
        g = jnp.dot(h, wg_ref[:, c0:c0 + c], preferred_element_type=F32)
        u = jnp.dot(h, wu_ref[:, c0:c0 + c], preferred_element_type=F32)
        act = (_silu(g) * u).astype(BF16)
        acc = acc + jnp.dot(act, wd_ref[c0:c0 + c, :], preferred_element_type=F32)
        c0 += c
    out_ref[...] = acc


def _proj_ffn(mix_ins, wout, x2, nffn, wg, wu, wd):
    tokens, d = x2.shape
    rows = FFN_ROWS
    assert sum(FF_CHUNKS) == wg.shape[1]
    row_spec = lambda w: pl.BlockSpec((rows, w), lambda i: (i, 0))
    return pl.pallas_call(
        functools.partial(_proj_ffn_kernel, n_in=len(mix_ins)),
        grid=(tokens // rows,),
        in_specs=[row_spec(m.shape[1]) for m in mix_ins]
        + [_const_spec(wout.shape), row_spec(d), _const_spec(nffn.shape),
           _const_spec(wg.shape), _const_spec(wu.shape), _const_spec(wd.shape)],
        out_specs=row_spec(d),
        out_shape=jax.ShapeDtypeStruct((tokens, d), F32),
        compiler_params=_params(("parallel",)),
        name="proj_ffn",
    )(*mix_ins, wout, x2, nffn, wg, wu, wd)


def _odd_in_kernel(x_ref, nmix_ref, win_ref, cos_ref, sin_ref, q_ref, k_ref, v_ref, g_ref):
    x = x_ref[...]
    h = (x * _rms_scale(x) * nmix_ref[...]).astype(BF16)
    cos, sin = cos_ref[...], sin_ref[...]
    nq = RET_HEADS * RET_QK
    nv = RET_HEADS * RET_V
    half = RET_QK // 2

    def rope_store(dst, col0, scale):
        for hd in range(RET_HEADS):
            c = col0 + hd * RET_QK
            t = jnp.dot(h, win_ref[:, c:c + RET_QK], preferred_element_type=F32)
            t1, t2 = t[:, :half], t[:, half:]
            dst[:, hd * RET_QK:hd * RET_QK + half] = ((t1 * cos - t2 * sin) * scale).astype(dst.dtype)
            dst[:, hd * RET_QK + half:(hd + 1) * RET_QK] = ((t2 * cos + t1 * sin) * scale).astype(dst.dtype)

    rope_store(q_ref, 0, 1.0)
    rope_store(k_ref, nq, RET_QK ** -0.5)
    for hd in range(RET_HEADS):
        cs = slice(hd * RET_V, (hd + 1) * RET_V)
        v_ref[:, cs] = jnp.dot(h, win_ref[:, 2 * nq + hd * RET_V:2 * nq + (hd + 1) * RET_V],
                               preferred_element_type=F32).astype(v_ref.dtype)
        gate = jnp.dot(h, win_ref[:, 2 * nq + nv + hd * RET_V:2 * nq + nv + (hd + 1) * RET_V],
                       preferred_element_type=F32)
        g_ref[:, cs] = _silu(gate).astype(g_ref.dtype)


def _odd_in(x2, nmix, win, cos, sin, seq):
    tokens, d = x2.shape
    rows = ODD_IN_ROWS
    per_seq = seq // rows
    nq = RET_HEADS * RET_QK
    nv = RET_HEADS * RET_V
    row_spec = lambda w: pl.BlockSpec((rows, w), lambda i: (i, 0))
    tab_spec = pl.BlockSpec((rows, RET_QK // 2), lambda i: (i % per_seq, 0))
    return pl.pallas_call(
        _odd_in_kernel,
        grid=(tokens // rows,),
        in_specs=[row_spec(d), _const_spec(nmix.shape), _const_spec(win.shape), tab_spec, tab_spec],
        out_specs=[row_spec(nq), row_spec(nq), row_spec(nv), row_spec(nv)],
        out_shape=[jax.ShapeDtypeStruct((tokens, nq), BF16), jax.ShapeDtypeStruct((tokens, nq), BF16),
                   jax.ShapeDtypeStruct((tokens, nv), BF16), jax.ShapeDtypeStruct((tokens, nv), BF16)],
        compiler_params=_params(("parallel",)),
        name="odd_in",
    )(x2, nmix, win, cos, sin)


def _retention_kernel(q_ref, k_ref, v_ref, g_ref, ron_ref, o_ref, state_ref, decay_ref):
    tc = q_ref.shape[0]
    j = pl.program_id(1)

    def log_gamma(hd):
        return jnp.log(jnp.full((1, 1), 1.0 - 2.0 ** (-5.0 - hd), F32))

    @pl.when(j == 0)
    def _():
        state_ref[...] = jnp.zeros_like(state_ref)
        diff = (lax.broadcasted_iota(jnp.int32, (tc, tc), 0)
                - lax.broadcasted_iota(jnp.int32, (tc, tc), 1))
        dpos = jnp.maximum(diff, 0).astype(F32)
        for hd in range(RET_HEADS):
            decay_ref[hd] = jnp.where(diff >= 0, jnp.exp(log_gamma(hd) * dpos), 0.0)

    pos = lax.broadcasted_iota(jnp.int32, (tc, 1), 0).astype(F32)
    for hd in range(RET_HEADS):
        lg = log_gamma(hd)
        xi = jnp.exp(lg * (pos + 1.0))
        zeta = jnp.exp(lg * (tc - 1.0 - pos))
        g_chunk = jnp.exp(lg * float(tc))
        qs = slice(hd * RET_QK, (hd + 1) * RET_QK)
        vs = slice(hd * RET_V, (hd + 1) * RET_V)
        qh, kh, vh = q_ref[:, qs], k_ref[:, qs], v_ref[:, vs]
        state = state_ref[hd]
        a = lax.dot_general(qh, kh, (((1,), (1,)), ((), ())), preferred_element_type=F32) * decay_ref[hd]
        o = (jnp.dot(a.astype(BF16), vh, preferred_element_type=F32)
             + jnp.dot(qh, state.astype(BF16), preferred_element_type=F32) * xi)
        kz = (kh.astype(F32) * zeta).astype(BF16)
        state_ref[hd] = state * g_chunk + lax.dot_general(
            kz, vh, (((0,), (0,)), ((), ())), preferred_element_type=F32)
        on = o * _rms_scale(o) * ron_ref[:, vs]
        o_ref[:, vs] = (on * g_ref[:, vs].astype(F32)).astype(o_ref.dtype)


def _retention(q3, k3, v3, g3, ron):
    b, s, nq = q3.shape
    nv = v3.shape[2]
    tc = RET_ROWS
    blk = lambda w: pl.BlockSpec((None, tc, w), lambda bi, j: (bi, j, 0))
    return pl.pallas_call(
        _retention_kernel,
        grid=(b, s // tc),
        in_specs=[blk(nq), blk(nq), blk(nv), blk(nv), _const_spec(ron.shape)],
        out_specs=blk(nv),
        out_shape=jax.ShapeDtypeStruct((b, s, nv), BF16),
        scratch_shapes=[pltpu.VMEM((RET_HEADS, RET_QK, RET_V), F32),
                        pltpu.VMEM((RET_HEADS, tc, tc), F32)],
        compiler_params=_params(("arbitrary", "arbitrary")),
        name="retention",
    )(q3, k3, v3, g3, ron)


def _rope_angles(seq, half):
    inv = ROPE_THETA ** (-jnp.arange(half, dtype=F32) / half)
    return jnp.arange(seq, dtype=F32)[:, None] * inv[None, :]


def _head_slot_tables(seq):
    half = MLA_ROPE // 2
    ang = _rope_angles(seq, half)
    cos, sin = jnp.cos(ang), jnp.sin(ang)
    zeros = lambda w: jnp.zeros((seq, w), F32)
    pad = LANES - MLA_QK
    cosm = jnp.concatenate([jnp.ones((seq, MLA_NOPE), F32), cos, cos, zeros(pad)], axis=1)
    sina = jnp.concatenate([zeros(MLA_NOPE), -sin, zeros(half + pad)], axis=1)
    sinb = jnp.concatenate([zeros(MLA_NOPE + half), sin, zeros(pad)], axis=1)
    return cosm, sina, sinb


def _pad_heads(w, heads, width):
    r = w.shape[0]
    w3 = w.reshape(r, heads, width)
    return jnp.pad(w3, ((0, 0), (0, 0), (0, LANES - width))).reshape(r, heads * LANES)


def kernel(x, norm_mix, norm_ffn, even_w_in, gm_v_norm, gm_w_s, gm_b_s, mla_q_a_norm, mla_w_uq,
           mla_kv_a_norm, mla_w_ukv, mla_q_norm, mla_k_norm, even_w_out, odd_w_in, ret_out_norm,
           odd_w_out, ffn_w_gate, ffn_w_up, ffn_w_down):
    b, s, d = x.shape
    tokens = b * s
    x2 = x.reshape(tokens, d)
    row = lambda t: t.reshape(1, -1)

    o4 = 2 * GM_WIDTH + MLA_Q_RANK + MLA_KV_RANK
    w_in = even_w_in[0]
    win = jnp.concatenate([w_in[:, :o4], jnp.zeros((d, MLA_NOPE), F32), w_in[:, o4:],
                           jnp.zeros((d, LANES - MLA_QK), F32)], axis=1).astype(BF16)
    wuq = _pad_heads(mla_w_uq[0], MLA_HEADS, MLA_QK).astype(BF16)
    wukv3 = mla_w_ukv[0].reshape(MLA_KV_RANK, MLA_HEADS, MLA_NOPE + MLA_V)
    wukv = jnp.concatenate(
        [_pad_heads(wukv3[:, :, :MLA_NOPE].reshape(MLA_KV_RANK, -1), MLA_HEADS, MLA_NOPE),
         wukv3[:, :, MLA_NOPE:].reshape(MLA_KV_RANK, -1)], axis=1).astype(BF16)
    qn = jnp.pad(mla_q_norm[0], (0, LANES - MLA_QK)).reshape(1, LANES)
    kn = jnp.pad(mla_k_norm[0], (0, LANES - MLA_QK)).reshape(1, LANES)
    cosm, sina, sinb = _head_slot_tables(s)
    a, q, k, v = _even_in(x2, row(norm_mix[0]), win, row(gm_v_norm[0]), gm_w_s[0], gm_b_s[0].T,
                          row(mla_q_a_norm[0]), wuq, row(mla_kv_a_norm[0]), wukv, qn, kn,
                          cosm, sina, sinb, s)
    hw = MLA_HEADS * LANES
    o = _attention(q.reshape(b, s, hw), k.reshape(b, s, hw), v.reshape(b, s, MLA_HEADS * MLA_V))
    x2 = _proj_ffn([a, o.reshape(tokens, -1)], even_w_out[0].astype(BF16), x2, row(norm_ffn[0]),
                   ffn_w_gate[0].astype(BF16), ffn_w_up[0].astype(BF16), ffn_w_down[0].astype(BF16))

    ang = _rope_angles(s, RET_QK // 2)
    rq, rk, rv, rg = _odd_in(x2, row(norm_mix[1]), odd_w_in[0].astype(BF16), jnp.cos(ang), jnp.sin(ang), s)
    nq = RET_HEADS * RET_QK
    nv = RET_HEADS * RET_V
    og = _retention(rq.reshape(b, s, nq), rk.reshape(b, s, nq), rv.reshape(b, s, nv),
                    rg.reshape(b, s, nv), row(ret_out_norm[0]))
    x2 = _proj_ffn([og.reshape(tokens, nv)], odd_w_out[0].astype(BF16), x2, row(norm_ffn[1]),
                   ffn_w_gate[1].astype(BF16), ffn_w_up[1].astype(BF16), ffn_w_down[1].astype(BF16))
    return x2.reshape(b, s, d)
```

```python
import functools
import math

import jax
import jax.numpy as jnp
from jax import lax
from jax.experimental import pallas as pl
from jax.experimental.pallas import tpu as pltpu

F32 = jnp.float32
BF16 = jnp.bfloat16

EPS = 1e-6
ROPE_THETA = 10000.0
CHUNK = 64

GM_GROUPS = 4
GM_GROUP_DIM = 128
GM_WIDTH = GM_GROUPS * GM_GROUP_DIM
GM_BLOCK = 128
MLA_HEADS = 8
MLA_Q_RANK = 384
MLA_KV_RANK = 256
MLA_NOPE = 64
MLA_ROPE = 32
MLA_V = 64
MLA_QK = MLA_NOPE + MLA_ROPE
RET_HEADS = 4
RET_QK = 256
RET_V = 512

LANES = 128
V7X_VMEM_LIMIT = 56 * 1024 * 1024

EVEN_IN_ROWS = 256
ATTN_Q_ROWS = 512
ATTN_KV_ROWS = 512
FFN_ROWS = 512
ODD_IN_ROWS = 512
RET_ROWS = 512
FF_CHUNKS = (768, 768, 768, 512)


def _rms_scale(t):
    return lax.rsqrt(jnp.mean(t * t, axis=-1, keepdims=True) + EPS)


def _gelu_tanh(t):
    return 0.5 * t * (1.0 + jnp.tanh(math.sqrt(2.0 / math.pi) * (t + 0.044715 * (t * t * t))))


def _silu(t):
    return t * (1.0 / (1.0 + jnp.exp(-t)))


def _const_spec(shape):
    nd = len(shape)
    return pl.BlockSpec(shape, lambda *_: (0,) * nd, pipeline_mode=pl.Buffered(1))


def _params(sem):
    return pltpu.CompilerParams(dimension_semantics=sem, vmem_limit_bytes=V7X_VMEM_LIMIT)


def _head_norm_rope(t, gain, cosm, sina, sinb):
    ss = jnp.sum(t * t, axis=-1, keepdims=True) * (1.0 / MLA_QK)
    n = t * lax.rsqrt(ss + EPS) * gain
    half = MLA_ROPE // 2
    return n * cosm + pltpu.roll(n, LANES - half, 1) * sina + pltpu.roll(n, half, 1) * sinb


def _even_in_kernel(x_ref, nmix_ref, win_ref, gvn_ref, ws_ref, bs_ref, qan_ref, wuq_ref,
                    kvan_ref, wukv_ref, qn_ref, kn_ref, cosm_ref, sina_ref, sinb_ref,
                    a_ref, q_ref, k_ref, v_ref):
    rows = x_ref.shape[0]
    x = x_ref[...]
    h = (x * _rms_scale(x) * nmix_ref[...]).astype(BF16)
    z = jnp.dot(h, win_ref[...], preferred_element_type=F32)

    o1, o2 = GM_WIDTH, 2 * GM_WIDTH
    o3 = o2 + MLA_Q_RANK
    o4 = o3 + MLA_KV_RANK

    u = _gelu_tanh(z[:, :o1])
    v = _gelu_tanh(z[:, o1:o2])
    t_out = lax.broadcasted_iota(jnp.int32, (GM_BLOCK, GM_BLOCK), 0) // CHUNK
    t_in = lax.broadcasted_iota(jnp.int32, (GM_BLOCK, GM_BLOCK), 1) // CHUNK
    causal = t_in <= t_out
    for g in range(GM_GROUPS):
        cs = slice(g * GM_GROUP_DIM, (g + 1) * GM_GROUP_DIM)
        vg = v[:, cs]
        vn = (vg * _rms_scale(vg) * gvn_ref[:, cs]).astype(BF16)
        wg = jnp.where(causal, ws_ref[g], 0.0).astype(BF16)
        bias = bs_ref[:, g:g + 1]
        for j in range(rows // GM_BLOCK):
            rs = slice(j * GM_BLOCK, (j + 1) * GM_BLOCK)
            s = jnp.dot(wg, vn[rs], preferred_element_type=F32) + bias
            a_ref[rs, cs] = (u[rs, cs] * s).astype(a_ref.dtype)

    cq = z[:, o2:o3]
    cqn = (cq * _rms_scale(cq) * qan_ref[...]).astype(BF16)
    q = jnp.dot(cqn, wuq_ref[...], preferred_element_type=F32)
    ckv = z[:, o3:o4]
    ckvn = (ckv * _rms_scale(ckv) * kvan_ref[...]).astype(BF16)
    kv = jnp.dot(ckvn, wukv_ref[...], preferred_element_type=F32)
    kpe = z[:, o4:o4 + LANES]

    cosm, sina, sinb = cosm_ref[...], sina_ref[...], sinb_ref[...]
    qgain = qn_ref[...] * (MLA_QK ** -0.5 * math.log2(math.e))
    kgain = kn_ref[...]
    for hd in range(MLA_HEADS):
        hs = slice(hd * LANES, (hd + 1) * LANES)
        q_ref[:, hs] = _head_norm_rope(q[:, hs], qgain, cosm, sina, sinb).astype(q_ref.dtype)
        k_ref[:, hs] = _head_norm_rope(kv[:, hs] + kpe, kgain, cosm, sina, sinb).astype(k_ref.dtype)
    v_ref[...] = kv[:, MLA_HEADS * LANES:].astype(v_ref.dtype)


def _even_in(x2, nmix, win, gvn, ws, bs_t, qan, wuq, kvan, wukv, qn, kn, cosm, sina, sinb, seq):
    tokens, d = x2.shape
    rows = EVEN_IN_ROWS
    per_seq = seq // rows
    row_spec = lambda w: pl.BlockSpec((rows, w), lambda i: (i, 0))
    tab_spec = pl.BlockSpec((rows, LANES), lambda i: (i % per_seq, 0))
    hw = MLA_HEADS * LANES
    return pl.pallas_call(
        _even_in_kernel,
        grid=(tokens // rows,),
        in_specs=[row_spec(d), _const_spec(nmix.shape), _const_spec(win.shape), _const_spec(gvn.shape),
                  _const_spec(ws.shape), _const_spec(bs_t.shape), _const_spec(qan.shape),
                  _const_spec(wuq.shape), _const_spec(kvan.shape), _const_spec(wukv.shape),
                  _const_spec(qn.shape), _const_spec(kn.shape), tab_spec, tab_spec, tab_spec],
        out_specs=[row_spec(GM_WIDTH), row_spec(hw), row_spec(hw), row_spec(MLA_HEADS * MLA_V)],
        out_shape=[jax.ShapeDtypeStruct((tokens, GM_WIDTH), BF16),
                   jax.ShapeDtypeStruct((tokens, hw), BF16),
                   jax.ShapeDtypeStruct((tokens, hw), BF16),
                   jax.ShapeDtypeStruct((tokens, MLA_HEADS * MLA_V), BF16)],
        compiler_params=_params(("parallel",)),
        name="even_in",
    )(x2, nmix, win, gvn, ws, bs_t, qan, wuq, kvan, wukv, qn, kn, cosm, sina, sinb)


def _attn_kernel(q_ref, k_ref, v_ref, o_ref, vt_ref, sa_ref, sb_ref, m_ref, l_ref, acc_ref):
    tq = q_ref.shape[0]
    tk = ATTN_KV_ROWS
    i = pl.program_id(2)

    @pl.when(i == 0)
    def _():
        for c in range(vt_ref.shape[0]):
            vt_ref[c] = v_ref[c * tk:(c + 1) * tk, :].astype(F32).T.astype(vt_ref.dtype)

    m_ref[...] = jnp.full(m_ref.shape, -jnp.inf, F32)
    l_ref[...] = jnp.zeros(l_ref.shape, F32)
    acc_ref[...] = jnp.zeros(acc_ref.shape, F32)

    def scores(j, dst):
        start = pl.multiple_of(j * tk, tk)
        for hh in range(2):
            dst[hh] = lax.dot_general(
                k_ref[pl.ds(start, tk), hh * LANES:(hh + 1) * LANES], q_ref[:, hh * LANES:(hh + 1) * LANES],
                (((1,), (1,)), ((), ())), preferred_element_type=F32)

    def update(j, src, masked):
        vt = vt_ref[j]
        for hh in range(2):
            s = src[hh]
            if masked:
                ck = lax.broadcasted_iota(jnp.int32, (tk, tq), 0) // CHUNK
                cq = lax.broadcasted_iota(jnp.int32, (tk, tq), 1) // CHUNK
                s = jnp.where(ck <= cq, s, -jnp.inf)
            m = m_ref[hh]
            m_new = jnp.maximum(m, jnp.max(s, axis=0, keepdims=True))
            alpha = jnp.exp2(m - m_new)
            p = jnp.exp2(s - m_new)
            m_ref[hh] = m_new
            l_ref[hh] = alpha * l_ref[hh] + jnp.sum(p, axis=0, keepdims=True)
            acc_ref[hh] = alpha * acc_ref[hh] + jnp.dot(vt[hh * MLA_V:(hh + 1) * MLA_V], p.astype(BF16),
                                                        preferred_element_type=F32)

    scores(0, sa_ref)

    def two_blocks(jj, carry):
        j0 = 2 * jj
        scores(j0 + 1, sb_ref)
        update(j0, sa_ref, False)
        scores(j0 + 2, sa_ref)
        update(j0 + 1, sb_ref, False)
        return carry

    lax.fori_loop(0, i // 2, two_blocks, 0)

    @pl.when(i % 2 == 1)
    def _():
        scores(i, sb_ref)
        update(i - 1, sa_ref, False)
        update(i, sb_ref, True)

    @pl.when(i % 2 == 0)
    def _():
        update(i, sa_ref, True)

    ot = jnp.concatenate([acc_ref[hh] * (1.0 / l_ref[hh]) for hh in range(2)], axis=0)
    o_ref[...] = ot.T.astype(o_ref.dtype)


def _attention(q3, k3, v3):
    b, s, _ = q3.shape
    tq = ATTN_Q_ROWS
    assert tq == ATTN_KV_ROWS
    pairs = MLA_HEADS // 2
    return pl.pallas_call(
        _attn_kernel,
        grid=(b, pairs, s // tq),
        in_specs=[pl.BlockSpec((None, tq, 2 * LANES), lambda bi, p, i: (bi, i, p)),
                  pl.BlockSpec((None, s, 2 * LANES), lambda bi, p, i: (bi, 0, p)),
                  pl.BlockSpec((None, s, LANES), lambda bi, p, i: (bi, 0, p))],
        out_specs=pl.BlockSpec((None, tq, LANES), lambda bi, p, i: (bi, i, p)),
        out_shape=jax.ShapeDtypeStruct((b, s, MLA_HEADS * MLA_V), BF16),
        scratch_shapes=[pltpu.VMEM((s // ATTN_KV_ROWS, LANES, ATTN_KV_ROWS), BF16),
                        pltpu.VMEM((2, ATTN_KV_ROWS, tq), F32),
                        pltpu.VMEM((2, ATTN_KV_ROWS, tq), F32),
                        pltpu.VMEM((2, 1, tq), F32),
                        pltpu.VMEM((2, 1, tq), F32),
                        pltpu.VMEM((2, MLA_V, tq), F32)],
        compiler_params=_params(("arbitrary", "arbitrary", "arbitrary")),
        name="attention",
    )(q3, k3, v3)


def _proj_ffn_kernel(*refs, n_in):
    in_refs = refs[:n_in]
    wout_ref, x_ref, nffn_ref, wg_ref, wu_ref, wd_ref, out_ref = refs[n_in:]
    mix = None
    off = 0
    for r in in_refs:
        kdim = r.shape[1]
        part = jnp.dot(r[...], wout_ref[off:off + kdim, :], preferred_element_type=F32)
        mix = part if mix is None else mix + part
        off += kdim
    x1 = x_ref[...] + mix
    h = (x1 * _rms_scale(x1) * nffn_ref[...]).astype(BF16)
    acc = x1
    c0 = 0
    for c in FF_CHUNKS:
        g = jnp.dot(h, wg_ref[:, c0:c0 + c], preferred_element_type=F32)
        u = jnp.dot(h, wu_ref[:, c0:c0 + c], preferred_element_type=F32)
        act = (_silu(g) * u).astype(BF16)
        acc = acc + jnp.dot(act, wd_ref[c0:c0 + c, :], preferred_element_type=F32)
        c0 += c
    out_ref[...] = acc


def _proj_ffn(mix_ins, wout, x2, nffn, wg, wu, wd):
    tokens, d = x2.shape
    rows = FFN_ROWS
    assert sum(FF_CHUNKS) == wg.shape[1]
    row_spec = lambda w: pl.BlockSpec((rows, w), lambda i: (i, 0))
    return pl.pallas_call(
        functools.partial(_proj_ffn_kernel, n_in=len(mix_ins)),
        grid=(tokens // rows,),
        in_specs=[row_spec(m.shape[1]) for m in mix_ins]
        + [_const_spec(wout.shape), row_spec(d), _const_spec(nffn.shape),
           _const_spec(wg.shape), _const_spec(wu.shape), _const_spec(wd.shape)],
        out_specs=row_spec(d),
        out_shape=jax.ShapeDtypeStruct((tokens, d), F32),
        compiler_params=_params(("parallel",)),
        name="proj_ffn",
    )(*mix_ins, wout, x2, nffn, wg, wu, wd)


def _odd_in_kernel(x_ref, nmix_ref, win_ref, cos_ref, sin_ref, q_ref, k_ref, v_ref, g_ref):
    x = x_ref[...]
    h = (x * _rms_scale(x) * nmix_ref[...]).astype(BF16)
    cos, sin = cos_ref[...], sin_ref[...]
    nq = RET_HEADS * RET_QK
    nv = RET_HEADS * RET_V
    half = RET_QK // 2

    def rope_store(dst, col0, scale):
        for hd in range(RET_HEADS):
            c = col0 + hd * RET_QK
            t = jnp.dot(h, win_ref[:, c:c + RET_QK], preferred_element_type=F32)
            t1, t2 = t[:, :half], t[:, half:]
            dst[:, hd * RET_QK:hd * RET_QK + half] = ((t1 * cos - t2 * sin) * scale).astype(dst.dtype)
            dst[:, hd * RET_QK + half:(hd + 1) * RET_QK] = ((t2 * cos + t1 * sin) * scale).astype(dst.dtype)

    rope_store(q_ref, 0, 1.0)
    rope_store(k_ref, nq, RET_QK ** -0.5)
    for hd in range(RET_HEADS):
        cs = slice(hd * RET_V, (hd + 1) * RET_V)
        v_ref[:, cs] = jnp.dot(h, win_ref[:, 2 * nq + hd * RET_V:2 * nq + (hd + 1) * RET_V],
                               preferred_element_type=F32).astype(v_ref.dtype)
        gate = jnp.dot(h, win_ref[:, 2 * nq + nv + hd * RET_V:2 * nq + nv + (hd + 1) * RET_V],
                       preferred_element_type=F32)
        g_ref[:, cs] = _silu(gate).astype(g_ref.dtype)


def _odd_in(x2, nmix, win, cos, sin, seq):
    tokens, d = x2.shape
    rows = ODD_IN_ROWS
    per_seq = seq // rows
    nq = RET_HEADS * RET_QK
    nv = RET_HEADS * RET_V
    row_spec = lambda w: pl.BlockSpec((rows, w), lambda i: (i, 0))
    tab_spec = pl.BlockSpec((rows, RET_QK // 2), lambda i: (i % per_seq, 0))
    return pl.pallas_call(
        _odd_in_kernel,
        grid=(tokens // rows,),
        in_specs=[row_spec(d), _const_spec(nmix.shape), _const_spec(win.shape), tab_spec, tab_spec],
        out_specs=[row_spec(nq), row_spec(nq), row_spec(nv), row_spec(nv)],
        out_shape=[jax.ShapeDtypeStruct((tokens, nq), BF16), jax.ShapeDtypeStruct((tokens, nq), BF16),
                   jax.ShapeDtypeStruct((tokens, nv), BF16), jax.ShapeDtypeStruct((tokens, nv), BF16)],
        compiler_params=_params(("parallel",)),
        name="odd_in",
    )(x2, nmix, win, cos, sin)


def _retention_kernel(q_ref, k_ref, v_ref, g_ref, ron_ref, o_ref, state_ref, decay_ref):
    tc = q_ref.shape[0]
    j = pl.program_id(1)

    def log_gamma(hd):
        return jnp.log(jnp.full((1, 1), 1.0 - 2.0 ** (-5.0 - hd), F32))

    @pl.when(j == 0)
    def _():
        state_ref[...] = jnp.zeros_like(state_ref)
        diff = (lax.broadcasted_iota(jnp.int32, (tc, tc), 0)
                - lax.broadcasted_iota(jnp.int32, (tc, tc), 1))
        dpos = jnp.maximum(diff, 0).astype(F32)
        for hd in range(RET_HEADS):
            decay_ref[hd] = jnp.where(diff >= 0, jnp.exp(log_gamma(hd) * dpos), 0.0)

    pos = lax.broadcasted_iota(jnp.int32, (tc, 1), 0).astype(F32)
    for hd in range(RET_HEADS):
        lg = log_gamma(hd)
        xi = jnp.exp(lg * (pos + 1.0))
        zeta = jnp.exp(lg * (tc - 1.0 - pos))
        g_chunk = jnp.exp(lg * float(tc))
        qs = slice(hd * RET_QK, (hd + 1) * RET_QK)
        vs = slice(hd * RET_V, (hd + 1) * RET_V)
        qh, kh, vh = q_ref[:, qs], k_ref[:, qs], v_ref[:, vs]
        state = state_ref[hd]
        a = lax.dot_general(qh, kh, (((1,), (1,)), ((), ())), preferred_element_type=F32) * decay_ref[hd]
        o = (jnp.dot(a.astype(BF16), vh, preferred_element_type=F32)
             + jnp.dot(qh, state.astype(BF16), preferred_element_type=F32) * xi)
        kz = (kh.astype(F32) * zeta).astype(BF16)
        state_ref[hd] = state * g_chunk + lax.dot_general(
            kz, vh, (((0,), (0,)), ((), ())), preferred_element_type=F32)
        on = o * _rms_scale(o) * ron_ref[:, vs]
        o_ref[:, vs] = (on * g_ref[:, vs].astype(F32)).astype(o_ref.dtype)


def _retention(q3, k3, v3, g3, ron):
    b, s, nq = q3.shape
    nv = v3.shape[2]
    tc = RET_ROWS
    blk = lambda w: pl.BlockSpec((None, tc, w), lambda bi, j: (bi, j, 0))
    return pl.pallas_call(
        _retention_kernel,
        grid=(b, s // tc),
        in_specs=[blk(nq), blk(nq), blk(nv), blk(nv), _const_spec(ron.shape)],
        out_specs=blk(nv),
        out_shape=jax.ShapeDtypeStruct((b, s, nv), BF16),
        scratch_shapes=[pltpu.VMEM((RET_HEADS, RET_QK, RET_V), F32),
                        pltpu.VMEM((RET_HEADS, tc, tc), F32)],
        compiler_params=_params(("arbitrary", "arbitrary")),
        name="retention",
    )(q3, k3, v3, g3, ron)


def _rope_angles(seq, half):
    inv = ROPE_THETA ** (-jnp.arange(half, dtype=F32) / half)
    return jnp.arange(seq, dtype=F32)[:, None] * inv[None, :]


def _head_slot_tables(seq):
    half = MLA_ROPE // 2
    ang = _rope_angles(seq, half)
    cos, sin = jnp.cos(ang), jnp.sin(ang)
    zeros = lambda w: jnp.zeros((seq, w), F32)
    pad = LANES - MLA_QK
    cosm = jnp.concatenate([jnp.ones((seq, MLA_NOPE), F32), cos, cos, zeros(pad)], axis=1)
    sina = jnp.concatenate([zeros(MLA_NOPE), -sin, zeros(half + pad)], axis=1)
    sinb = jnp.concatenate([zeros(MLA_NOPE + half), sin, zeros(pad)], axis=1)
    return cosm, sina, sinb


def _pad_heads(w, heads, width):
    r = w.shape[0]
    w3 = w.reshape(r, heads, width)
    return jnp.pad(w3, ((0, 0), (0, 0), (0, LANES - width))).reshape(r, heads * LANES)


def kernel(x, norm_mix, norm_ffn, even_w_in, gm_v_norm, gm_w_s, gm_b_s, mla_q_a_norm, mla_w_uq,
           mla_kv_a_norm, mla_w_ukv, mla_q_norm, mla_k_norm, even_w_out, odd_w_in, ret_out_norm,
           odd_w_out, ffn_w_gate, ffn_w_up, ffn_w_down):
    b, s, d = x.shape
    tokens = b * s
    x2 = x.reshape(tokens, d)
    row = lambda t: t.reshape(1, -1)

    o4 = 2 * GM_WIDTH + MLA_Q_RANK + MLA_KV_RANK
    w_in = even_w_in[0]
    win = jnp.concatenate([w_in[:, :o4], jnp.zeros((d, MLA_NOPE), F32), w_in[:, o4:],
                           jnp.zeros((d, LANES - MLA_QK), F32)], axis=1).astype(BF16)
    wuq = _pad_heads(mla_w_uq[0], MLA_HEADS, MLA_QK).astype(BF16)
    wukv3 = mla_w_ukv[0].reshape(MLA_KV_RANK, MLA_HEADS, MLA_NOPE + MLA_V)
    wukv = jnp.concatenate(
        [_pad_heads(wukv3[:, :, :MLA_NOPE].reshape(MLA_KV_RANK, -1), MLA_HEADS, MLA_NOPE),
         wukv3[:, :, MLA_NOPE:].reshape(MLA_KV_RANK, -1)], axis=1).astype(BF16)
    qn = jnp.pad(mla_q_norm[0], (0, LANES - MLA_QK)).reshape(1, LANES)
    kn = jnp.pad(mla_k_norm[0], (0, LANES - MLA_QK)).reshape(1, LANES)
    cosm, sina, sinb = _head_slot_tables(s)
    a, q, k, v = _even_in(x2, row(norm_mix[0]), win, row(gm_v_norm[0]), gm_w_s[0], gm_b_s[0].T,
                          row(mla_q_a_norm[0]), wuq, row(mla_kv_a_norm[0]), wukv, qn, kn,
                          cosm, sina, sinb, s)
    hw = MLA_HEADS * LANES
    o = _attention(q.reshape(b, s, hw), k.reshape(b, s, hw), v.reshape(b, s, MLA_HEADS * MLA_V))
    x2 = _proj_ffn([a, o.reshape(tokens, -1)], even_w_out[0].astype(BF16), x2, row(norm_ffn[0]),
                   ffn_w_gate[0].astype(BF16), ffn_w_up[0].astype(BF16), ffn_w_down[0].astype(BF16))

    ang = _rope_angles(s, RET_QK // 2)
    rq, rk, rv, rg = _odd_in(x2, row(norm_mix[1]), odd_w_in[0].astype(BF16), jnp.cos(ang), jnp.sin(ang), s)
    nq = RET_HEADS * RET_QK
    nv = RET_HEADS * RET_V
    og = _retention(rq.reshape(b, s, nq), rk.reshape(b, s, nq), rv.reshape(b, s, nv),
                    rg.reshape(b, s, nv), row(ret_out_norm[0]))
    x2 = _proj_ffn([og.reshape(tokens, nv)], odd_w_out[0].astype(BF16), x2, row(norm_ffn[1]),
                   ffn_w_gate[1].astype(BF16), ffn_w_up[1].astype(BF16), ffn_w_down[1].astype(BF16))
    return x2.reshape(b, s, d)
```

```python
import functools
import math

import jax
import jax.numpy as jnp
from jax import lax
from jax.experimental import pallas as pl
from jax.experimental.pallas import tpu as pltpu

F32 = jnp.float32
BF16 = jnp.bfloat16

EPS = 1e-6
ROPE_THETA = 10000.0
CHUNK = 64

GM_GROUPS = 4
GM_GROUP_DIM = 128
GM_WIDTH = GM_GROUPS * GM_GROUP_DIM
GM_BLOCK = 128
MLA_HEADS = 8
MLA_Q_RANK = 384
MLA_KV_RANK = 256
MLA_NOPE = 64
MLA_ROPE = 32
MLA_V = 64
MLA_QK = MLA_NOPE + MLA_ROPE
RET_HEADS = 4
RET_QK = 256
RET_V = 512

LANES = 128
V7X_VMEM_LIMIT = 56 * 1024 * 1024

EVEN_IN_ROWS = 256
ATTN_Q_ROWS = 512
ATTN_KV_ROWS = 512
FFN_ROWS = 512
ODD_IN_ROWS = 512
RET_ROWS = 512
FF_CHUNKS = (768, 768, 768, 512)

NT_DIMS = (((1,), (1,)), ((), ()))


def _rms_scale(t):
    return lax.rsqrt(jnp.mean(t * t, axis=-1, keepdims=True) + EPS)


def _gelu_tanh(t):
    return 0.5 * t * (1.0 + jnp.tanh(math.sqrt(2.0 / math.pi) * (t + 0.044715 * (t * t * t))))


def _silu(t):
    return t * (1.0 / (1.0 + jnp.exp(-t)))


def _const_spec(shape):
    nd = len(shape)
    return pl.BlockSpec(shape, lambda *_: (0,) * nd, pipeline_mode=pl.Buffered(1))


def _params(sem):
    return pltpu.CompilerParams(dimension_semantics=sem, vmem_limit_bytes=V7X_VMEM_LIMIT)


def _even_in_kernel(x_ref, nmix_ref, wuv_ref, wct_ref, gvn_ref, ws_ref, bs_ref, gqa_ref, wuqt_ref,
                    gkva_ref, wukvt_ref, gq_ref, gk_ref, cos_ref, sin_ref,
                    a_ref, qt_ref, k_ref, vt_ref):
    rows = x_ref.shape[0]
    x = x_ref[...]
    h = (x * _rms_scale(x) * nmix_ref[...]).astype(BF16)

    zuv = jnp.dot(h, wuv_ref[...], preferred_element_type=F32)
    u = _gelu_tanh(zuv[:, :GM_WIDTH])
    v = _gelu_tanh(zuv[:, GM_WIDTH:])
    t_out = lax.broadcasted_iota(jnp.int32, (GM_BLOCK, GM_BLOCK), 0) // CHUNK
    t_in = lax.broadcasted_iota(jnp.int32, (GM_BLOCK, GM_BLOCK), 1) // CHUNK
    causal = t_in <= t_out
    nblk = rows // GM_BLOCK
    for g in range(GM_GROUPS):
        cs = slice(g * GM_GROUP_DIM, (g + 1) * GM_GROUP_DIM)
        vg = v[:, cs]
        vn = (vg * _rms_scale(vg) * gvn_ref[:, cs]).astype(BF16)
        wg = jnp.where(causal, ws_ref[g], 0.0).astype(BF16)
        vcat = jnp.concatenate([vn[j * GM_BLOCK:(j + 1) * GM_BLOCK] for j in range(nblk)], axis=1)
        s = jnp.dot(wg, vcat, preferred_element_type=F32) + bs_ref[:, g:g + 1]
        for j in range(nblk):
            rs = slice(j * GM_BLOCK, (j + 1) * GM_BLOCK)
            a_ref[rs, cs] = (u[rs, cs] * s[:, j * GM_BLOCK:(j + 1) * GM_BLOCK]).astype(a_ref.dtype)

    zc = lax.dot_general(wct_ref[...], h, NT_DIMS, preferred_element_type=F32)
    o1 = MLA_Q_RANK
    o2 = o1 + MLA_KV_RANK
    cq, ckv, kpe = zc[:o1], zc[o1:o2], zc[o2:]

    def col_rms(t, n):
        return lax.rsqrt(jnp.sum(t * t, axis=0, keepdims=True) * (1.0 / n) + EPS)

    cqn = (cq * col_rms(cq, MLA_Q_RANK) * gqa_ref[...]).astype(BF16)
    ckvn = (ckv * col_rms(ckv, MLA_KV_RANK) * gkva_ref[...]).astype(BF16)
    qt = jnp.dot(wuqt_ref[...], cqn, preferred_element_type=F32)
    kvt = jnp.dot(wukvt_ref[...], ckvn, preferred_element_type=F32)

    cos, sin = cos_ref[...], sin_ref[...]
    half = MLA_ROPE // 2
    n1, n2 = MLA_NOPE, MLA_NOPE + half

    def rope(x1, x2):
        return x1 * cos - x2 * sin, x2 * cos + x1 * sin

    gq = gq_ref[...] * (MLA_QK ** -0.5 * math.log2(math.e))
    for hd in range(MLA_HEADS):
        t = qt[hd * LANES:(hd + 1) * LANES]
        r = lax.rsqrt(jnp.sum(t * t, axis=0, keepdims=True) * (1.0 / MLA_QK) + EPS)
        tn = t * gq
        r1, r2 = rope(tn[n1:n2], tn[n2:MLA_QK])
        base = hd * LANES
        qt_ref[base:base + n1, :] = (tn[:n1] * r).astype(qt_ref.dtype)
        qt_ref[base + n1:base + n2, :] = (r1 * r).astype(qt_ref.dtype)
        qt_ref[base + n2:base + MLA_QK, :] = (r2 * r).astype(qt_ref.dtype)
        qt_ref[base + MLA_QK:base + LANES, :] = jnp.zeros((LANES - MLA_QK, rows), qt_ref.dtype)

    gk = gk_ref[...]
    kpe_ss = jnp.sum(kpe * kpe, axis=0, keepdims=True)
    kr1, kr2 = rope(kpe[:half] * gk[n1:n2], kpe[half:] * gk[n2:MLA_QK])
    pad = jnp.zeros((LANES - MLA_QK, rows), F32)
    for hd in range(MLA_HEADS):
        t = kvt[hd * MLA_NOPE:(hd + 1) * MLA_NOPE]
        r = lax.rsqrt((jnp.sum(t * t, axis=0, keepdims=True) + kpe_ss) * (1.0 / MLA_QK) + EPS)
        kt = jnp.concatenate([t * gk[:n1] * r, kr1 * r, kr2 * r, pad], axis=0)
        k_ref[:, hd * LANES:(hd + 1) * LANES] = kt.T.astype(k_ref.dtype)
    vt_ref[0] = kvt[MLA_HEADS * MLA_NOPE:].astype(vt_ref.dtype)


def _even_in(x2, nmix, wuv, wct, gvn, ws, bs_t, gqa, wuqt, gkva, wukvt, gq, gk, cos_t, sin_t, seq):
    tokens, d = x2.shape
    rows = EVEN_IN_ROWS
    per_seq = seq // rows
    row_spec = lambda w: pl.BlockSpec((rows, w), lambda i: (i, 0))
    tab_spec = pl.BlockSpec((MLA_ROPE // 2, rows), lambda i: (0, i % per_seq))
    hw = MLA_HEADS * LANES
    vw = MLA_HEADS * MLA_V
    consts = [nmix, wuv, wct, gvn, ws, bs_t, gqa, wuqt, gkva, wukvt, gq, gk]
    return pl.pallas_call(
        _even_in_kernel,
        grid=(tokens // rows,),
        in_specs=[row_spec(d)] + [_const_spec(c.shape) for c in consts] + [tab_spec, tab_spec],
        out_specs=[row_spec(GM_WIDTH),
                   pl.BlockSpec((hw, rows), lambda i: (0, i)),
                   row_spec(hw),
                   pl.BlockSpec((1, vw, rows), lambda i: (i, 0, 0))],
        out_shape=[jax.ShapeDtypeStruct((tokens, GM_WIDTH), BF16),
                   jax.ShapeDtypeStruct((hw, tokens), BF16),
                   jax.ShapeDtypeStruct((tokens, hw), BF16),
                   jax.ShapeDtypeStruct((tokens // rows, vw, rows), BF16)],
        compiler_params=_params(("parallel",)),
        name="even_in",
    )(x2, *consts, cos_t, sin_t)


def _attn_kernel(qt_ref, k_ref, vt_ref, o_ref, sa_ref, sb_ref, m_ref, l_ref, acc_ref):
    tq = qt_ref.shape[1]
    tk = ATTN_KV_ROWS
    vrows = vt_ref.shape[2]
    i = pl.program_id(2)

    m_ref[...] = jnp.full(m_ref.shape, -jnp.inf, F32)
    l_ref[...] = jnp.zeros(l_ref.shape, F32)
    acc_ref[...] = jnp.zeros(acc_ref.shape, F32)

    def scores(j, dst):
        start = pl.multiple_of(j * tk, tk)
        for hh in range(2):
            dst[hh] = jnp.dot(k_ref[pl.ds(start, tk), hh * LANES:(hh + 1) * LANES],
                              qt_ref[hh * LANES:(hh + 1) * LANES, :], preferred_element_type=F32)

    def update(j, src, masked):
        for hh in range(2):
            s = src[hh]
            if masked:
                ck = lax.broadcasted_iota(jnp.int32, (tk, tq), 0) // CHUNK
                cq = lax.broadcasted_iota(jnp.int32, (tk, tq), 1) // CHUNK
                s = jnp.where(ck <= cq, s, -jnp.inf)
            m = m_ref[hh]
            m_new = jnp.maximum(m, jnp.max(s, axis=0, keepdims=True))
            alpha = jnp.exp2(m - m_new)
            p32 = jnp.exp2(s - m_new)
            p = p32.astype(BF16)
            m_ref[hh] = m_new
            l_ref[hh] = alpha * l_ref[hh] + jnp.sum(p32, axis=0, keepdims=True)
            pv = None
            for c in range(tk // vrows):
                vt = vt_ref[j * (tk // vrows) + c, hh * MLA_V:(hh + 1) * MLA_V, :]
                part = jnp.dot(vt, p[c * vrows:(c + 1) * vrows], preferred_element_type=F32)
                pv = part if pv is None else pv + part
            acc_ref[hh] = alpha * acc_ref[hh] + pv

    scores(0, sa_ref)

    def two_blocks(jj, carry):
        j0 = 2 * jj
        scores(j0 + 1, sb_ref)
        update(j0, sa_ref, False)
        scores(j0 + 2, sa_ref)
        update(j0 + 1, sb_ref, False)
        return carry

    lax.fori_loop(0, i // 2, two_blocks, 0)

    @pl.when(i % 2 == 1)
    def _():
        scores(i, sb_ref)
        update(i - 1, sa_ref, False)
        update(i, sb_ref, True)

    @pl.when(i % 2 == 0)
    def _():
        update(i, sa_ref, True)

    ot = jnp.concatenate([acc_ref[hh] * (1.0 / l_ref[hh]) for hh in range(2)], axis=0)
    o_ref[...] = ot.T.astype(o_ref.dtype)


def _attention(qt, k3, vt, b, s):
    tq = ATTN_Q_ROWS
    assert tq == ATTN_KV_ROWS
    pairs = MLA_HEADS // 2
    nq = s // tq
    vchunks, _, vrows = vt.shape
    return pl.pallas_call(
        _attn_kernel,
        grid=(b, pairs, nq),
        in_specs=[pl.BlockSpec((2 * LANES, tq), lambda bi, p, i: (p, bi * nq + i)),
                  pl.BlockSpec((None, s, 2 * LANES), lambda bi, p, i: (bi, 0, p)),
                  pl.BlockSpec((vchunks // b, 2 * MLA_V, vrows), lambda bi, p, i: (bi, p, 0))],
        out_specs=pl.BlockSpec((None, tq, LANES), lambda bi, p, i: (bi, i, p)),
        out_shape=jax.ShapeDtypeStruct((b, s, MLA_HEADS * MLA_V), BF16),
        scratch_shapes=[pltpu.VMEM((2, ATTN_KV_ROWS, tq), F32),
                        pltpu.VMEM((2, ATTN_KV_ROWS, tq), F32),
                        pltpu.VMEM((2, 1, tq), F32),
                        pltpu.VMEM((2, 1, tq), F32),
                        pltpu.VMEM((2, MLA_V, tq), F32)],
        compiler_params=_params(("parallel", "parallel", "parallel")),
        name="attention",
    )(qt, k3, vt)


def _proj_ffn_kernel(*refs, n_in):
    in_refs = refs[:n_in]
    wout_ref, x_ref, nffn_ref, wg_ref, wu_ref, wd_ref, out_ref = refs[n_in:]
    mix = None
    off = 0
    for r in in_refs:
        kdim = r.shape[1]
        part = jnp.dot(r[...], wout_ref[off:off + kdim, :], preferred_element_type=F32)
        mix = part if mix is None else mix + part
        off += kdim
    x1 = x_ref[...] + mix
    h = (x1 * _rms_scale(x1) * nffn_ref[...]).astype(BF16)
    acc = x1
    c0 = 0
    for c in FF_CHUNKS:
        g = jnp.dot(h, wg_ref[:, c0:c0 + c], preferred_element_type=F32)
        u = jnp.dot(h, wu_ref[:, c0:c0 + c], preferred_element_type=F32)
        act = (_silu(g) * u).astype(BF16)
        acc = acc + jnp.dot(act, wd_ref[c0:c0 + c, :], preferred_element_type=F32)
        c0 += c
    out_ref[...] = acc


def _proj_ffn(mix_ins, wout, x2, nffn, wg, wu, wd):
    tokens, d = x2.shape
    rows = FFN_ROWS
    assert sum(FF_CHUNKS) == wg.shape[1]
    row_spec = lambda w: pl.BlockSpec((rows, w), lambda i: (i, 0))
    return pl.pallas_call(
        functools.partial(_proj_ffn_kernel, n_in=len(mix_ins)),
        grid=(tokens // rows,),
        in_specs=[row_spec(m.shape[1]) for m in mix_ins]
        + [_const_spec(wout.shape), row_spec(d), _const_spec(nffn.shape),
           _const_spec(wg.shape), _const_spec(wu.shape), _const_spec(wd.shape)],
        out_specs=row_spec(d),
        out_shape=jax.ShapeDtypeStruct((tokens, d), F32),
        compiler_params=_params(("parallel",)),
        name="proj_ffn",
    )(*mix_ins, wout, x2, nffn, wg, wu, wd)


def _odd_in_kernel(x_ref, nmix_ref, win_ref, cos_ref, sin_ref, q_ref, k_ref, v_ref, g_ref):
    x = x_ref[...]
    h = (x * _rms_scale(x) * nmix_ref[...]).astype(BF16)
    cos, sin = cos_ref[...], sin_ref[...]
    nq = RET_HEADS * RET_QK
    nv = RET_HEADS * RET_V
    half = RET_QK // 2

    def rope_store(dst, col0, scale):
        for hd in range(RET_HEADS):
            c = col0 + hd * RET_QK
            t = jnp.dot(h, win_ref[:, c:c + RET_QK], preferred_element_type=F32)
            t1, t2 = t[:, :half], t[:, half:]
            dst[:, hd * RET_QK:hd * RET_QK + half] = ((t1 * cos - t2 * sin) * scale).astype(dst.dtype)
            dst[:, hd * RET_QK + half:(hd + 1) * RET_QK] = ((t2 * cos + t1 * sin) * scale).astype(dst.dtype)

    rope_store(q_ref, 0, 1.0)
    rope_store(k_ref, nq, RET_QK ** -0.5)
    for hd in range(RET_HEADS):
        cs = slice(hd * RET_V, (hd + 1) * RET_V)
        v_ref[:, cs] = jnp.dot(h, win_ref[:, 2 * nq + hd * RET_V:2 * nq + (hd + 1) * RET_V],
                               preferred_element_type=F32).astype(v_ref.dtype)
        gate = jnp.dot(h, win_ref[:, 2 * nq + nv + hd * RET_V:2 * nq + nv + (hd + 1) * RET_V],
                       preferred_element_type=F32)
        g_ref[:, cs] = _silu(gate).astype(g_ref.dtype)


def _odd_in(x2, nmix, win, cos, sin, seq):
    tokens, d = x2.shape
    rows = ODD_IN_ROWS
    per_seq = seq // rows
    nq = RET_HEADS * RET_QK
    nv = RET_HEADS * RET_V
    row_spec = lambda w: pl.BlockSpec((rows, w), lambda i: (i, 0))
    tab_spec = pl.BlockSpec((rows, RET_QK // 2), lambda i: (i % per_seq, 0))
    return pl.pallas_call(
        _odd_in_kernel,
        grid=(tokens // rows,),
        in_specs=[row_spec(d), _const_spec(nmix.shape), _const_spec(win.shape), tab_spec, tab_spec],
        out_specs=[row_spec(nq), row_spec(nq), row_spec(nv), row_spec(nv)],
        out_shape=[jax.ShapeDtypeStruct((tokens, nq), BF16), jax.ShapeDtypeStruct((tokens, nq), BF16),
                   jax.ShapeDtypeStruct((tokens, nv), BF16), jax.ShapeDtypeStruct((tokens, nv), BF16)],
        compiler_params=_params(("parallel",)),
        name="odd_in",
    )(x2, nmix, win, cos, sin)


def _retention_kernel(q_ref, k_ref, v_ref, g_ref, ron_ref, o_ref, state_ref, decay_ref):
    tc = q_ref.shape[0]
    j = pl.program_id(1)

    def log_gamma(hd):
        return jnp.log(jnp.full((1, 1), 1.0 - 2.0 ** (-5.0 - hd), F32))

    @pl.when(j == 0)
    def _():
        state_ref[...] = jnp.zeros_like(state_ref)
        diff = (lax.broadcasted_iota(jnp.int32, (tc, tc), 0)
                - lax.broadcasted_iota(jnp.int32, (tc, tc), 1))
        dpos = jnp.maximum(diff, 0).astype(F32)
        for hd in range(RET_HEADS):
            decay_ref[hd] = jnp.where(diff >= 0, jnp.exp(log_gamma(hd) * dpos), 0.0)

    pos = lax.broadcasted_iota(jnp.int32, (tc, 1), 0).astype(F32)
    for hd in range(RET_HEADS):
        lg = log_gamma(hd)
        xi = jnp.exp(lg * (pos + 1.0))
        zeta = jnp.exp(lg * (tc - 1.0 - pos))
        g_chunk = jnp.exp(lg * float(tc))
        qs = slice(hd * RET_QK, (hd + 1) * RET_QK)
        vs = slice(hd * RET_V, (hd + 1) * RET_V)
        qh, kh, vh = q_ref[:, qs], k_ref[:, qs], v_ref[:, vs]
        state = state_ref[hd]
        a = lax.dot_general(qh, kh, NT_DIMS, preferred_element_type=F32) * decay_ref[hd]
        o = (jnp.dot(a.astype(BF16), vh, preferred_element_type=F32)
             + jnp.dot(qh, state.astype(BF16), preferred_element_type=F32) * xi)
        kz = (kh.astype(F32) * zeta).astype(BF16)
        state_ref[hd] = state * g_chunk + lax.dot_general(
            kz, vh, (((0,), (0,)), ((), ())), preferred_element_type=F32)
        on = o * _rms_scale(o) * ron_ref[:, vs]
        o_ref[:, vs] = (on * g_ref[:, vs].astype(F32)).astype(o_ref.dtype)


def _retention(q3, k3, v3, g3, ron):
    b, s, nq = q3.shape
    nv = v3.shape[2]
    tc = RET_ROWS
    blk = lambda w: pl.BlockSpec((None, tc, w), lambda bi, j: (bi, j, 0))
    return pl.pallas_call(
        _retention_kernel,
        grid=(b, s // tc),
        in_specs=[blk(nq), blk(nq), blk(nv), blk(nv), _const_spec(ron.shape)],
        out_specs=blk(nv),
        out_shape=jax.ShapeDtypeStruct((b, s, nv), BF16),
        scratch_shapes=[pltpu.VMEM((RET_HEADS, RET_QK, RET_V), F32),
                        pltpu.VMEM((RET_HEADS, tc, tc), F32)],
        compiler_params=_params(("arbitrary", "arbitrary")),
        name="retention",
    )(q3, k3, v3, g3, ron)


def _rope_angles(seq, half):
    inv = ROPE_THETA ** (-jnp.arange(half, dtype=F32) / half)
    return jnp.arange(seq, dtype=F32)[:, None] * inv[None, :]


def _lane_bcast(g, width):
    return jnp.broadcast_to(g[:, None], (g.shape[0], width))


def kernel(x, norm_mix, norm_ffn, even_w_in, gm_v_norm, gm_w_s, gm_b_s, mla_q_a_norm, mla_w_uq,
           mla_kv_a_norm, mla_w_ukv, mla_q_norm, mla_k_norm, even_w_out, odd_w_in, ret_out_norm,
           odd_w_out, ffn_w_gate, ffn_w_up, ffn_w_down):
    b, s, d = x.shape
    tokens = b * s
    x2 = x.reshape(tokens, d)
    row = lambda t: t.reshape(1, -1)

    w_in = even_w_in[0]
    wuv = w_in[:, :2 * GM_WIDTH].astype(BF16)
    wct = w_in[:, 2 * GM_WIDTH:].T.astype(BF16)
    wuq3 = mla_w_uq[0].reshape(MLA_Q_RANK, MLA_HEADS, MLA_QK)
    wuqt = jnp.pad(wuq3, ((0, 0), (0, 0), (0, LANES - MLA_QK))).reshape(MLA_Q_RANK, -1).T.astype(BF16)
    wukv3 = mla_w_ukv[0].reshape(MLA_KV_RANK, MLA_HEADS, MLA_NOPE + MLA_V)
    wukvt = jnp.concatenate([wukv3[:, :, :MLA_NOPE].reshape(MLA_KV_RANK, -1),
                             wukv3[:, :, MLA_NOPE:].reshape(MLA_KV_RANK, -1)], axis=1).T.astype(BF16)
    er = EVEN_IN_ROWS
    pad_head = lambda g: jnp.pad(g, (0, LANES - MLA_QK))
    ang = _rope_angles(s, MLA_ROPE // 2)
    a, qt, k, vt = _even_in(
        x2, row(norm_mix[0]), wuv, wct, row(gm_v_norm[0]), gm_w_s[0], gm_b_s[0].T,
        _lane_bcast(mla_q_a_norm[0], er), wuqt, _lane_bcast(mla_kv_a_norm[0], er), wukvt,
        _lane_bcast(pad_head(mla_q_norm[0]), er), _lane_bcast(pad_head(mla_k_norm[0]), er),
        jnp.cos(ang).T, jnp.sin(ang).T, s)
    o = _attention(qt, k.reshape(b, s, MLA_HEADS * LANES), vt, b, s)
    x2 = _proj_ffn([a, o.reshape(tokens, -1)], even_w_out[0].astype(BF16), x2, row(norm_ffn[0]),
                   ffn_w_gate[0].astype(BF16), ffn_w_up[0].astype(BF16), ffn_w_down[0].astype(BF16))

    ang = _rope_angles(s, RET_QK // 2)
    rq, rk, rv, rg = _odd_in(x2, row(norm_mix[1]), odd_w_in[0].astype(BF16), jnp.cos(ang), jnp.sin(ang), s)
    nq = RET_HEADS * RET_QK
    nv = RET_HEADS * RET_V
    og = _retention(rq.reshape(b, s, nq), rk.reshape(b, s, nq), rv.reshape(b, s, nv),
                    rg.reshape(b, s, nv), row(ret_out_norm[0]))
    x2 = _proj_ffn([og.reshape(tokens, nv)], odd_w_out[0].astype(BF16), x2, row(norm_ffn[1]),
                   ffn_w_gate[1].astype(BF16), ffn_w_up[1].astype(BF16), ffn_w_down[1].astype(BF16))
    return x2.reshape(b, s, d)
```

```python
import functools
import math

import jax
import jax.numpy as jnp
from jax import lax
from jax.experimental import pallas as pl
from jax.experimental.pallas import tpu as pltpu

F32 = jnp.float32
BF16 = jnp.bfloat16

EPS = 1e-6
ROPE_THETA = 10000.0
CHUNK = 64

GM_GROUPS = 4
GM_GROUP_DIM = 128
GM_WIDTH = GM_GROUPS * GM_GROUP_DIM
GM_BLOCK = 128
MLA_HEADS = 8
MLA_Q_RANK = 384
MLA_KV_RANK = 256
MLA_NOPE = 64
MLA_ROPE = 32
MLA_V = 64
MLA_QK = MLA_NOPE + MLA_ROPE
RET_HEADS = 4
RET_QK = 256
RET_V = 512

LANES = 128
ONES_ROWS = 16
V7X_VMEM_LIMIT = 56 * 1024 * 1024

EVEN_IN_ROWS = 256
ATTN_Q_ROWS = 512
ATTN_KV_ROWS = 512
FFN_ROWS = 512
ODD_IN_ROWS = 512
RET_ROWS = 512
FF_CHUNKS = (768, 768, 768, 512)

NT_DIMS = (((1,), (1,)), ((), ()))


def _rms_scale(t):
    return lax.rsqrt(jnp.mean(t * t, axis=-1, keepdims=True) + EPS)


def _gelu_tanh(t):
    return 0.5 * t * (1.0 + jnp.tanh(math.sqrt(2.0 / math.pi) * (t + 0.044715 * (t * t * t))))


def _silu(t):
    return t * (1.0 / (1.0 + jnp.exp(-t)))


def _const_spec(shape):
    nd = len(shape)
    return pl.BlockSpec(shape, lambda *_: (0,) * nd, pipeline_mode=pl.Buffered(1))


def _params(sem):
    return pltpu.CompilerParams(dimension_semantics=sem, vmem_limit_bytes=V7X_VMEM_LIMIT)


def _even_in_kernel(x_ref, nmix_ref, wuv_ref, wct_ref, gvn_ref, ws_ref, bs_ref, gqa_ref, wuqt_ref,
                    gkva_ref, wukvt_ref, gq_ref, gk_ref, cos_ref, sin_ref,
                    a_ref, qt_ref, k_ref, vt_ref):
    rows = x_ref.shape[0]
    x = x_ref[...]
    h = (x * _rms_scale(x) * nmix_ref[...]).astype(BF16)

    zuv = jnp.dot(h, wuv_ref[...], preferred_element_type=F32)
    u = _gelu_tanh(zuv[:, :GM_WIDTH])
    v = _gelu_tanh(zuv[:, GM_WIDTH:])
    t_out = lax.broadcasted_iota(jnp.int32, (GM_BLOCK, GM_BLOCK), 0) // CHUNK
    t_in = lax.broadcasted_iota(jnp.int32, (GM_BLOCK, GM_BLOCK), 1) // CHUNK
    causal = t_in <= t_out
    nblk = rows // GM_BLOCK
    for g in range(GM_GROUPS):
        cs = slice(g * GM_GROUP_DIM, (g + 1) * GM_GROUP_DIM)
        vg = v[:, cs]
        vn = (vg * _rms_scale(vg) * gvn_ref[:, cs]).astype(BF16)
        wg = jnp.where(causal, ws_ref[g], 0.0).astype(BF16)
        vcat = jnp.concatenate([vn[j * GM_BLOCK:(j + 1) * GM_BLOCK] for j in range(nblk)], axis=1)
        s = jnp.dot(wg, vcat, preferred_element_type=F32) + bs_ref[:, g:g + 1]
        for j in range(nblk):
            rs = slice(j * GM_BLOCK, (j + 1) * GM_BLOCK)
            a_ref[rs, cs] = (u[rs, cs] * s[:, j * GM_BLOCK:(j + 1) * GM_BLOCK]).astype(a_ref.dtype)

    zc = lax.dot_general(wct_ref[...], h, NT_DIMS, preferred_element_type=F32)
    o1 = MLA_Q_RANK
    o2 = o1 + MLA_KV_RANK
    cq, ckv, kpe = zc[:o1], zc[o1:o2], zc[o2:]

    def col_rms(t, n):
        return lax.rsqrt(jnp.sum(t * t, axis=0, keepdims=True) * (1.0 / n) + EPS)

    cqn = (cq * col_rms(cq, MLA_Q_RANK) * gqa_ref[...]).astype(BF16)
    ckvn = (ckv * col_rms(ckv, MLA_KV_RANK) * gkva_ref[...]).astype(BF16)
    qt = jnp.dot(wuqt_ref[...], cqn, preferred_element_type=F32)
    kvt = jnp.dot(wukvt_ref[...], ckvn, preferred_element_type=F32)

    cos, sin = cos_ref[...], sin_ref[...]
    half = MLA_ROPE // 2
    n1, n2 = MLA_NOPE, MLA_NOPE + half

    def rope(x1, x2):
        return x1 * cos - x2 * sin, x2 * cos + x1 * sin

    gq = gq_ref[...] * (MLA_QK ** -0.5 * math.log2(math.e))
    for hd in range(MLA_HEADS):
        t = qt[hd * LANES:(hd + 1) * LANES]
        r = lax.rsqrt(jnp.sum(t * t, axis=0, keepdims=True) * (1.0 / MLA_QK) + EPS)
        tn = t * gq
        r1, r2 = rope(tn[n1:n2], tn[n2:MLA_QK])
        base = hd * LANES
        qt_ref[base:base + n1, :] = (tn[:n1] * r).astype(qt_ref.dtype)
        qt_ref[base + n1:base + n2, :] = (r1 * r).astype(qt_ref.dtype)
        qt_ref[base + n2:base + MLA_QK, :] = (r2 * r).astype(qt_ref.dtype)
        qt_ref[base + MLA_QK:base + LANES, :] = jnp.zeros((LANES - MLA_QK, rows), qt_ref.dtype)

    gk = gk_ref[...]
    kpe_ss = jnp.sum(kpe * kpe, axis=0, keepdims=True)
    kr1, kr2 = rope(kpe[:half] * gk[n1:n2], kpe[half:] * gk[n2:MLA_QK])
    pad = jnp.zeros((LANES - MLA_QK, rows), F32)
    for hd in range(MLA_HEADS):
        t = kvt[hd * MLA_NOPE:(hd + 1) * MLA_NOPE]
        r = lax.rsqrt((jnp.sum(t * t, axis=0, keepdims=True) + kpe_ss) * (1.0 / MLA_QK) + EPS)
        kt = jnp.concatenate([t * gk[:n1] * r, kr1 * r, kr2 * r, pad], axis=0)
        k_ref[:, hd * LANES:(hd + 1) * LANES] = kt.T.astype(k_ref.dtype)
    vt_ref[0] = kvt[MLA_HEADS * MLA_NOPE:].astype(vt_ref.dtype)


def _even_in(x2, nmix, wuv, wct, gvn, ws, bs_t, gqa, wuqt, gkva, wukvt, gq, gk, cos_t, sin_t, seq):
    tokens, d = x2.shape
    rows = EVEN_IN_ROWS
    per_seq = seq // rows
    row_spec = lambda w: pl.BlockSpec((rows, w), lambda i: (i, 0))
    tab_spec = pl.BlockSpec((MLA_ROPE // 2, rows), lambda i: (0, i % per_seq))
    hw = MLA_HEADS * LANES
    vw = MLA_HEADS * MLA_V
    consts = [nmix, wuv, wct, gvn, ws, bs_t, gqa, wuqt, gkva, wukvt, gq, gk]
    return pl.pallas_call(
        _even_in_kernel,
        grid=(tokens // rows,),
        in_specs=[row_spec(d)] + [_const_spec(c.shape) for c in consts] + [tab_spec, tab_spec],
        out_specs=[row_spec(GM_WIDTH),
                   pl.BlockSpec((hw, rows), lambda i: (0, i)),
                   row_spec(hw),
                   pl.BlockSpec((1, vw, rows), lambda i: (i, 0, 0))],
        out_shape=[jax.ShapeDtypeStruct((tokens, GM_WIDTH), BF16),
                   jax.ShapeDtypeStruct((hw, tokens), BF16),
                   jax.ShapeDtypeStruct((tokens, hw), BF16),
                   jax.ShapeDtypeStruct((tokens // rows, vw, rows), BF16)],
        compiler_params=_params(("parallel",)),
        name="even_in",
    )(x2, *consts, cos_t, sin_t)


def _attn_kernel(qt_ref, k_ref, vt_ref, o_ref, sa_ref, sb_ref, ma_ref, mb_ref, m_ref, acc_ref):
    tq = qt_ref.shape[1]
    tk = ATTN_KV_ROWS
    vrows = vt_ref.shape[2]
    i = pl.program_id(2)

    m_ref[...] = jnp.full(m_ref.shape, -jnp.inf, F32)
    acc_ref[...] = jnp.zeros(acc_ref.shape, F32)
    ones = jnp.ones((ONES_ROWS, vrows), BF16)

    def scores(j, dst, bmax, masked):
        start = pl.multiple_of(j * tk, tk)
        for hh in range(2):
            s = jnp.dot(k_ref[pl.ds(start, tk), hh * LANES:(hh + 1) * LANES],
                        qt_ref[hh * LANES:(hh + 1) * LANES, :], preferred_element_type=F32)
            if masked:
                ck = lax.broadcasted_iota(jnp.int32, (tk, tq), 0) // CHUNK
                cq = lax.broadcasted_iota(jnp.int32, (tk, tq), 1) // CHUNK
                s = jnp.where(ck <= cq, s, -jnp.inf)
            dst[hh] = s
            bmax[hh] = jnp.max(s, axis=0, keepdims=True)

    def update(j, src, bmax):
        for hh in range(2):
            m = m_ref[hh]
            m_new = jnp.maximum(m, bmax[hh])
            alpha = jnp.exp2(m - m_new)
            p = jnp.exp2(src[hh] - m_new).astype(BF16)
            m_ref[hh] = m_new
            pv = None
            for c in range(tk // vrows):
                vt = vt_ref[j * (tk // vrows) + c, hh * MLA_V:(hh + 1) * MLA_V, :]
                part = jnp.dot(jnp.concatenate([vt, ones], axis=0), p[c * vrows:(c + 1) * vrows],
                               preferred_element_type=F32)
                pv = part if pv is None else pv + part
            acc_ref[hh] = alpha * acc_ref[hh] + pv

    @pl.when(i > 0)
    def _():
        scores(0, sa_ref, ma_ref, False)

    def two_blocks(jj, carry):
        j0 = 2 * jj
        scores(j0 + 1, sb_ref, mb_ref, False)
        update(j0, sa_ref, ma_ref)
        scores(j0 + 2, sa_ref, ma_ref, False)
        update(j0 + 1, sb_ref, mb_ref)
        return carry

    lax.fori_loop(0, (i - 1) // 2, two_blocks, 0)

    @pl.when(i == 0)
    def _():
        scores(0, sa_ref, ma_ref, True)
        update(0, sa_ref, ma_ref)

    @pl.when(i % 2 == 1)
    def _():
        scores(i, sb_ref, mb_ref, True)
        update(i - 1, sa_ref, ma_ref)
        update(i, sb_ref, mb_ref)

    @pl.when(jnp.logical_and(i % 2 == 0, i > 0))
    def _():
        scores(i - 1, sb_ref, mb_ref, False)
        update(i - 2, sa_ref, ma_ref)
        scores(i, sa_ref, ma_ref, True)
        update(i - 1, sb_ref, mb_ref)
        update(i, sa_ref, ma_ref)

    ot = jnp.concatenate([acc_ref[hh, :MLA_V] * (1.0 / acc_ref[hh, MLA_V:MLA_V + 1]) for hh in range(2)],
                         axis=0)
    o_ref[...] = ot.T.astype(o_ref.dtype)


def _attention(qt, k3, vt, b, s):
    tq = ATTN_Q_ROWS
    assert tq == ATTN_KV_ROWS
    pairs = MLA_HEADS // 2
    nq = s // tq
    vchunks, _, vrows = vt.shape
    return pl.pallas_call(
        _attn_kernel,
        grid=(b, pairs, nq),
        in_specs=[pl.BlockSpec((2 * LANES, tq), lambda bi, p, i: (p, bi * nq + i)),
                  pl.BlockSpec((None, s, 2 * LANES), lambda bi, p, i: (bi, 0, p)),
                  pl.BlockSpec((vchunks // b, 2 * MLA_V, vrows), lambda bi, p, i: (bi, p, 0))],
        out_specs=pl.BlockSpec((None, tq, LANES), lambda bi, p, i: (bi, i, p)),
        out_shape=jax.ShapeDtypeStruct((b, s, MLA_HEADS * MLA_V), BF16),
        scratch_shapes=[pltpu.VMEM((2, ATTN_KV_ROWS, tq), F32),
                        pltpu.VMEM((2, ATTN_KV_ROWS, tq), F32),
                        pltpu.VMEM((2, 1, tq), F32),
                        pltpu.VMEM((2, 1, tq), F32),
                        pltpu.VMEM((2, 1, tq), F32),
                        pltpu.VMEM((2, MLA_V + ONES_ROWS, tq), F32)],
        compiler_params=_params(("parallel", "parallel", "parallel")),
        name="attention",
    )(qt, k3, vt)


def _proj_ffn_kernel(*refs, n_in):
    in_refs = refs[:n_in]
    wout_ref, x_ref, nffn_ref, wg_ref, wu_ref, wd_ref, out_ref = refs[n_in:]
    mix = None
    off = 0
    for r in in_refs:
        kdim = r.shape[1]
        part = jnp.dot(r[...], wout_ref[off:off + kdim, :], preferred_element_type=F32)
        mix = part if mix is None else mix + part
        off += kdim
    x1 = x_ref[...] + mix
    h = (x1 * _rms_scale(x1) * nffn_ref[...]).astype(BF16)
    acc = x1
    c0 = 0
    for c in FF_CHUNKS:
        g = jnp.dot(h, wg_ref[:, c0:c0 + c], preferred_element_type=F32)
        u = jnp.dot(h, wu_ref[:, c0:c0 + c], preferred_element_type=F32)
        act = (_silu(g) * u).astype(BF16)
        acc = acc + jnp.dot(act, wd_ref[c0:c0 + c, :], preferred_element_type=F32)
        c0 += c
    out_ref[...] = acc


def _proj_ffn(mix_ins, wout, x2, nffn, wg, wu, wd):
    tokens, d = x2.shape
    rows = FFN_ROWS
    assert sum(FF_CHUNKS) == wg.shape[1]
    row_spec = lambda w: pl.BlockSpec((rows, w), lambda i: (i, 0))
    return pl.pallas_call(
        functools.partial(_proj_ffn_kernel, n_in=len(mix_ins)),
        grid=(tokens // rows,),
        in_specs=[row_spec(m.shape[1]) for m in mix_ins]
        + [_const_spec(wout.shape), row_spec(d), _const_spec(nffn.shape),
           _const_spec(wg.shape), _const_spec(wu.shape), _const_spec(wd.shape)],
        out_specs=row_spec(d),
        out_shape=jax.ShapeDtypeStruct((tokens, d), F32),
        compiler_params=_params(("parallel",)),
        name="proj_ffn",
    )(*mix_ins, wout, x2, nffn, wg, wu, wd)


def _odd_in_kernel(x_ref, nmix_ref, win_ref, cos_ref, sin_ref, q_ref, k_ref, v_ref, g_ref):
    x = x_ref[...]
    h = (x * _rms_scale(x) * nmix_ref[...]).astype(BF16)
    cos, sin = cos_ref[...], sin_ref[...]
    nq = RET_HEADS * RET_QK
    nv = RET_HEADS * RET_V
    half = RET_QK // 2

    def rope_store(dst, col0, scale):
        for hd in range(RET_HEADS):
            c = col0 + hd * RET_QK
            t = jnp.dot(h, win_ref[:, c:c + RET_QK], preferred_element_type=F32)
            t1, t2 = t[:, :half], t[:, half:]
            dst[:, hd * RET_QK:hd * RET_QK + half] = ((t1 * cos - t2 * sin) * scale).astype(dst.dtype)
            dst[:, hd * RET_QK + half:(hd + 1) * RET_QK] = ((t2 * cos + t1 * sin) * scale).astype(dst.dtype)

    rope_store(q_ref, 0, 1.0)
    rope_store(k_ref, nq, RET_QK ** -0.5)
    for hd in range(RET_HEADS):
        cs = slice(hd * RET_V, (hd + 1) * RET_V)
        v_ref[:, cs] = jnp.dot(h, win_ref[:, 2 * nq + hd * RET_V:2 * nq + (hd + 1) * RET_V],
                               preferred_element_type=F32).astype(v_ref.dtype)
        gate = jnp.dot(h, win_ref[:, 2 * nq + nv + hd * RET_V:2 * nq + nv + (hd + 1) * RET_V],
                       preferred_element_type=F32)
        g_ref[:, cs] = _silu(gate).astype(g_ref.dtype)


def _odd_in(x2, nmix, win, cos, sin, seq):
    tokens, d = x2.shape
    rows = ODD_IN_ROWS
    per_seq = seq // rows
    nq = RET_HEADS * RET_QK
    nv = RET_HEADS * RET_V
    row_spec = lambda w: pl.BlockSpec((rows, w), lambda i: (i, 0))
    tab_spec = pl.BlockSpec((rows, RET_QK // 2), lambda i: (i % per_seq, 0))
    return pl.pallas_call(
        _odd_in_kernel,
        grid=(tokens // rows,),
        in_specs=[row_spec(d), _const_spec(nmix.shape), _const_spec(win.shape), tab_spec, tab_spec],
        out_specs=[row_spec(nq), row_spec(nq), row_spec(nv), row_spec(nv)],
        out_shape=[jax.ShapeDtypeStruct((tokens, nq), BF16), jax.ShapeDtypeStruct((tokens, nq), BF16),
                   jax.ShapeDtypeStruct((tokens, nv), BF16), jax.ShapeDtypeStruct((tokens, nv), BF16)],
        compiler_params=_params(("parallel",)),
        name="odd_in",
    )(x2, nmix, win, cos, sin)


def _retention_kernel(q_ref, k_ref, v_ref, g_ref, ron_ref, o_ref, state_ref, decay_ref):
    tc = q_ref.shape[0]
    j = pl.program_id(1)

    def log_gamma(hd):
        return jnp.log(jnp.full((1, 1), 1.0 - 2.0 ** (-5.0 - hd), F32))

    @pl.when(j == 0)
    def _():
        state_ref[...] = jnp.zeros_like(state_ref)
        diff = (lax.broadcasted_iota(jnp.int32, (tc, tc), 0)
                - lax.broadcasted_iota(jnp.int32, (tc, tc), 1))
        dpos = jnp.maximum(diff, 0).astype(F32)
        for hd in range(RET_HEADS):
            decay_ref[hd] = jnp.where(diff >= 0, jnp.exp(log_gamma(hd) * dpos), 0.0)

    pos = lax.broadcasted_iota(jnp.int32, (tc, 1), 0).astype(F32)
    for hd in range(RET_HEADS):
        lg = log_gamma(hd)
        xi = jnp.exp(lg * (pos + 1.0))
        zeta = jnp.exp(lg * (tc - 1.0 - pos))
        g_chunk = jnp.exp(lg * float(tc))
        qs = slice(hd * RET_QK, (hd + 1) * RET_QK)
        vs = slice(hd * RET_V, (hd + 1) * RET_V)
        qh, kh, vh = q_ref[:, qs], k_ref[:, qs], v_ref[:, vs]
        state = state_ref[hd]
        a = lax.dot_general(qh, kh, NT_DIMS, preferred_element_type=F32) * decay_ref[hd]
        o = (jnp.dot(a.astype(BF16), vh, preferred_element_type=F32)
             + jnp.dot(qh, state.astype(BF16), preferred_element_type=F32) * xi)
        kz = (kh.astype(F32) * zeta).astype(BF16)
        state_ref[hd] = state * g_chunk + lax.dot_general(
            kz, vh, (((0,), (0,)), ((), ())), preferred_element_type=F32)
        on = o * _rms_scale(o) * ron_ref[:, vs]
        o_ref[:, vs] = (on * g_ref[:, vs].astype(F32)).astype(o_ref.dtype)


def _retention(q3, k3, v3, g3, ron):
    b, s, nq = q3.shape
    nv = v3.shape[2]
    tc = RET_ROWS
    blk = lambda w: pl.BlockSpec((None, tc, w), lambda bi, j: (bi, j, 0))
    return pl.pallas_call(
        _retention_kernel,
        grid=(b, s // tc),
        in_specs=[blk(nq), blk(nq), blk(nv), blk(nv), _const_spec(ron.shape)],
        out_specs=blk(nv),
        out_shape=jax.ShapeDtypeStruct((b, s, nv), BF16),
        scratch_shapes=[pltpu.VMEM((RET_HEADS, RET_QK, RET_V), F32),
                        pltpu.VMEM((RET_HEADS, tc, tc), F32)],
        compiler_params=_params(("arbitrary", "arbitrary")),
        name="retention",
    )(q3, k3, v3, g3, ron)


def _rope_angles(seq, half):
    inv = ROPE_THETA ** (-jnp.arange(half, dtype=F32) / half)
    return jnp.arange(seq, dtype=F32)[:, None] * inv[None, :]


def _lane_bcast(g, width):
    return jnp.broadcast_to(g[:, None], (g.shape[0], width))


def kernel(x, norm_mix, norm_ffn, even_w_in, gm_v_norm, gm_w_s, gm_b_s, mla_q_a_norm, mla_w_uq,
           mla_kv_a_norm, mla_w_ukv, mla_q_norm, mla_k_norm, even_w_out, odd_w_in, ret_out_norm,
           odd_w_out, ffn_w_gate, ffn_w_up, ffn_w_down):
    b, s, d = x.shape
    tokens = b * s
    x2 = x.reshape(tokens, d)
    row = lambda t: t.reshape(1, -1)

    w_in = even_w_in[0]
    wuv = w_in[:, :2 * GM_WIDTH].astype(BF16)
    wct = w_in[:, 2 * GM_WIDTH:].T.astype(BF16)
    wuq3 = mla_w_uq[0].reshape(MLA_Q_RANK, MLA_HEADS, MLA_QK)
    wuqt = jnp.pad(wuq3, ((0, 0), (0, 0), (0, LANES - MLA_QK))).reshape(MLA_Q_RANK, -1).T.astype(BF16)
    wukv3 = mla_w_ukv[0].reshape(MLA_KV_RANK, MLA_HEADS, MLA_NOPE + MLA_V)
    wukvt = jnp.concatenate([wukv3[:, :, :MLA_NOPE].reshape(MLA_KV_RANK, -1),
                             wukv3[:, :, MLA_NOPE:].reshape(MLA_KV_RANK, -1)], axis=1).T.astype(BF16)
    er = EVEN_IN_ROWS
    pad_head = lambda g: jnp.pad(g, (0, LANES - MLA_QK))
    ang = _rope_angles(s, MLA_ROPE // 2)
    a, qt, k, vt = _even_in(
        x2, row(norm_mix[0]), wuv, wct, row(gm_v_norm[0]), gm_w_s[0], gm_b_s[0].T,
        _lane_bcast(mla_q_a_norm[0], er), wuqt, _lane_bcast(mla_kv_a_norm[0], er), wukvt,
        _lane_bcast(pad_head(mla_q_norm[0]), er), _lane_bcast(pad_head(mla_k_norm[0]), er),
        jnp.cos(ang).T, jnp.sin(ang).T, s)
    o = _attention(qt, k.reshape(b, s, MLA_HEADS * LANES), vt, b, s)
    x2 = _proj_ffn([a, o.reshape(tokens, -1)], even_w_out[0].astype(BF16), x2, row(norm_ffn[0]),
                   ffn_w_gate[0].astype(BF16), ffn_w_up[0].astype(BF16), ffn_w_down[0].astype(BF16))

    ang = _rope_angles(s, RET_QK // 2)
    rq, rk, rv, rg = _odd_in(x2, row(norm_mix[1]), odd_w_in[0].astype(BF16), jnp.cos(ang), jnp.sin(ang), s)
    nq = RET_HEADS * RET_QK
    nv = RET_HEADS * RET_V
    og = _retention(rq.reshape(b, s, nq), rk.reshape(b, s, nq), rv.reshape(b, s, nv),
                    rg.reshape(b, s, nv), row(ret_out_norm[0]))
    x2 = _proj_ffn([og.reshape(tokens, nv)], odd_w_out[0].astype(BF16), x2, row(norm_ffn[1]),
                   ffn_w_gate[1].astype(BF16), ffn_w_up[1].astype(BF16), ffn_w_down[1].astype(BF16))
    return x2.reshape(b, s, d)
```

```python
import functools
import math

import jax
import jax.numpy as jnp
from jax import lax
from jax.experimental import pallas as pl
from jax.experimental.pallas import tpu as pltpu

F32 = jnp.float32
BF16 = jnp.bfloat16

EPS = 1e-6
ROPE_THETA = 10000.0
CHUNK = 64

GM_GROUPS = 4
GM_GROUP_DIM = 128
GM_WIDTH = GM_GROUPS * GM_GROUP_DIM
GM_BLOCK = 128
MLA_HEADS = 8
MLA_Q_RANK = 384
MLA_KV_RANK = 256
MLA_NOPE = 64
MLA_ROPE = 32
MLA_V = 64
MLA_QK = MLA_NOPE + MLA_ROPE
RET_HEADS = 4
RET_QK = 256
RET_V = 512

LANES = 128
ONES_ROWS = 16
V7X_VMEM_LIMIT = 56 * 1024 * 1024

EVEN_IN_ROWS = 512
EVEN_IN_SUB = 256
ATTN_Q_ROWS = 512
ATTN_KV_ROWS = 512
FFN_ROWS = 512
ODD_IN_ROWS = 512
RET_ROWS = 512
RET_CHUNK = 256
FF_CHUNKS = (768, 768, 768, 512)

NT_DIMS = (((1,), (1,)), ((), ()))


def _rms_scale(t):
    return lax.rsqrt(jnp.mean(t * t, axis=-1, keepdims=True) + EPS)


def _gelu_tanh(t):
    return 0.5 * t * (1.0 + jnp.tanh(math.sqrt(2.0 / math.pi) * (t + 0.044715 * (t * t * t))))


def _silu(t):
    return t * (1.0 / (1.0 + jnp.exp(-t)))


def _const_spec(shape):
    nd = len(shape)
    return pl.BlockSpec(shape, lambda *_: (0,) * nd, pipeline_mode=pl.Buffered(1))


def _params(sem):
    return pltpu.CompilerParams(dimension_semantics=sem, vmem_limit_bytes=V7X_VMEM_LIMIT)


def _even_in_kernel(x_ref, nmix_ref, wuv_ref, wct_ref, gvn_ref, ws_ref, bs_ref, gqa_ref, wuqt_ref,
                    gkva_ref, wukvt_ref, gq_ref, gk_ref, cos_ref, sin_ref,
                    a_ref, qt_ref, k_ref, vt_ref):
    for sub in range(x_ref.shape[0] // EVEN_IN_SUB):
        _even_in_subtile(sub, x_ref, nmix_ref, wuv_ref, wct_ref, gvn_ref, ws_ref, bs_ref, gqa_ref, wuqt_ref,
                         gkva_ref, wukvt_ref, gq_ref, gk_ref, cos_ref, sin_ref, a_ref, qt_ref, k_ref, vt_ref)


def _even_in_subtile(sub, x_ref, nmix_ref, wuv_ref, wct_ref, gvn_ref, ws_ref, bs_ref, gqa_ref, wuqt_ref,
                     gkva_ref, wukvt_ref, gq_ref, gk_ref, cos_ref, sin_ref, a_ref, qt_ref, k_ref, vt_ref):
    rows = EVEN_IN_SUB
    tok = slice(sub * rows, (sub + 1) * rows)
    x = x_ref[tok, :]
    h = (x * _rms_scale(x) * nmix_ref[...]).astype(BF16)

    zuv = jnp.dot(h, wuv_ref[...], preferred_element_type=F32)
    u = _gelu_tanh(zuv[:, :GM_WIDTH])
    v = _gelu_tanh(zuv[:, GM_WIDTH:])
    t_out = lax.broadcasted_iota(jnp.int32, (GM_BLOCK, GM_BLOCK), 0) // CHUNK
    t_in = lax.broadcasted_iota(jnp.int32, (GM_BLOCK, GM_BLOCK), 1) // CHUNK
    causal = t_in <= t_out
    nblk = rows // GM_BLOCK
    for g in range(GM_GROUPS):
        cs = slice(g * GM_GROUP_DIM, (g + 1) * GM_GROUP_DIM)
        vg = v[:, cs]
        vn = (vg * _rms_scale(vg) * gvn_ref[:, cs]).astype(BF16)
        wg = jnp.where(causal, ws_ref[g], 0.0).astype(BF16)
        vcat = jnp.concatenate([vn[j * GM_BLOCK:(j + 1) * GM_BLOCK] for j in range(nblk)], axis=1)
        s = jnp.dot(wg, vcat, preferred_element_type=F32) + bs_ref[:, g:g + 1]
        for j in range(nblk):
            rs = slice(j * GM_BLOCK, (j + 1) * GM_BLOCK)
            dst = slice(sub * rows + j * GM_BLOCK, sub * rows + (j + 1) * GM_BLOCK)
            a_ref[dst, cs] = (u[rs, cs] * s[:, j * GM_BLOCK:(j + 1) * GM_BLOCK]).astype(a_ref.dtype)

    zc = lax.dot_general(wct_ref[...], h, NT_DIMS, preferred_element_type=F32)
    o1 = MLA_Q_RANK
    o2 = o1 + MLA_KV_RANK
    cq, ckv, kpe = zc[:o1], zc[o1:o2], zc[o2:]

    def col_rms(t, n):
        return lax.rsqrt(jnp.sum(t * t, axis=0, keepdims=True) * (1.0 / n) + EPS)

    cqn = (cq * col_rms(cq, MLA_Q_RANK) * gqa_ref[...]).astype(BF16)
    ckvn = (ckv * col_rms(ckv, MLA_KV_RANK) * gkva_ref[...]).astype(BF16)
    qt = jnp.dot(wuqt_ref[...], cqn, preferred_element_type=F32)
    kvt = jnp.dot(wukvt_ref[...], ckvn, preferred_element_type=F32)

    cos, sin = cos_ref[:, tok], sin_ref[:, tok]
    half = MLA_ROPE // 2
    n1, n2 = MLA_NOPE, MLA_NOPE + half

    def rope(x1, x2):
        return x1 * cos - x2 * sin, x2 * cos + x1 * sin

    gq = gq_ref[...] * (MLA_QK ** -0.5 * math.log2(math.e))
    for hd in range(MLA_HEADS):
        t = qt[hd * LANES:(hd + 1) * LANES]
        r = lax.rsqrt(jnp.sum(t * t, axis=0, keepdims=True) * (1.0 / MLA_QK) + EPS)
        tn = t * gq
        r1, r2 = rope(tn[n1:n2], tn[n2:MLA_QK])
        base = hd * LANES
        qt_ref[base:base + n1, tok] = (tn[:n1] * r).astype(qt_ref.dtype)
        qt_ref[base + n1:base + n2, tok] = (r1 * r).astype(qt_ref.dtype)
        qt_ref[base + n2:base + MLA_QK, tok] = (r2 * r).astype(qt_ref.dtype)
        qt_ref[base + MLA_QK:base + LANES, tok] = jnp.zeros((LANES - MLA_QK, rows), qt_ref.dtype)

    gk = gk_ref[...]
    kpe_ss = jnp.sum(kpe * kpe, axis=0, keepdims=True)
    kr1, kr2 = rope(kpe[:half] * gk[n1:n2], kpe[half:] * gk[n2:MLA_QK])
    pad = jnp.zeros((LANES - MLA_QK, rows), F32)
    for hd in range(MLA_HEADS):
        t = kvt[hd * MLA_NOPE:(hd + 1) * MLA_NOPE]
        r = lax.rsqrt((jnp.sum(t * t, axis=0, keepdims=True) + kpe_ss) * (1.0 / MLA_QK) + EPS)
        kt = jnp.concatenate([t * gk[:n1] * r, kr1 * r, kr2 * r, pad], axis=0)
        k_ref[tok, hd * LANES:(hd + 1) * LANES] = kt.T.astype(k_ref.dtype)
    vt_ref[sub] = kvt[MLA_HEADS * MLA_NOPE:].astype(vt_ref.dtype)


def _even_in(x2, nmix, wuv, wct, gvn, ws, bs_t, gqa, wuqt, gkva, wukvt, gq, gk, cos_t, sin_t, seq):
    tokens, d = x2.shape
    rows = EVEN_IN_ROWS
    per_seq = seq // rows
    row_spec = lambda w: pl.BlockSpec((rows, w), lambda i: (i, 0))
    tab_spec = pl.BlockSpec((MLA_ROPE // 2, rows), lambda i: (0, i % per_seq))
    hw = MLA_HEADS * LANES
    vw = MLA_HEADS * MLA_V
    consts = [nmix, wuv, wct, gvn, ws, bs_t, gqa, wuqt, gkva, wukvt, gq, gk]
    return pl.pallas_call(
        _even_in_kernel,
        grid=(tokens // rows,),
        in_specs=[row_spec(d)] + [_const_spec(c.shape) for c in consts] + [tab_spec, tab_spec],
        out_specs=[row_spec(GM_WIDTH),
                   pl.BlockSpec((hw, rows), lambda i: (0, i)),
                   row_spec(hw),
                   pl.BlockSpec((rows // EVEN_IN_SUB, vw, EVEN_IN_SUB), lambda i: (i, 0, 0))],
        out_shape=[jax.ShapeDtypeStruct((tokens, GM_WIDTH), BF16),
                   jax.ShapeDtypeStruct((hw, tokens), BF16),
                   jax.ShapeDtypeStruct((tokens, hw), BF16),
                   jax.ShapeDtypeStruct((tokens // EVEN_IN_SUB, vw, EVEN_IN_SUB), BF16)],
        compiler_params=_params(("parallel",)),
        name="even_in",
    )(x2, *consts, cos_t, sin_t)


def _attn_kernel(qt_ref, k_ref, vt_ref, o_ref, sa_ref, sb_ref, ma_ref, mb_ref, m_ref, acc_ref):
    tq = qt_ref.shape[1]
    tk = ATTN_KV_ROWS
    vrows = vt_ref.shape[2]
    i = pl.program_id(2)

    m_ref[...] = jnp.full(m_ref.shape, -jnp.inf, F32)
    acc_ref[...] = jnp.zeros(acc_ref.shape, F32)
    ones = jnp.ones((ONES_ROWS, vrows), BF16)

    def scores(j, dst, bmax, masked):
        start = pl.multiple_of(j * tk, tk)
        for hh in range(2):
            s = jnp.dot(k_ref[pl.ds(start, tk), hh * LANES:(hh + 1) * LANES],
                        qt_ref[hh * LANES:(hh + 1) * LANES, :], preferred_element_type=F32)
            if masked:
                ck = lax.broadcasted_iota(jnp.int32, (tk, tq), 0) // CHUNK
                cq = lax.broadcasted_iota(jnp.int32, (tk, tq), 1) // CHUNK
                s = jnp.where(ck <= cq, s, -jnp.inf)
            dst[hh] = s
            bmax[hh] = jnp.max(s, axis=0, keepdims=True)

    def update(j, src, bmax):
        for hh in range(2):
            m = m_ref[hh]
            m_new = jnp.maximum(m, bmax[hh])
            alpha = jnp.exp2(m - m_new)
            p = jnp.exp2(src[hh] - m_new).astype(BF16)
            m_ref[hh] = m_new
            pv = None
            for c in range(tk // vrows):
                vt = vt_ref[j * (tk // vrows) + c, hh * MLA_V:(hh + 1) * MLA_V, :]
                part = jnp.dot(jnp.concatenate([vt, ones], axis=0), p[c * vrows:(c + 1) * vrows],
                               preferred_element_type=F32)
                pv = part if pv is None else pv + part
            acc_ref[hh] = alpha * acc_ref[hh] + pv

    @pl.when(i > 0)
    def _():
        scores(0, sa_ref, ma_ref, False)

    def two_blocks(jj, carry):
        j0 = 2 * jj
        scores(j0 + 1, sb_ref, mb_ref, False)
        update(j0, sa_ref, ma_ref)
        scores(j0 + 2, sa_ref, ma_ref, False)
        update(j0 + 1, sb_ref, mb_ref)
        return carry

    lax.fori_loop(0, (i - 1) // 2, two_blocks, 0)

    @pl.when(i == 0)
    def _():
        scores(0, sa_ref, ma_ref, True)
        update(0, sa_ref, ma_ref)

    @pl.when(i % 2 == 1)
    def _():
        scores(i, sb_ref, mb_ref, True)
        update(i - 1, sa_ref, ma_ref)
        update(i, sb_ref, mb_ref)

    @pl.when(jnp.logical_and(i % 2 == 0, i > 0))
    def _():
        scores(i - 1, sb_ref, mb_ref, False)
        update(i - 2, sa_ref, ma_ref)
        scores(i, sa_ref, ma_ref, True)
        update(i - 1, sb_ref, mb_ref)
        update(i, sa_ref, ma_ref)

    ot = jnp.concatenate([acc_ref[hh, :MLA_V] * (1.0 / acc_ref[hh, MLA_V:MLA_V + 1]) for hh in range(2)],
                         axis=0)
    o_ref[...] = ot.T.astype(o_ref.dtype)


def _attention(qt, k3, vt, b, s):
    tq = ATTN_Q_ROWS
    assert tq == ATTN_KV_ROWS
    pairs = MLA_HEADS // 2
    nq = s // tq
    vchunks, _, vrows = vt.shape
    return pl.pallas_call(
        _attn_kernel,
        grid=(b, pairs, nq),
        in_specs=[pl.BlockSpec((2 * LANES, tq), lambda bi, p, i: (p, bi * nq + i)),
                  pl.BlockSpec((None, s, 2 * LANES), lambda bi, p, i: (bi, 0, p)),
                  pl.BlockSpec((vchunks // b, 2 * MLA_V, vrows), lambda bi, p, i: (bi, p, 0))],
        out_specs=pl.BlockSpec((None, tq, LANES), lambda bi, p, i: (bi, i, p)),
        out_shape=jax.ShapeDtypeStruct((b, s, MLA_HEADS * MLA_V), BF16),
        scratch_shapes=[pltpu.VMEM((2, ATTN_KV_ROWS, tq), F32),
                        pltpu.VMEM((2, ATTN_KV_ROWS, tq), F32),
                        pltpu.VMEM((2, 1, tq), F32),
                        pltpu.VMEM((2, 1, tq), F32),
                        pltpu.VMEM((2, 1, tq), F32),
                        pltpu.VMEM((2, MLA_V + ONES_ROWS, tq), F32)],
        compiler_params=_params(("parallel", "parallel", "parallel")),
        name="attention",
    )(qt, k3, vt)


def _proj_ffn_kernel(*refs, n_in):
    in_refs = refs[:n_in]
    wout_ref, x_ref, nffn_ref, wg_ref, wu_ref, wd_ref, out_ref = refs[n_in:]
    mix = None
    off = 0
    for r in in_refs:
        kdim = r.shape[1]
        part = jnp.dot(r[...], wout_ref[off:off + kdim, :], preferred_element_type=F32)
        mix = part if mix is None else mix + part
        off += kdim
    x1 = x_ref[...] + mix
    h = (x1 * _rms_scale(x1) * nffn_ref[...]).astype(BF16)
    acc = x1
    c0 = 0
    for c in FF_CHUNKS:
        g = jnp.dot(h, wg_ref[:, c0:c0 + c], preferred_element_type=F32)
        u = jnp.dot(h, wu_ref[:, c0:c0 + c], preferred_element_type=F32)
        act = (_silu(g) * u).astype(BF16)
        acc = acc + jnp.dot(act, wd_ref[c0:c0 + c, :], preferred_element_type=F32)
        c0 += c
    out_ref[...] = acc


def _proj_ffn(mix_ins, wout, x2, nffn, wg, wu, wd, layer):
    tokens, d = x2.shape
    rows = FFN_ROWS
    assert sum(FF_CHUNKS) == wg.shape[2]
    row_spec = lambda w: pl.BlockSpec((rows, w), lambda i: (i, 0))
    layer_spec = lambda w: pl.BlockSpec((None,) + w.shape[1:], lambda i: (layer, 0, 0),
                                        pipeline_mode=pl.Buffered(1))
    return pl.pallas_call(
        functools.partial(_proj_ffn_kernel, n_in=len(mix_ins)),
        grid=(tokens // rows,),
        in_specs=[row_spec(m.shape[1]) for m in mix_ins]
        + [_const_spec(wout.shape), row_spec(d), _const_spec(nffn.shape),
           layer_spec(wg), layer_spec(wu), layer_spec(wd)],
        out_specs=row_spec(d),
        out_shape=jax.ShapeDtypeStruct((tokens, d), F32),
        compiler_params=_params(("parallel",)),
        name="proj_ffn",
    )(*mix_ins, wout, x2, nffn, wg, wu, wd)


def _odd_in_kernel(x_ref, nmix_ref, win_ref, cos_ref, sin_ref, q_ref, k_ref, v_ref, g_ref):
    x = x_ref[...]
    h = (x * _rms_scale(x) * nmix_ref[...]).astype(BF16)
    cos, sin = cos_ref[...], sin_ref[...]
    nq = RET_HEADS * RET_QK
    nv = RET_HEADS * RET_V
    half = RET_QK // 2

    def rope_store(dst, col0, scale):
        for hd in range(RET_HEADS):
            c = col0 + hd * RET_QK
            t = jnp.dot(h, win_ref[:, c:c + RET_QK], preferred_element_type=F32)
            t1, t2 = t[:, :half], t[:, half:]
            dst[:, hd * RET_QK:hd * RET_QK + half] = ((t1 * cos - t2 * sin) * scale).astype(dst.dtype)
            dst[:, hd * RET_QK + half:(hd + 1) * RET_QK] = ((t2 * cos + t1 * sin) * scale).astype(dst.dtype)

    rope_store(q_ref, 0, 1.0)
    rope_store(k_ref, nq, RET_QK ** -0.5)
    for hd in range(RET_HEADS):
        cs = slice(hd * RET_V, (hd + 1) * RET_V)
        v_ref[:, cs] = jnp.dot(h, win_ref[:, 2 * nq + hd * RET_V:2 * nq + (hd + 1) * RET_V],
                               preferred_element_type=F32).astype(v_ref.dtype)
        gate = jnp.dot(h, win_ref[:, 2 * nq + nv + hd * RET_V:2 * nq + nv + (hd + 1) * RET_V],
                       preferred_element_type=F32)
        g_ref[:, cs] = _silu(gate).astype(g_ref.dtype)


def _odd_in(x2, nmix, win, cos, sin, seq):
    tokens, d = x2.shape
    rows = ODD_IN_ROWS
    per_seq = seq // rows
    nq = RET_HEADS * RET_QK
    nv = RET_HEADS * RET_V
    row_spec = lambda w: pl.BlockSpec((rows, w), lambda i: (i, 0))
    tab_spec = pl.BlockSpec((rows, RET_QK // 2), lambda i: (i % per_seq, 0))
    return pl.pallas_call(
        _odd_in_kernel,
        grid=(tokens // rows,),
        in_specs=[row_spec(d), _const_spec(nmix.shape), _const_spec(win.shape), tab_spec, tab_spec],
        out_specs=[row_spec(nq), row_spec(nq), row_spec(nv), row_spec(nv)],
        out_shape=[jax.ShapeDtypeStruct((tokens, nq), BF16), jax.ShapeDtypeStruct((tokens, nq), BF16),
                   jax.ShapeDtypeStruct((tokens, nv), BF16), jax.ShapeDtypeStruct((tokens, nv), BF16)],
        compiler_params=_params(("parallel",)),
        name="odd_in",
    )(x2, nmix, win, cos, sin)


def _retention_kernel(q_ref, k_ref, v_ref, g_ref, ron_ref, o_ref, state_ref, decay_ref):
    tc = RET_CHUNK
    j = pl.program_id(1)

    def log_gamma(hd):
        return jnp.log(jnp.full((1, 1), 1.0 - 2.0 ** (-5.0 - hd), F32))

    @pl.when(j == 0)
    def _():
        state_ref[...] = jnp.zeros_like(state_ref)
        diff = (lax.broadcasted_iota(jnp.int32, (tc, tc), 0)
                - lax.broadcasted_iota(jnp.int32, (tc, tc), 1))
        dpos = jnp.maximum(diff, 0).astype(F32)
        for hd in range(RET_HEADS):
            decay_ref[hd] = jnp.where(diff >= 0, jnp.exp(log_gamma(hd) * dpos), 0.0)

    pos = lax.broadcasted_iota(jnp.int32, (tc, 1), 0).astype(F32)
    scales = []
    for hd in range(RET_HEADS):
        lg = log_gamma(hd)
        scales.append((jnp.exp(lg * (pos + 1.0)), jnp.exp(lg * (tc - 1.0 - pos)), jnp.exp(lg * float(tc))))
    for c in range(q_ref.shape[0] // tc):
        rs = slice(c * tc, (c + 1) * tc)
        for hd in range(RET_HEADS):
            xi, zeta, g_chunk = scales[hd]
            qs = slice(hd * RET_QK, (hd + 1) * RET_QK)
            vs = slice(hd * RET_V, (hd + 1) * RET_V)
            qh, kh, vh = q_ref[rs, qs], k_ref[rs, qs], v_ref[rs, vs]
            state = state_ref[hd]
            a = lax.dot_general(qh, kh, NT_DIMS, preferred_element_type=F32) * decay_ref[hd]
            o = (jnp.dot(a.astype(BF16), vh, preferred_element_type=F32)
                 + jnp.dot(qh, state.astype(BF16), preferred_element_type=F32) * xi)
            kz = (kh.astype(F32) * zeta).astype(BF16)
            state_ref[hd] = state * g_chunk + lax.dot_general(
                kz, vh, (((0,), (0,)), ((), ())), preferred_element_type=F32)
            on = o * _rms_scale(o) * ron_ref[:, vs]
            o_ref[rs, vs] = (on * g_ref[rs, vs].astype(F32)).astype(o_ref.dtype)


def _retention(q3, k3, v3, g3, ron):
    b, s, nq = q3.shape
    nv = v3.shape[2]
    tc = RET_ROWS
    blk = lambda w: pl.BlockSpec((None, tc, w), lambda bi, j: (bi, j, 0))
    return pl.pallas_call(
        _retention_kernel,
        grid=(b, s // tc),
        in_specs=[blk(nq), blk(nq), blk(nv), blk(nv), _const_spec(ron.shape)],
        out_specs=blk(nv),
        out_shape=jax.ShapeDtypeStruct((b, s, nv), BF16),
        scratch_shapes=[pltpu.VMEM((RET_HEADS, RET_QK, RET_V), F32),
                        pltpu.VMEM((RET_HEADS, RET_CHUNK, RET_CHUNK), F32)],
        compiler_params=_params(("arbitrary", "arbitrary")),
        name="retention",
    )(q3, k3, v3, g3, ron)


def _rope_angles(seq, half):
    inv = ROPE_THETA ** (-jnp.arange(half, dtype=F32) / half)
    return jnp.arange(seq, dtype=F32)[:, None] * inv[None, :]


def _lane_bcast(g, width):
    return jnp.broadcast_to(g[:, None], (g.shape[0], width))


def kernel(x, norm_mix, norm_ffn, even_w_in, gm_v_norm, gm_w_s, gm_b_s, mla_q_a_norm, mla_w_uq,
           mla_kv_a_norm, mla_w_ukv, mla_q_norm, mla_k_norm, even_w_out, odd_w_in, ret_out_norm,
           odd_w_out, ffn_w_gate, ffn_w_up, ffn_w_down):
    b, s, d = x.shape
    tokens = b * s
    x2 = x.reshape(tokens, d)
    row = lambda t: t.reshape(1, -1)

    w_in = even_w_in[0]
    wuv = w_in[:, :2 * GM_WIDTH].astype(BF16)
    wct = w_in[:, 2 * GM_WIDTH:].T.astype(BF16)
    wuq3 = mla_w_uq[0].reshape(MLA_Q_RANK, MLA_HEADS, MLA_QK)
    wuqt = jnp.pad(wuq3, ((0, 0), (0, 0), (0, LANES - MLA_QK))).reshape(MLA_Q_RANK, -1).T.astype(BF16)
    wukv3 = mla_w_ukv[0].reshape(MLA_KV_RANK, MLA_HEADS, MLA_NOPE + MLA_V)
    wukvt = jnp.concatenate([wukv3[:, :, :MLA_NOPE].reshape(MLA_KV_RANK, -1),
                             wukv3[:, :, MLA_NOPE:].reshape(MLA_KV_RANK, -1)], axis=1).T.astype(BF16)
    er = EVEN_IN_SUB
    pad_head = lambda g: jnp.pad(g, (0, LANES - MLA_QK))
    ang = _rope_angles(s, MLA_ROPE // 2)
    a, qt, k, vt = _even_in(
        x2, row(norm_mix[0]), wuv, wct, row(gm_v_norm[0]), gm_w_s[0], gm_b_s[0].T,
        _lane_bcast(mla_q_a_norm[0], er), wuqt, _lane_bcast(mla_kv_a_norm[0], er), wukvt,
        _lane_bcast(pad_head(mla_q_norm[0]), er), _lane_bcast(pad_head(mla_k_norm[0]), er),
        jnp.cos(ang).T, jnp.sin(ang).T, s)
    o = _attention(qt, k.reshape(b, s, MLA_HEADS * LANES), vt, b, s)
    wg, wu, wd = ffn_w_gate.astype(BF16), ffn_w_up.astype(BF16), ffn_w_down.astype(BF16)
    x2 = _proj_ffn([a, o.reshape(tokens, -1)], even_w_out[0].astype(BF16), x2, row(norm_ffn[0]), wg, wu, wd, 0)

    ang = _rope_angles(s, RET_QK // 2)
    rq, rk, rv, rg = _odd_in(x2, row(norm_mix[1]), odd_w_in[0].astype(BF16), jnp.cos(ang), jnp.sin(ang), s)
    nq = RET_HEADS * RET_QK
    nv = RET_HEADS * RET_V
    og = _retention(rq.reshape(b, s, nq), rk.reshape(b, s, nq), rv.reshape(b, s, nv),
                    rg.reshape(b, s, nv), row(ret_out_norm[0]))
    x2 = _proj_ffn([og.reshape(tokens, nv)], odd_w_out[0].astype(BF16), x2, row(norm_ffn[1]), wg, wu, wd, 1)
    return x2.reshape(b, s, d)
```

```python
import functools
import math

import jax
import jax.numpy as jnp
from jax import lax
from jax.experimental import pallas as pl
from jax.experimental.pallas import tpu as pltpu

F32 = jnp.float32
BF16 = jnp.bfloat16

EPS = 1e-6
ROPE_THETA = 10000.0
CHUNK = 64

GM_GROUPS = 4
GM_GROUP_DIM = 128
GM_WIDTH = GM_GROUPS * GM_GROUP_DIM
GM_BLOCK = 128
MLA_HEADS = 8
MLA_Q_RANK = 384
MLA_KV_RANK = 256
MLA_NOPE = 64
MLA_ROPE = 32
MLA_V = 64
MLA_QK = MLA_NOPE + MLA_ROPE
RET_HEADS = 4
RET_QK = 256
RET_V = 512

LANES = 128
ONES_ROWS = 16
V7X_VMEM_LIMIT = 56 * 1024 * 1024

EVEN_IN_ROWS = 512
EVEN_IN_SUB = 256
ATTN_Q_ROWS = 512
ATTN_KV_ROWS = 512
ATTN_UNIT = 256
FFN_ROWS = 512
ODD_IN_ROWS = 512
RET_ROWS = 512
RET_CHUNK = 256
FF_CHUNKS = (768, 768, 768, 512)

NT_DIMS = (((1,), (1,)), ((), ()))


def _rms_scale(t):
    return lax.rsqrt(jnp.mean(t * t, axis=-1, keepdims=True) + EPS)


def _gelu_tanh(t):
    return 0.5 * t * (1.0 + jnp.tanh(math.sqrt(2.0 / math.pi) * (t + 0.044715 * (t * t * t))))


def _silu(t):
    return t * (1.0 / (1.0 + jnp.exp(-t)))


def _const_spec(shape):
    nd = len(shape)
    return pl.BlockSpec(shape, lambda *_: (0,) * nd, pipeline_mode=pl.Buffered(1))


def _params(sem):
    return pltpu.CompilerParams(dimension_semantics=sem, vmem_limit_bytes=V7X_VMEM_LIMIT)


def _even_in_kernel(x_ref, nmix_ref, wuv_ref, wct_ref, gvn_ref, ws_ref, bs_ref, gqa_ref, wuqt_ref,
                    gkva_ref, wukvt_ref, gq_ref, gk_ref, cos_ref, sin_ref,
                    a_ref, qt_ref, k_ref, vt_ref):
    subs = range(x_ref.shape[0] // EVEN_IN_SUB)
    proj = [_even_in_project(sub, x_ref, nmix_ref, wuv_ref, wct_ref) for sub in subs]
    heads = [_even_in_gate_and_expand(sub, proj[sub], gvn_ref, ws_ref, bs_ref, gqa_ref, wuqt_ref, gkva_ref,
                                      wukvt_ref, a_ref) for sub in subs]
    for sub in subs:
        _even_in_heads(sub, heads[sub], gq_ref, gk_ref, cos_ref, sin_ref, qt_ref, k_ref, vt_ref)


def _even_in_project(sub, x_ref, nmix_ref, wuv_ref, wct_ref):
    rows = EVEN_IN_SUB
    x = x_ref[sub * rows:(sub + 1) * rows, :]
    h = (x * _rms_scale(x) * nmix_ref[...]).astype(BF16)
    zuv = jnp.dot(h, wuv_ref[...], preferred_element_type=F32)
    zc = lax.dot_general(wct_ref[...], h, NT_DIMS, preferred_element_type=F32)
    return zuv, zc


def _even_in_gate_and_expand(sub, proj, gvn_ref, ws_ref, bs_ref, gqa_ref, wuqt_ref, gkva_ref, wukvt_ref, a_ref):
    rows = EVEN_IN_SUB
    zuv, zc = proj
    u = _gelu_tanh(zuv[:, :GM_WIDTH])
    v = _gelu_tanh(zuv[:, GM_WIDTH:])
    t_out = lax.broadcasted_iota(jnp.int32, (GM_BLOCK, GM_BLOCK), 0) // CHUNK
    t_in = lax.broadcasted_iota(jnp.int32, (GM_BLOCK, GM_BLOCK), 1) // CHUNK
    causal = t_in <= t_out
    nblk = rows // GM_BLOCK
    for g in range(GM_GROUPS):
        cs = slice(g * GM_GROUP_DIM, (g + 1) * GM_GROUP_DIM)
        vg = v[:, cs]
        vn = (vg * _rms_scale(vg) * gvn_ref[:, cs]).astype(BF16)
        wg = jnp.where(causal, ws_ref[g], 0.0).astype(BF16)
        vcat = jnp.concatenate([vn[j * GM_BLOCK:(j + 1) * GM_BLOCK] for j in range(nblk)], axis=1)
        s = jnp.dot(wg, vcat, preferred_element_type=F32) + bs_ref[:, g:g + 1]
        for j in range(nblk):
            rs = slice(j * GM_BLOCK, (j + 1) * GM_BLOCK)
            dst = slice(sub * rows + j * GM_BLOCK, sub * rows + (j + 1) * GM_BLOCK)
            a_ref[dst, cs] = (u[rs, cs] * s[:, j * GM_BLOCK:(j + 1) * GM_BLOCK]).astype(a_ref.dtype)

    o1 = MLA_Q_RANK
    o2 = o1 + MLA_KV_RANK
    cq, ckv, kpe = zc[:o1], zc[o1:o2], zc[o2:]

    def col_rms(t, n):
        return lax.rsqrt(jnp.sum(t * t, axis=0, keepdims=True) * (1.0 / n) + EPS)

    cqn = (cq * col_rms(cq, MLA_Q_RANK) * gqa_ref[...]).astype(BF16)
    ckvn = (ckv * col_rms(ckv, MLA_KV_RANK) * gkva_ref[...]).astype(BF16)
    qt = jnp.dot(wuqt_ref[...], cqn, preferred_element_type=F32)
    kvt = jnp.dot(wukvt_ref[...], ckvn, preferred_element_type=F32)
    return qt, kvt, kpe


def _even_in_heads(sub, expanded, gq_ref, gk_ref, cos_ref, sin_ref, qt_ref, k_ref, vt_ref):
    rows = EVEN_IN_SUB
    qt, kvt, kpe = expanded
    tok = slice(sub * rows, (sub + 1) * rows)
    cos, sin = cos_ref[:, tok], sin_ref[:, tok]
    half = MLA_ROPE // 2
    n1, n2 = MLA_NOPE, MLA_NOPE + half

    def rope(x1, x2):
        return x1 * cos - x2 * sin, x2 * cos + x1 * sin

    gq = gq_ref[...] * (MLA_QK ** -0.5 * math.log2(math.e))
    for hd in range(MLA_HEADS):
        t = qt[hd * LANES:(hd + 1) * LANES]
        r = lax.rsqrt(jnp.sum(t * t, axis=0, keepdims=True) * (1.0 / MLA_QK) + EPS)
        tn = t * gq
        r1, r2 = rope(tn[n1:n2], tn[n2:MLA_QK])
        base = hd * LANES
        qt_ref[base:base + n1, tok] = (tn[:n1] * r).astype(qt_ref.dtype)
        qt_ref[base + n1:base + n2, tok] = (r1 * r).astype(qt_ref.dtype)
        qt_ref[base + n2:base + MLA_QK, tok] = (r2 * r).astype(qt_ref.dtype)
        qt_ref[base + MLA_QK:base + LANES, tok] = jnp.zeros((LANES - MLA_QK, rows), qt_ref.dtype)

    gk = gk_ref[...]
    kpe_ss = jnp.sum(kpe * kpe, axis=0, keepdims=True)
    kr1, kr2 = rope(kpe[:half] * gk[n1:n2], kpe[half:] * gk[n2:MLA_QK])
    pad = jnp.zeros((LANES - MLA_QK, rows), F32)
    for hd in range(MLA_HEADS):
        t = kvt[hd * MLA_NOPE:(hd + 1) * MLA_NOPE]
        r = lax.rsqrt((jnp.sum(t * t, axis=0, keepdims=True) + kpe_ss) * (1.0 / MLA_QK) + EPS)
        kt = jnp.concatenate([t * gk[:n1] * r, kr1 * r, kr2 * r, pad], axis=0)
        k_ref[tok, hd * LANES:(hd + 1) * LANES] = kt.T.astype(k_ref.dtype)
    vt_ref[sub] = kvt[MLA_HEADS * MLA_NOPE:].astype(vt_ref.dtype)


def _even_in(x2, nmix, wuv, wct, gvn, ws, bs_t, gqa, wuqt, gkva, wukvt, gq, gk, cos_t, sin_t, seq):
    tokens, d = x2.shape
    rows = EVEN_IN_ROWS
    per_seq = seq // rows
    row_spec = lambda w: pl.BlockSpec((rows, w), lambda i: (i, 0))
    tab_spec = pl.BlockSpec((MLA_ROPE // 2, rows), lambda i: (0, i % per_seq))
    hw = MLA_HEADS * LANES
    vw = MLA_HEADS * MLA_V
    consts = [nmix, wuv, wct, gvn, ws, bs_t, gqa, wuqt, gkva, wukvt, gq, gk]
    return pl.pallas_call(
        _even_in_kernel,
        grid=(tokens // rows,),
        in_specs=[row_spec(d)] + [_const_spec(c.shape) for c in consts] + [tab_spec, tab_spec],
        out_specs=[row_spec(GM_WIDTH),
                   pl.BlockSpec((hw, rows), lambda i: (0, i)),
                   row_spec(hw),
                   pl.BlockSpec((rows // EVEN_IN_SUB, vw, EVEN_IN_SUB), lambda i: (i, 0, 0))],
        out_shape=[jax.ShapeDtypeStruct((tokens, GM_WIDTH), BF16),
                   jax.ShapeDtypeStruct((hw, tokens), BF16),
                   jax.ShapeDtypeStruct((tokens, hw), BF16),
                   jax.ShapeDtypeStruct((tokens // EVEN_IN_SUB, vw, EVEN_IN_SUB), BF16)],
        compiler_params=_params(("parallel",)),
        name="even_in",
    )(x2, *consts, cos_t, sin_t)


def _attn_kernel(qt_ref, k_ref, vt_ref, o_ref, sa_ref, sb_ref, ma_ref, mb_ref, m_ref, acc_ref, *, n_qblocks):
    tq = qt_ref.shape[1]
    tk = ATTN_KV_ROWS
    vrows = vt_ref.shape[2]
    i = pl.program_id(2)

    m_ref[...] = jnp.full(m_ref.shape, -jnp.inf, F32)
    acc_ref[...] = jnp.zeros(acc_ref.shape, F32)
    ones = jnp.ones((ONES_ROWS, vrows), BF16)
    buf_a = (sa_ref, ma_ref)
    buf_b = (sb_ref, mb_ref)
    units = [(hh, slice(u * ATTN_UNIT, (u + 1) * ATTN_UNIT)) for hh in range(2) for u in range(tq // ATTN_UNIT)]

    def scores(j, bufs, masked, hh, qs):
        dst, bmax = bufs
        s = jnp.dot(k_ref[j * tk:(j + 1) * tk, hh * LANES:(hh + 1) * LANES],
                    qt_ref[hh * LANES:(hh + 1) * LANES, qs], preferred_element_type=F32)
        if masked:
            ck = lax.broadcasted_iota(jnp.int32, s.shape, 0) // CHUNK
            cq = (lax.broadcasted_iota(jnp.int32, s.shape, 1) + qs.start) // CHUNK
            s = jnp.where(ck <= cq, s, -jnp.inf)
        dst[hh, :, qs] = s
        bmax[hh, :, qs] = jnp.max(s, axis=0, keepdims=True)

    def update(j, bufs, hh, qs):
        src, bmax = bufs
        m = m_ref[hh, :, qs]
        m_new = jnp.maximum(m, bmax[hh, :, qs])
        alpha = jnp.exp2(m - m_new)
        p = jnp.exp2(src[hh, :, qs] - m_new).astype(BF16)
        m_ref[hh, :, qs] = m_new
        pv = None
        for c in range(tk // vrows):
            vt = vt_ref[j * (tk // vrows) + c, hh * MLA_V:(hh + 1) * MLA_V, :]
            part = jnp.dot(jnp.concatenate([vt, ones], axis=0), p[c * vrows:(c + 1) * vrows],
                           preferred_element_type=F32)
            pv = part if pv is None else pv + part
        acc_ref[hh, :, qs] = alpha * acc_ref[hh, :, qs] + pv

    def stage(nxt, cur):
        for hh, qs in units:
            if nxt is not None:
                scores(nxt[0], nxt[1], nxt[2], hh, qs)
            if cur is not None:
                update(cur[0], cur[1], hh, qs)

    def run(n_full):
        bufs = (buf_a, buf_b)
        stage((0, buf_a, n_full == 0), None)
        for t in range(n_full + 1):
            nxt = (t + 1, bufs[(t + 1) % 2], t + 1 == n_full) if t < n_full else None
            stage(nxt, (t, bufs[t % 2]))

    for c in range(n_qblocks):
        pl.when(i == c)(functools.partial(run, c))

    ot = jnp.concatenate([acc_ref[hh, :MLA_V] * (1.0 / acc_ref[hh, MLA_V:MLA_V + 1]) for hh in range(2)],
                         axis=0)
    o_ref[...] = ot.T.astype(o_ref.dtype)


def _attention(qt, k3, vt, b, s):
    tq = ATTN_Q_ROWS
    assert tq == ATTN_KV_ROWS
    pairs = MLA_HEADS // 2
    nq = s // tq
    vchunks, _, vrows = vt.shape
    return pl.pallas_call(
        functools.partial(_attn_kernel, n_qblocks=nq),
        grid=(b, pairs, nq),
        in_specs=[pl.BlockSpec((2 * LANES, tq), lambda bi, p, i: (p, bi * nq + i)),
                  pl.BlockSpec((None, s, 2 * LANES), lambda bi, p, i: (bi, 0, p)),
                  pl.BlockSpec((vchunks // b, 2 * MLA_V, vrows), lambda bi, p, i: (bi, p, 0))],
        out_specs=pl.BlockSpec((None, tq, LANES), lambda bi, p, i: (bi, i, p)),
        out_shape=jax.ShapeDtypeStruct((b, s, MLA_HEADS * MLA_V), BF16),
        scratch_shapes=[pltpu.VMEM((2, ATTN_KV_ROWS, tq), F32),
                        pltpu.VMEM((2, ATTN_KV_ROWS, tq), F32),
                        pltpu.VMEM((2, 1, tq), F32),
                        pltpu.VMEM((2, 1, tq), F32),
                        pltpu.VMEM((2, 1, tq), F32),
                        pltpu.VMEM((2, MLA_V + ONES_ROWS, tq), F32)],
        compiler_params=_params(("parallel", "parallel", "parallel")),
        name="attention",
    )(qt, k3, vt)


def _proj_ffn_kernel(*refs, n_in):
    in_refs = refs[:n_in]
    wout_ref, x_ref, nffn_ref, wg_ref, wu_ref, wd_ref, out_ref = refs[n_in:]
    mix = None
    off = 0
    for r in in_refs:
        kdim = r.shape[1]
        part = jnp.dot(r[...], wout_ref[off:off + kdim, :], preferred_element_type=F32)
        mix = part if mix is None else mix + part
        off += kdim
    x1 = x_ref[...] + mix
    h = (x1 * _rms_scale(x1) * nffn_ref[...]).astype(BF16)
    acc = x1
    c0 = 0
    for c in FF_CHUNKS:
        g = jnp.dot(h, wg_ref[:, c0:c0 + c], preferred_element_type=F32)
        u = jnp.dot(h, wu_ref[:, c0:c0 + c], preferred_element_type=F32)
        act = (_silu(g) * u).astype(BF16)
        acc = acc + jnp.dot(act, wd_ref[c0:c0 + c, :], preferred_element_type=F32)
        c0 += c
    out_ref[...] = acc


def _proj_ffn(mix_ins, wout, x2, nffn, wg, wu, wd, layer):
    tokens, d = x2.shape
    rows = FFN_ROWS
    assert sum(FF_CHUNKS) == wg.shape[2]
    row_spec = lambda w: pl.BlockSpec((rows, w), lambda i: (i, 0))
    layer_spec = lambda w: pl.BlockSpec((None,) + w.shape[1:], lambda i: (layer, 0, 0),
                                        pipeline_mode=pl.Buffered(1))
    return pl.pallas_call(
        functools.partial(_proj_ffn_kernel, n_in=len(mix_ins)),
        grid=(tokens // rows,),
        in_specs=[row_spec(m.shape[1]) for m in mix_ins]
        + [_const_spec(wout.shape), row_spec(d), _const_spec(nffn.shape),
           layer_spec(wg), layer_spec(wu), layer_spec(wd)],
        out_specs=row_spec(d),
        out_shape=jax.ShapeDtypeStruct((tokens, d), F32),
        compiler_params=_params(("parallel",)),
        name="proj_ffn",
    )(*mix_ins, wout, x2, nffn, wg, wu, wd)


def _odd_in_kernel(x_ref, nmix_ref, win_ref, cos_ref, sin_ref, q_ref, k_ref, v_ref, g_ref):
    x = x_ref[...]
    h = (x * _rms_scale(x) * nmix_ref[...]).astype(BF16)
    cos, sin = cos_ref[...], sin_ref[...]
    nq = RET_HEADS * RET_QK
    nv = RET_HEADS * RET_V
    half = RET_QK // 2

    def rope_store(dst, col0, scale):
        for hd in range(RET_HEADS):
            c = col0 + hd * RET_QK
            t = jnp.dot(h, win_ref[:, c:c + RET_QK], preferred_element_type=F32)
            t1, t2 = t[:, :half], t[:, half:]
            dst[:, hd * RET_QK:hd * RET_QK + half] = ((t1 * cos - t2 * sin) * scale).astype(dst.dtype)
            dst[:, hd * RET_QK + half:(hd + 1) * RET_QK] = ((t2 * cos + t1 * sin) * scale).astype(dst.dtype)

    rope_store(q_ref, 0, 1.0)
    rope_store(k_ref, nq, RET_QK ** -0.5)
    for hd in range(RET_HEADS):
        cs = slice(hd * RET_V, (hd + 1) * RET_V)
        v_ref[:, cs] = jnp.dot(h, win_ref[:, 2 * nq + hd * RET_V:2 * nq + (hd + 1) * RET_V],
                               preferred_element_type=F32).astype(v_ref.dtype)
        gate = jnp.dot(h, win_ref[:, 2 * nq + nv + hd * RET_V:2 * nq + nv + (hd + 1) * RET_V],
                       preferred_element_type=F32)
        g_ref[:, cs] = _silu(gate).astype(g_ref.dtype)


def _odd_in(x2, nmix, win, cos, sin, seq):
    tokens, d = x2.shape
    rows = ODD_IN_ROWS
    per_seq = seq // rows
    nq = RET_HEADS * RET_QK
    nv = RET_HEADS * RET_V
    row_spec = lambda w: pl.BlockSpec((rows, w), lambda i: (i, 0))
    tab_spec = pl.BlockSpec((rows, RET_QK // 2), lambda i: (i % per_seq, 0))
    return pl.pallas_call(
        _odd_in_kernel,
        grid=(tokens // rows,),
        in_specs=[row_spec(d), _const_spec(nmix.shape), _const_spec(win.shape), tab_spec, tab_spec],
        out_specs=[row_spec(nq), row_spec(nq), row_spec(nv), row_spec(nv)],
        out_shape=[jax.ShapeDtypeStruct((tokens, nq), BF16), jax.ShapeDtypeStruct((tokens, nq), BF16),
                   jax.ShapeDtypeStruct((tokens, nv), BF16), jax.ShapeDtypeStruct((tokens, nv), BF16)],
        compiler_params=_params(("parallel",)),
        name="odd_in",
    )(x2, nmix, win, cos, sin)


def _retention_kernel(q_ref, k_ref, v_ref, g_ref, ron_ref, o_ref, state_ref, decay_ref):
    tc = RET_CHUNK
    j = pl.program_id(1)

    def log_gamma(hd):
        return jnp.log(jnp.full((1, 1), 1.0 - 2.0 ** (-5.0 - hd), F32))

    @pl.when(j == 0)
    def _():
        state_ref[...] = jnp.zeros_like(state_ref)
        diff = (lax.broadcasted_iota(jnp.int32, (tc, tc), 0)
                - lax.broadcasted_iota(jnp.int32, (tc, tc), 1))
        dpos = jnp.maximum(diff, 0).astype(F32)
        for hd in range(RET_HEADS):
            decay_ref[hd] = jnp.where(diff >= 0, jnp.exp(log_gamma(hd) * dpos), 0.0)

    pos = lax.broadcasted_iota(jnp.int32, (tc, 1), 0).astype(F32)
    scales = []
    for hd in range(RET_HEADS):
        lg = log_gamma(hd)
        scales.append((jnp.exp(lg * (pos + 1.0)), jnp.exp(lg * (tc - 1.0 - pos)), jnp.exp(lg * float(tc))))
    for c in range(q_ref.shape[0] // tc):
        rs = slice(c * tc, (c + 1) * tc)
        for hd in range(RET_HEADS):
            xi, zeta, g_chunk = scales[hd]
            qs = slice(hd * RET_QK, (hd + 1) * RET_QK)
            vs = slice(hd * RET_V, (hd + 1) * RET_V)
            qh, kh, vh = q_ref[rs, qs], k_ref[rs, qs], v_ref[rs, vs]
            state = state_ref[hd]
            a = lax.dot_general(qh, kh, NT_DIMS, preferred_element_type=F32) * decay_ref[hd]
            o = (jnp.dot(a.astype(BF16), vh, preferred_element_type=F32)
                 + jnp.dot(qh, state.astype(BF16), preferred_element_type=F32) * xi)
            kz = (kh.astype(F32) * zeta).astype(BF16)
            state_ref[hd] = state * g_chunk + lax.dot_general(
                kz, vh, (((0,), (0,)), ((), ())), preferred_element_type=F32)
            on = o * _rms_scale(o) * ron_ref[:, vs]
            o_ref[rs, vs] = (on * g_ref[rs, vs].astype(F32)).astype(o_ref.dtype)


def _retention(q3, k3, v3, g3, ron):
    b, s, nq = q3.shape
    nv = v3.shape[2]
    tc = RET_ROWS
    blk = lambda w: pl.BlockSpec((None, tc, w), lambda bi, j: (bi, j, 0))
    return pl.pallas_call(
        _retention_kernel,
        grid=(b, s // tc),
        in_specs=[blk(nq), blk(nq), blk(nv), blk(nv), _const_spec(ron.shape)],
        out_specs=blk(nv),
        out_shape=jax.ShapeDtypeStruct((b, s, nv), BF16),
        scratch_shapes=[pltpu.VMEM((RET_HEADS, RET_QK, RET_V), F32),
                        pltpu.VMEM((RET_HEADS, RET_CHUNK, RET_CHUNK), F32)],
        compiler_params=_params(("arbitrary", "arbitrary")),
        name="retention",
    )(q3, k3, v3, g3, ron)


def _rope_angles(seq, half):
    inv = ROPE_THETA ** (-jnp.arange(half, dtype=F32) / half)
    return jnp.arange(seq, dtype=F32)[:, None] * inv[None, :]


def _lane_bcast(g, width):
    return jnp.broadcast_to(g[:, None], (g.shape[0], width))


def kernel(x, norm_mix, norm_ffn, even_w_in, gm_v_norm, gm_w_s, gm_b_s, mla_q_a_norm, mla_w_uq,
           mla_kv_a_norm, mla_w_ukv, mla_q_norm, mla_k_norm, even_w_out, odd_w_in, ret_out_norm,
           odd_w_out, ffn_w_gate, ffn_w_up, ffn_w_down):
    b, s, d = x.shape
    tokens = b * s
    x2 = x.reshape(tokens, d)
    row = lambda t: t.reshape(1, -1)

    w_in = even_w_in[0]
    wuv = w_in[:, :2 * GM_WIDTH].astype(BF16)
    wct = w_in[:, 2 * GM_WIDTH:].T.astype(BF16)
    wuq3 = mla_w_uq[0].reshape(MLA_Q_RANK, MLA_HEADS, MLA_QK)
    wuqt = jnp.pad(wuq3, ((0, 0), (0, 0), (0, LANES - MLA_QK))).reshape(MLA_Q_RANK, -1).T.astype(BF16)
    wukv3 = mla_w_ukv[0].reshape(MLA_KV_RANK, MLA_HEADS, MLA_NOPE + MLA_V)
    wukvt = jnp.concatenate([wukv3[:, :, :MLA_NOPE].reshape(MLA_KV_RANK, -1),
                             wukv3[:, :, MLA_NOPE:].reshape(MLA_KV_RANK, -1)], axis=1).T.astype(BF16)
    er = EVEN_IN_SUB
    pad_head = lambda g: jnp.pad(g, (0, LANES - MLA_QK))
    ang = _rope_angles(s, MLA_ROPE // 2)
    a, qt, k, vt = _even_in(
        x2, row(norm_mix[0]), wuv, wct, row(gm_v_norm[0]), gm_w_s[0], gm_b_s[0].T,
        _lane_bcast(mla_q_a_norm[0], er), wuqt, _lane_bcast(mla_kv_a_norm[0], er), wukvt,
        _lane_bcast(pad_head(mla_q_norm[0]), er), _lane_bcast(pad_head(mla_k_norm[0]), er),
        jnp.cos(ang).T, jnp.sin(ang).T, s)
    o = _attention(qt, k.reshape(b, s, MLA_HEADS * LANES), vt, b, s)
    wg, wu, wd = ffn_w_gate.astype(BF16), ffn_w_up.astype(BF16), ffn_w_down.astype(BF16)
    x2 = _proj_ffn([a, o.reshape(tokens, -1)], even_w_out[0].astype(BF16), x2, row(norm_ffn[0]), wg, wu, wd, 0)

    ang = _rope_angles(s, RET_QK // 2)
    rq, rk, rv, rg = _odd_in(x2, row(norm_mix[1]), odd_w_in[0].astype(BF16), jnp.cos(ang), jnp.sin(ang), s)
    nq = RET_HEADS * RET_QK
    nv = RET_HEADS * RET_V
    og = _retention(rq.reshape(b, s, nq), rk.reshape(b, s, nq), rv.reshape(b, s, nv),
                    rg.reshape(b, s, nv), row(ret_out_norm[0]))
    x2 = _proj_ffn([og.reshape(tokens, nv)], odd_w_out[0].astype(BF16), x2, row(norm_ffn[1]), wg, wu, wd, 1)
    return x2.reshape(b, s, d)
```

```python
import functools
import math

import jax
import jax.numpy as jnp
from jax import lax
from jax.experimental import pallas as pl
from jax.experimental.pallas import tpu as pltpu

F32 = jnp.float32
BF16 = jnp.bfloat16

EPS = 1e-6
ROPE_THETA = 10000.0
CHUNK = 64

GM_GROUPS = 4
GM_GROUP_DIM = 128
GM_WIDTH = GM_GROUPS * GM_GROUP_DIM
GM_BLOCK = 128
MLA_HEADS = 8
MLA_Q_RANK = 384
MLA_KV_RANK = 256
MLA_NOPE = 64
MLA_ROPE = 32
MLA_V = 64
MLA_QK = MLA_NOPE + MLA_ROPE
RET_HEADS = 4
RET_QK = 256
RET_V = 512

LANES = 128
ONES_ROWS = 16
V7X_VMEM_LIMIT = 56 * 1024 * 1024

EVEN_IN_ROWS = 512
EVEN_IN_SUB = 256
ATTN_Q_ROWS = 512
ATTN_KV_ROWS = 512
ATTN_UNIT = 256
ATTN_GROUP = 2
FFN_ROWS = 512
ODD_IN_ROWS = 512
RET_ROWS = 512
RET_CHUNK = 256
FF_CHUNKS = (768, 768, 768, 512)

NT_DIMS = (((1,), (1,)), ((), ()))


def _rms_scale(t):
    return lax.rsqrt(jnp.mean(t * t, axis=-1, keepdims=True) + EPS)


def _gelu_tanh(t):
    return 0.5 * t * (1.0 + jnp.tanh(math.sqrt(2.0 / math.pi) * (t + 0.044715 * (t * t * t))))


def _silu(t):
    return t * (1.0 / (1.0 + jnp.exp(-t)))


def _const_spec(shape):
    nd = len(shape)
    return pl.BlockSpec(shape, lambda *_: (0,) * nd, pipeline_mode=pl.Buffered(1))


def _params(sem):
    return pltpu.CompilerParams(dimension_semantics=sem, vmem_limit_bytes=V7X_VMEM_LIMIT)


def _even_in_kernel(x_ref, nmix_ref, wuv_ref, wct_ref, gvn_ref, ws_ref, bs_ref, gqa_ref, wuqt_ref,
                    gkva_ref, wukvt_ref, gq_ref, gk_ref, cos_ref, sin_ref,
                    a_ref, qt_ref, k_ref, vt_ref):
    subs = range(x_ref.shape[0] // EVEN_IN_SUB)
    proj = [_even_in_project(sub, x_ref, nmix_ref, wuv_ref, wct_ref) for sub in subs]
    heads = [_even_in_gate_and_expand(sub, proj[sub], gvn_ref, ws_ref, bs_ref, gqa_ref, wuqt_ref, gkva_ref,
                                      wukvt_ref, a_ref) for sub in subs]
    for sub in subs:
        _even_in_heads(sub, heads[sub], gq_ref, gk_ref, cos_ref, sin_ref, qt_ref, k_ref, vt_ref)


def _even_in_project(sub, x_ref, nmix_ref, wuv_ref, wct_ref):
    rows = EVEN_IN_SUB
    x = x_ref[sub * rows:(sub + 1) * rows, :]
    h = (x * _rms_scale(x) * nmix_ref[...]).astype(BF16)
    zuv = jnp.dot(h, wuv_ref[...], preferred_element_type=F32)
    zc = lax.dot_general(wct_ref[...], h, NT_DIMS, preferred_element_type=F32)
    return zuv, zc


def _even_in_gate_and_expand(sub, proj, gvn_ref, ws_ref, bs_ref, gqa_ref, wuqt_ref, gkva_ref, wukvt_ref, a_ref):
    rows = EVEN_IN_SUB
    zuv, zc = proj
    u = _gelu_tanh(zuv[:, :GM_WIDTH])
    v = _gelu_tanh(zuv[:, GM_WIDTH:])
    t_out = lax.broadcasted_iota(jnp.int32, (GM_BLOCK, GM_BLOCK), 0) // CHUNK
    t_in = lax.broadcasted_iota(jnp.int32, (GM_BLOCK, GM_BLOCK), 1) // CHUNK
    causal = t_in <= t_out
    nblk = rows // GM_BLOCK
    for g in range(GM_GROUPS):
        cs = slice(g * GM_GROUP_DIM, (g + 1) * GM_GROUP_DIM)
        vg = v[:, cs]
        vn = (vg * _rms_scale(vg) * gvn_ref[:, cs]).astype(BF16)
        wg = jnp.where(causal, ws_ref[g], 0.0).astype(BF16)
        vcat = jnp.concatenate([vn[j * GM_BLOCK:(j + 1) * GM_BLOCK] for j in range(nblk)], axis=1)
        s = jnp.dot(wg, vcat, preferred_element_type=F32) + bs_ref[:, g:g + 1]
        for j in range(nblk):
            rs = slice(j * GM_BLOCK, (j + 1) * GM_BLOCK)
            dst = slice(sub * rows + j * GM_BLOCK, sub * rows + (j + 1) * GM_BLOCK)
            a_ref[dst, cs] = (u[rs, cs] * s[:, j * GM_BLOCK:(j + 1) * GM_BLOCK]).astype(a_ref.dtype)

    o1 = MLA_Q_RANK
    o2 = o1 + MLA_KV_RANK
    cq, ckv, kpe = zc[:o1], zc[o1:o2], zc[o2:]

    def col_rms(t, n):
        return lax.rsqrt(jnp.sum(t * t, axis=0, keepdims=True) * (1.0 / n) + EPS)

    cqn = (cq * col_rms(cq, MLA_Q_RANK) * gqa_ref[...]).astype(BF16)
    ckvn = (ckv * col_rms(ckv, MLA_KV_RANK) * gkva_ref[...]).astype(BF16)
    qt = jnp.dot(wuqt_ref[...], cqn, preferred_element_type=F32)
    kvt = jnp.dot(wukvt_ref[...], ckvn, preferred_element_type=F32)
    return qt, kvt, kpe


def _even_in_heads(sub, expanded, gq_ref, gk_ref, cos_ref, sin_ref, qt_ref, k_ref, vt_ref):
    rows = EVEN_IN_SUB
    qt, kvt, kpe = expanded
    tok = slice(sub * rows, (sub + 1) * rows)
    cos, sin = cos_ref[:, tok], sin_ref[:, tok]
    half = MLA_ROPE // 2
    n1, n2 = MLA_NOPE, MLA_NOPE + half

    def rope(x1, x2):
        return x1 * cos - x2 * sin, x2 * cos + x1 * sin

    gq = gq_ref[...] * (MLA_QK ** -0.5 * math.log2(math.e))
    for hd in range(MLA_HEADS):
        t = qt[hd * LANES:(hd + 1) * LANES]
        r = lax.rsqrt(jnp.sum(t * t, axis=0, keepdims=True) * (1.0 / MLA_QK) + EPS)
        tn = t * gq
        r1, r2 = rope(tn[n1:n2], tn[n2:MLA_QK])
        base = hd * LANES
        qt_ref[base:base + n1, tok] = (tn[:n1] * r).astype(qt_ref.dtype)
        qt_ref[base + n1:base + n2, tok] = (r1 * r).astype(qt_ref.dtype)
        qt_ref[base + n2:base + MLA_QK, tok] = (r2 * r).astype(qt_ref.dtype)
        qt_ref[base + MLA_QK:base + LANES, tok] = jnp.zeros((LANES - MLA_QK, rows), qt_ref.dtype)

    gk = gk_ref[...]
    kpe_ss = jnp.sum(kpe * kpe, axis=0, keepdims=True)
    kr1, kr2 = rope(kpe[:half] * gk[n1:n2], kpe[half:] * gk[n2:MLA_QK])
    pad = jnp.zeros((LANES - MLA_QK, rows), F32)
    for hd in range(MLA_HEADS):
        t = kvt[hd * MLA_NOPE:(hd + 1) * MLA_NOPE]
        r = lax.rsqrt((jnp.sum(t * t, axis=0, keepdims=True) + kpe_ss) * (1.0 / MLA_QK) + EPS)
        kt = jnp.concatenate([t * gk[:n1] * r, kr1 * r, kr2 * r, pad], axis=0)
        k_ref[tok, hd * LANES:(hd + 1) * LANES] = kt.T.astype(k_ref.dtype)
    vt_ref[sub] = kvt[MLA_HEADS * MLA_NOPE:].astype(vt_ref.dtype)


def _even_in(x2, nmix, wuv, wct, gvn, ws, bs_t, gqa, wuqt, gkva, wukvt, gq, gk, cos_t, sin_t, seq):
    tokens, d = x2.shape
    rows = EVEN_IN_ROWS
    per_seq = seq // rows
    row_spec = lambda w: pl.BlockSpec((rows, w), lambda i: (i, 0))
    tab_spec = pl.BlockSpec((MLA_ROPE // 2, rows), lambda i: (0, i % per_seq))
    hw = MLA_HEADS * LANES
    vw = MLA_HEADS * MLA_V
    consts = [nmix, wuv, wct, gvn, ws, bs_t, gqa, wuqt, gkva, wukvt, gq, gk]
    return pl.pallas_call(
        _even_in_kernel,
        grid=(tokens // rows,),
        in_specs=[row_spec(d)] + [_const_spec(c.shape) for c in consts] + [tab_spec, tab_spec],
        out_specs=[row_spec(GM_WIDTH),
                   pl.BlockSpec((hw, rows), lambda i: (0, i)),
                   row_spec(hw),
                   pl.BlockSpec((rows // EVEN_IN_SUB, vw, EVEN_IN_SUB), lambda i: (i, 0, 0))],
        out_shape=[jax.ShapeDtypeStruct((tokens, GM_WIDTH), BF16),
                   jax.ShapeDtypeStruct((hw, tokens), BF16),
                   jax.ShapeDtypeStruct((tokens, hw), BF16),
                   jax.ShapeDtypeStruct((tokens // EVEN_IN_SUB, vw, EVEN_IN_SUB), BF16)],
        compiler_params=_params(("parallel",)),
        name="even_in",
    )(x2, *consts, cos_t, sin_t)


def _attn_kernel(qt_ref, k_ref, vt_ref, o_ref, sa_ref, sb_ref, ma_ref, mb_ref, m_ref, acc_ref, *, n_groups):
    tq = ATTN_Q_ROWS
    tk = ATTN_KV_ROWS
    vrows = vt_ref.shape[2]
    ones = jnp.ones((ONES_ROWS, vrows), BF16)
    bufs = ((sa_ref, ma_ref), (sb_ref, mb_ref))
    units = [(hh, slice(u * ATTN_UNIT, (u + 1) * ATTN_UNIT)) for hh in range(2) for u in range(tq // ATTN_UNIT)]

    def run(group):
        first = group * ATTN_GROUP
        tasks = [(c, t) for c in range(first, first + ATTN_GROUP) for t in range(c + 1)]

        def key_rows(c, j, qs):
            return qs.stop if j == c else tk

        def scores(n, hh, qs):
            c, j = tasks[n]
            dst, bmax = bufs[n % 2]
            rows = key_rows(c, j, qs)
            q0 = (c - first) * tq
            s = jnp.dot(k_ref[j * tk:j * tk + rows, hh * LANES:(hh + 1) * LANES],
                        qt_ref[hh * LANES:(hh + 1) * LANES, q0 + qs.start:q0 + qs.stop],
                        preferred_element_type=F32)
            if j == c:
                ck = lax.broadcasted_iota(jnp.int32, s.shape, 0) // CHUNK
                cq = (lax.broadcasted_iota(jnp.int32, s.shape, 1) + qs.start) // CHUNK
                s = jnp.where(ck <= cq, s, -jnp.inf)
            dst[hh, :rows, qs] = s
            bmax[hh, :, qs] = jnp.max(s, axis=0, keepdims=True)

        def update(n, hh, qs):
            c, j = tasks[n]
            src, bmax = bufs[n % 2]
            rows = key_rows(c, j, qs)
            slot = c % 2
            m = m_ref[slot, hh, :, qs]
            m_new = jnp.maximum(m, bmax[hh, :, qs])
            alpha = jnp.exp2(m - m_new)
            p = jnp.exp2(src[hh, :rows, qs] - m_new).astype(BF16)
            m_ref[slot, hh, :, qs] = m_new
            pv = None
            for ch in range(rows // vrows):
                vt = vt_ref[j * (tk // vrows) + ch, hh * MLA_V:(hh + 1) * MLA_V, :]
                part = jnp.dot(jnp.concatenate([vt, ones], axis=0), p[ch * vrows:(ch + 1) * vrows],
                               preferred_element_type=F32)
                pv = part if pv is None else pv + part
            acc_ref[slot, hh, :, qs] = alpha * acc_ref[slot, hh, :, qs] + pv

        def start_block(c):
            m_ref[c % 2] = jnp.full(m_ref.shape[1:], -jnp.inf, F32)
            acc_ref[c % 2] = jnp.zeros(acc_ref.shape[1:], F32)

        def finish_block(c):
            acc = acc_ref[c % 2]
            ot = jnp.concatenate([acc[hh, :MLA_V] * (1.0 / acc[hh, MLA_V:MLA_V + 1]) for hh in range(2)],
                                 axis=0)
            o_ref[(c - first) * tq:(c - first + 1) * tq, :] = ot.T.astype(o_ref.dtype)

        start_block(first)
        for hh, qs in units:
            scores(0, hh, qs)
        for n, (c, j) in enumerate(tasks):
            has_next = n + 1 < len(tasks)
            if has_next and tasks[n + 1][1] == 0:
                start_block(tasks[n + 1][0])
            for hh, qs in units:
                if has_next:
                    scores(n + 1, hh, qs)
                update(n, hh, qs)
            if j == c:
                finish_block(c)

    for group in range(n_groups):
        pl.when(pl.program_id(2) == group)(functools.partial(run, group))


def _attention(qt, k3, vt, b, s):
    assert ATTN_Q_ROWS == ATTN_KV_ROWS and ATTN_UNIT % vt.shape[2] == 0
    tq = ATTN_Q_ROWS
    pairs = MLA_HEADS // 2
    vchunks, _, vrows = vt.shape
    rows = ATTN_GROUP * tq
    n_groups = s // rows
    return pl.pallas_call(
        functools.partial(_attn_kernel, n_groups=n_groups),
        grid=(b, pairs, n_groups),
        in_specs=[pl.BlockSpec((2 * LANES, rows), lambda bi, p, g: (p, bi * n_groups + g)),
                  pl.BlockSpec((None, s, 2 * LANES), lambda bi, p, g: (bi, 0, p)),
                  pl.BlockSpec((vchunks // b, 2 * MLA_V, vrows), lambda bi, p, g: (bi, p, 0))],
        out_specs=pl.BlockSpec((None, rows, LANES), lambda bi, p, g: (bi, g, p)),
        out_shape=jax.ShapeDtypeStruct((b, s, MLA_HEADS * MLA_V), BF16),
        scratch_shapes=[pltpu.VMEM((2, ATTN_KV_ROWS, tq), F32),
                        pltpu.VMEM((2, ATTN_KV_ROWS, tq), F32),
                        pltpu.VMEM((2, 1, tq), F32),
                        pltpu.VMEM((2, 1, tq), F32),
                        pltpu.VMEM((2, 2, 1, tq), F32),
                        pltpu.VMEM((2, 2, MLA_V + ONES_ROWS, tq), F32)],
        compiler_params=_params(("parallel", "parallel", "parallel")),
        name="attention",
    )(qt, k3, vt)


def _proj_ffn_kernel(*refs, n_in):
    in_refs = refs[:n_in]
    wout_ref, x_ref, nffn_ref, wg_ref, wu_ref, wd_ref, out_ref = refs[n_in:]
    mix = None
    off = 0
    for r in in_refs:
        kdim = r.shape[1]
        part = jnp.dot(r[...], wout_ref[off:off + kdim, :], preferred_element_type=F32)
        mix = part if mix is None else mix + part
        off += kdim
    x1 = x_ref[...] + mix
    h = (x1 * _rms_scale(x1) * nffn_ref[...]).astype(BF16)
    acc = x1
    c0 = 0
    for c in FF_CHUNKS:
        g = jnp.dot(h, wg_ref[:, c0:c0 + c], preferred_element_type=F32)
        u = jnp.dot(h, wu_ref[:, c0:c0 + c], preferred_element_type=F32)
        act = (_silu(g) * u).astype(BF16)
        acc = acc + jnp.dot(act, wd_ref[c0:c0 + c, :], preferred_element_type=F32)
        c0 += c
    out_ref[...] = acc


def _proj_ffn(mix_ins, wout, x2, nffn, wg, wu, wd, layer):
    tokens, d = x2.shape
    rows = FFN_ROWS
    assert sum(FF_CHUNKS) == wg.shape[2]
    row_spec = lambda w: pl.BlockSpec((rows, w), lambda i: (i, 0))
    layer_spec = lambda w: pl.BlockSpec((None,) + w.shape[1:], lambda i: (layer, 0, 0),
                                        pipeline_mode=pl.Buffered(1))
    return pl.pallas_call(
        functools.partial(_proj_ffn_kernel, n_in=len(mix_ins)),
        grid=(tokens // rows,),
        in_specs=[row_spec(m.shape[1]) for m in mix_ins]
        + [_const_spec(wout.shape), row_spec(d), _const_spec(nffn.shape),
           layer_spec(wg), layer_spec(wu), layer_spec(wd)],
        out_specs=row_spec(d),
        out_shape=jax.ShapeDtypeStruct((tokens, d), F32),
        compiler_params=_params(("parallel",)),
        name="proj_ffn",
    )(*mix_ins, wout, x2, nffn, wg, wu, wd)


def _odd_in_kernel(x_ref, nmix_ref, win_ref, cos_ref, sin_ref, q_ref, k_ref, v_ref, g_ref):
    x = x_ref[...]
    h = (x * _rms_scale(x) * nmix_ref[...]).astype(BF16)
    cos, sin = cos_ref[...], sin_ref[...]
    nq = RET_HEADS * RET_QK
    nv = RET_HEADS * RET_V
    half = RET_QK // 2

    def rope_store(dst, col0, scale):
        for hd in range(RET_HEADS):
            c = col0 + hd * RET_QK
            t = jnp.dot(h, win_ref[:, c:c + RET_QK], preferred_element_type=F32)
            t1, t2 = t[:, :half], t[:, half:]
            dst[:, hd * RET_QK:hd * RET_QK + half] = ((t1 * cos - t2 * sin) * scale).astype(dst.dtype)
            dst[:, hd * RET_QK + half:(hd + 1) * RET_QK] = ((t2 * cos + t1 * sin) * scale).astype(dst.dtype)

    rope_store(q_ref, 0, 1.0)
    rope_store(k_ref, nq, RET_QK ** -0.5)
    for hd in range(RET_HEADS):
        cs = slice(hd * RET_V, (hd + 1) * RET_V)
        v_ref[:, cs] = jnp.dot(h, win_ref[:, 2 * nq + hd * RET_V:2 * nq + (hd + 1) * RET_V],
                               preferred_element_type=F32).astype(v_ref.dtype)
        gate = jnp.dot(h, win_ref[:, 2 * nq + nv + hd * RET_V:2 * nq + nv + (hd + 1) * RET_V],
                       preferred_element_type=F32)
        g_ref[:, cs] = _silu(gate).astype(g_ref.dtype)


def _odd_in(x2, nmix, win, cos, sin, seq):
    tokens, d = x2.shape
    rows = ODD_IN_ROWS
    per_seq = seq // rows
    nq = RET_HEADS * RET_QK
    nv = RET_HEADS * RET_V
    row_spec = lambda w: pl.BlockSpec((rows, w), lambda i: (i, 0))
    tab_spec = pl.BlockSpec((rows, RET_QK // 2), lambda i: (i % per_seq, 0))
    return pl.pallas_call(
        _odd_in_kernel,
        grid=(tokens // rows,),
        in_specs=[row_spec(d), _const_spec(nmix.shape), _const_spec(win.shape), tab_spec, tab_spec],
        out_specs=[row_spec(nq), row_spec(nq), row_spec(nv), row_spec(nv)],
        out_shape=[jax.ShapeDtypeStruct((tokens, nq), BF16), jax.ShapeDtypeStruct((tokens, nq), BF16),
                   jax.ShapeDtypeStruct((tokens, nv), BF16), jax.ShapeDtypeStruct((tokens, nv), BF16)],
        compiler_params=_params(("parallel",)),
        name="odd_in",
    )(x2, nmix, win, cos, sin)


def _retention_kernel(q_ref, k_ref, v_ref, g_ref, ron_ref, o_ref, state_ref, decay_ref):
    tc = RET_CHUNK
    j = pl.program_id(1)

    def log_gamma(hd):
        return jnp.log(jnp.full((1, 1), 1.0 - 2.0 ** (-5.0 - hd), F32))

    @pl.when(j == 0)
    def _():
        state_ref[...] = jnp.zeros_like(state_ref)
        diff = (lax.broadcasted_iota(jnp.int32, (tc, tc), 0)
                - lax.broadcasted_iota(jnp.int32, (tc, tc), 1))
        dpos = jnp.maximum(diff, 0).astype(F32)
        for hd in range(RET_HEADS):
            decay_ref[hd] = jnp.where(diff >= 0, jnp.exp(log_gamma(hd) * dpos), 0.0)

    pos = lax.broadcasted_iota(jnp.int32, (tc, 1), 0).astype(F32)
    scales = []
    for hd in range(RET_HEADS):
        lg = log_gamma(hd)
        scales.append((jnp.exp(lg * (pos + 1.0)), jnp.exp(lg * (tc - 1.0 - pos)), jnp.exp(lg * float(tc))))
    for c in range(q_ref.shape[0] // tc):
        rs = slice(c * tc, (c + 1) * tc)
        for hd in range(RET_HEADS):
            xi, zeta, g_chunk = scales[hd]
            qs = slice(hd * RET_QK, (hd + 1) * RET_QK)
            vs = slice(hd * RET_V, (hd + 1) * RET_V)
            qh, kh, vh = q_ref[rs, qs], k_ref[rs, qs], v_ref[rs, vs]
            state = state_ref[hd]
            a = lax.dot_general(qh, kh, NT_DIMS, preferred_element_type=F32) * decay_ref[hd]
            o = (jnp.dot(a.astype(BF16), vh, preferred_element_type=F32)
                 + jnp.dot(qh, state.astype(BF16), preferred_element_type=F32) * xi)
            kz = (kh.astype(F32) * zeta).astype(BF16)
            state_ref[hd] = state * g_chunk + lax.dot_general(
                kz, vh, (((0,), (0,)), ((), ())), preferred_element_type=F32)
            on = o * _rms_scale(o) * ron_ref[:, vs]
            o_ref[rs, vs] = (on * g_ref[rs, vs].astype(F32)).astype(o_ref.dtype)


def _retention(q3, k3, v3, g3, ron):
    b, s, nq = q3.shape
    nv = v3.shape[2]
    tc = RET_ROWS
    blk = lambda w: pl.BlockSpec((None, tc, w), lambda bi, j: (bi, j, 0))
    return pl.pallas_call(
        _retention_kernel,
        grid=(b, s // tc),
        in_specs=[blk(nq), blk(nq), blk(nv), blk(nv), _const_spec(ron.shape)],
        out_specs=blk(nv),
        out_shape=jax.ShapeDtypeStruct((b, s, nv), BF16),
        scratch_shapes=[pltpu.VMEM((RET_HEADS, RET_QK, RET_V), F32),
                        pltpu.VMEM((RET_HEADS, RET_CHUNK, RET_CHUNK), F32)],
        compiler_params=_params(("arbitrary", "arbitrary")),
        name="retention",
    )(q3, k3, v3, g3, ron)


def _rope_angles(seq, half):
    inv = ROPE_THETA ** (-jnp.arange(half, dtype=F32) / half)
    return jnp.arange(seq, dtype=F32)[:, None] * inv[None, :]


def _lane_bcast(g, width):
    return jnp.broadcast_to(g[:, None], (g.shape[0], width))


def kernel(x, norm_mix, norm_ffn, even_w_in, gm_v_norm, gm_w_s, gm_b_s, mla_q_a_norm, mla_w_uq,
           mla_kv_a_norm, mla_w_ukv, mla_q_norm, mla_k_norm, even_w_out, odd_w_in, ret_out_norm,
           odd_w_out, ffn_w_gate, ffn_w_up, ffn_w_down):
    b, s, d = x.shape
    tokens = b * s
    x2 = x.reshape(tokens, d)
    row = lambda t: t.reshape(1, -1)

    w_in = even_w_in[0]
    wuv = w_in[:, :2 * GM_WIDTH].astype(BF16)
    wct = w_in[:, 2 * GM_WIDTH:].T.astype(BF16)
    wuq3 = mla_w_uq[0].reshape(MLA_Q_RANK, MLA_HEADS, MLA_QK)
    wuqt = jnp.pad(wuq3, ((0, 0), (0, 0), (0, LANES - MLA_QK))).reshape(MLA_Q_RANK, -1).T.astype(BF16)
    wukv3 = mla_w_ukv[0].reshape(MLA_KV_RANK, MLA_HEADS, MLA_NOPE + MLA_V)
    wukvt = jnp.concatenate([wukv3[:, :, :MLA_NOPE].reshape(MLA_KV_RANK, -1),
                             wukv3[:, :, MLA_NOPE:].reshape(MLA_KV_RANK, -1)], axis=1).T.astype(BF16)
    er = EVEN_IN_SUB
    pad_head = lambda g: jnp.pad(g, (0, LANES - MLA_QK))
    ang = _rope_angles(s, MLA_ROPE // 2)
    a, qt, k, vt = _even_in(
        x2, row(norm_mix[0]), wuv, wct, row(gm_v_norm[0]), gm_w_s[0], gm_b_s[0].T,
        _lane_bcast(mla_q_a_norm[0], er), wuqt, _lane_bcast(mla_kv_a_norm[0], er), wukvt,
        _lane_bcast(pad_head(mla_q_norm[0]), er), _lane_bcast(pad_head(mla_k_norm[0]), er),
        jnp.cos(ang).T, jnp.sin(ang).T, s)
    o = _attention(qt, k.reshape(b, s, MLA_HEADS * LANES), vt, b, s)
    wg, wu, wd = ffn_w_gate.astype(BF16), ffn_w_up.astype(BF16), ffn_w_down.astype(BF16)
    x2 = _proj_ffn([a, o.reshape(tokens, -1)], even_w_out[0].astype(BF16), x2, row(norm_ffn[0]), wg, wu, wd, 0)

    ang = _rope_angles(s, RET_QK // 2)
    rq, rk, rv, rg = _odd_in(x2, row(norm_mix[1]), odd_w_in[0].astype(BF16), jnp.cos(ang), jnp.sin(ang), s)
    nq = RET_HEADS * RET_QK
    nv = RET_HEADS * RET_V
    og = _retention(rq.reshape(b, s, nq), rk.reshape(b, s, nq), rv.reshape(b, s, nv),
                    rg.reshape(b, s, nv), row(ret_out_norm[0]))
    x2 = _proj_ffn([og.reshape(tokens, nv)], odd_w_out[0].astype(BF16), x2, row(norm_ffn[1]), wg, wu, wd, 1)
    return x2.reshape(b, s, d)
```

```python
import functools
import math

import jax
import jax.numpy as jnp
from jax import lax
from jax.experimental import pallas as pl
from jax.experimental.pallas import tpu as pltpu

F32 = jnp.float32
BF16 = jnp.bfloat16

EPS = 1e-6
ROPE_THETA = 10000.0
CHUNK = 64

GM_GROUPS = 4
GM_GROUP_DIM = 128
GM_WIDTH = GM_GROUPS * GM_GROUP_DIM
GM_BLOCK = 128
MLA_HEADS = 8
MLA_Q_RANK = 384
MLA_KV_RANK = 256
MLA_NOPE = 64
MLA_ROPE = 32
MLA_V = 64
MLA_QK = MLA_NOPE + MLA_ROPE
RET_HEADS = 4
RET_QK = 256
RET_V = 512

LANES = 128
BF16_SUBLANES = 16
ONES_ROWS = BF16_SUBLANES
V7X_VMEM_LIMIT = 56 * 1024 * 1024

EVEN_IN_ROWS = 512
EVEN_IN_SUB = 256
ATTN_Q_ROWS = 512
ATTN_KV_ROWS = 512
ATTN_UNIT = 256
ATTN_GROUP = 2
FFN_ROWS = 512
ODD_IN_ROWS = 512
RET_ROWS = 512
RET_CHUNK = 256
FF_CHUNKS = (768, 768, 768, 512)

NT_DIMS = (((1,), (1,)), ((), ()))


def _rms_scale(t):
    return lax.rsqrt(jnp.mean(t * t, axis=-1, keepdims=True) + EPS)


def _gelu_tanh(t):
    return 0.5 * t * (1.0 + jnp.tanh(math.sqrt(2.0 / math.pi) * (t + 0.044715 * (t * t * t))))


def _silu(t):
    return t * (1.0 / (1.0 + jnp.exp(-t)))


def _const_spec(shape):
    nd = len(shape)
    return pl.BlockSpec(shape, lambda *_: (0,) * nd, pipeline_mode=pl.Buffered(1))


def _params(sem):
    return pltpu.CompilerParams(dimension_semantics=sem, vmem_limit_bytes=V7X_VMEM_LIMIT)


def _even_in_kernel(x_ref, nmix_ref, wuv_ref, wct_ref, gvn_ref, ws_ref, bs_ref, gqa_ref, wuqt_ref,
                    gkva_ref, wukvt_ref, gq_ref, gk_ref, cos_ref, sin_ref,
                    a_ref, qt_ref, k_ref, vt_ref):
    subs = range(x_ref.shape[0] // EVEN_IN_SUB)
    proj = [_even_in_project(sub, x_ref, nmix_ref, wuv_ref, wct_ref) for sub in subs]
    heads = [_even_in_gate_and_expand(sub, proj[sub], gvn_ref, ws_ref, bs_ref, gqa_ref, wuqt_ref, gkva_ref,
                                      wukvt_ref, a_ref) for sub in subs]
    for sub in subs:
        _even_in_heads(sub, heads[sub], gq_ref, gk_ref, cos_ref, sin_ref, qt_ref, k_ref, vt_ref)


def _even_in_project(sub, x_ref, nmix_ref, wuv_ref, wct_ref):
    rows = EVEN_IN_SUB
    x = x_ref[sub * rows:(sub + 1) * rows, :]
    h = (x * _rms_scale(x) * nmix_ref[...]).astype(BF16)
    zuv = jnp.dot(h, wuv_ref[...], preferred_element_type=F32)
    zc = lax.dot_general(wct_ref[...], h, NT_DIMS, preferred_element_type=F32)
    return zuv, zc


def _even_in_gate_and_expand(sub, proj, gvn_ref, ws_ref, bs_ref, gqa_ref, wuqt_ref, gkva_ref, wukvt_ref, a_ref):
    rows = EVEN_IN_SUB
    zuv, zc = proj
    u = _gelu_tanh(zuv[:, :GM_WIDTH])
    v = _gelu_tanh(zuv[:, GM_WIDTH:])
    t_out = lax.broadcasted_iota(jnp.int32, (GM_BLOCK, GM_BLOCK), 0) // CHUNK
    t_in = lax.broadcasted_iota(jnp.int32, (GM_BLOCK, GM_BLOCK), 1) // CHUNK
    causal = t_in <= t_out
    nblk = rows // GM_BLOCK
    for g in range(GM_GROUPS):
        cs = slice(g * GM_GROUP_DIM, (g + 1) * GM_GROUP_DIM)
        vg = v[:, cs]
        vn = (vg * _rms_scale(vg) * gvn_ref[:, cs]).astype(BF16)
        wg = jnp.where(causal, ws_ref[g], 0.0).astype(BF16)
        vcat = jnp.concatenate([vn[j * GM_BLOCK:(j + 1) * GM_BLOCK] for j in range(nblk)], axis=1)
        s = jnp.dot(wg, vcat, preferred_element_type=F32) + bs_ref[:, g:g + 1]
        for j in range(nblk):
            rs = slice(j * GM_BLOCK, (j + 1) * GM_BLOCK)
            dst = slice(sub * rows + j * GM_BLOCK, sub * rows + (j + 1) * GM_BLOCK)
            a_ref[dst, cs] = (u[rs, cs] * s[:, j * GM_BLOCK:(j + 1) * GM_BLOCK]).astype(a_ref.dtype)

    o1 = MLA_Q_RANK
    o2 = o1 + MLA_KV_RANK
    cq, ckv, kpe = zc[:o1], zc[o1:o2], zc[o2:]

    def col_rms(t, n):
        return lax.rsqrt(jnp.sum(t * t, axis=0, keepdims=True) * (1.0 / n) + EPS)

    cqn = (cq * col_rms(cq, MLA_Q_RANK) * gqa_ref[...]).astype(BF16)
    ckvn = (ckv * col_rms(ckv, MLA_KV_RANK) * gkva_ref[...]).astype(BF16)
    qt = jnp.dot(wuqt_ref[...], cqn, preferred_element_type=F32)
    kvt = jnp.dot(wukvt_ref[...], ckvn, preferred_element_type=F32)
    return qt, kvt, kpe


def _even_in_heads(sub, expanded, gq_ref, gk_ref, cos_ref, sin_ref, qt_ref, k_ref, vt_ref):
    rows = EVEN_IN_SUB
    qt, kvt, kpe = expanded
    tok = slice(sub * rows, (sub + 1) * rows)
    cos, sin = cos_ref[:, tok], sin_ref[:, tok]
    half = MLA_ROPE // 2
    n1, n2 = MLA_NOPE, MLA_NOPE + half

    def rope(x1, x2):
        return x1 * cos - x2 * sin, x2 * cos + x1 * sin

    gq = gq_ref[...] * (MLA_QK ** -0.5 * math.log2(math.e))
    for hd in range(MLA_HEADS):
        t = qt[hd * LANES:(hd + 1) * LANES]
        r = lax.rsqrt(jnp.sum(t * t, axis=0, keepdims=True) * (1.0 / MLA_QK) + EPS)
        tn = t * gq
        r1, r2 = rope(tn[n1:n2], tn[n2:MLA_QK])
        base = hd * LANES
        qt_ref[base:base + n1, tok] = (tn[:n1] * r).astype(qt_ref.dtype)
        qt_ref[base + n1:base + n2, tok] = (r1 * r).astype(qt_ref.dtype)
        qt_ref[base + n2:base + MLA_QK, tok] = (r2 * r).astype(qt_ref.dtype)
        qt_ref[base + MLA_QK:base + LANES, tok] = jnp.zeros((LANES - MLA_QK, rows), qt_ref.dtype)

    gk = gk_ref[...]
    kpe_ss = jnp.sum(kpe * kpe, axis=0, keepdims=True)
    kr1, kr2 = rope(kpe[:half] * gk[n1:n2], kpe[half:] * gk[n2:MLA_QK])
    pad = jnp.zeros((LANES - MLA_QK, rows), F32)
    for hd in range(MLA_HEADS):
        t = kvt[hd * MLA_NOPE:(hd + 1) * MLA_NOPE]
        r = lax.rsqrt((jnp.sum(t * t, axis=0, keepdims=True) + kpe_ss) * (1.0 / MLA_QK) + EPS)
        kt = jnp.concatenate([t * gk[:n1] * r, kr1 * r, kr2 * r, pad], axis=0)
        k_ref[tok, hd * LANES:(hd + 1) * LANES] = kt.T.astype(k_ref.dtype)
    vt_ref[sub] = kvt[MLA_HEADS * MLA_NOPE:].astype(vt_ref.dtype)


def _even_in(x2, nmix, wuv, wct, gvn, ws, bs_t, gqa, wuqt, gkva, wukvt, gq, gk, cos_t, sin_t, seq):
    tokens, d = x2.shape
    rows = EVEN_IN_ROWS
    per_seq = seq // rows
    row_spec = lambda w: pl.BlockSpec((rows, w), lambda i: (i, 0))
    tab_spec = pl.BlockSpec((MLA_ROPE // 2, rows), lambda i: (0, i % per_seq))
    hw = MLA_HEADS * LANES
    vw = MLA_HEADS * MLA_V
    consts = [nmix, wuv, wct, gvn, ws, bs_t, gqa, wuqt, gkva, wukvt, gq, gk]
    return pl.pallas_call(
        _even_in_kernel,
        grid=(tokens // rows,),
        in_specs=[row_spec(d)] + [_const_spec(c.shape) for c in consts] + [tab_spec, tab_spec],
        out_specs=[row_spec(GM_WIDTH),
                   pl.BlockSpec((hw, rows), lambda i: (0, i)),
                   row_spec(hw),
                   pl.BlockSpec((rows // EVEN_IN_SUB, vw, EVEN_IN_SUB), lambda i: (i, 0, 0))],
        out_shape=[jax.ShapeDtypeStruct((tokens, GM_WIDTH), BF16),
                   jax.ShapeDtypeStruct((hw, tokens), BF16),
                   jax.ShapeDtypeStruct((tokens, hw), BF16),
                   jax.ShapeDtypeStruct((tokens // EVEN_IN_SUB, vw, EVEN_IN_SUB), BF16)],
        compiler_params=_params(("parallel",)),
        name="even_in",
    )(x2, *consts, cos_t, sin_t)


def _attn_kernel(*refs, n_groups, n_cast):
    qt_ref, k_ref, vt_ref = refs[:3]
    w_refs = refs[3:3 + n_cast]
    o_ref = refs[3 + n_cast]
    wb_refs = refs[4 + n_cast:4 + 2 * n_cast]
    sa_ref, sb_ref, ma_ref, mb_ref, m_ref, acc_ref = refs[4 + 2 * n_cast:]
    for w_ref, wb_ref in zip(w_refs, wb_refs):
        wb_ref[...] = w_ref[...].astype(wb_ref.dtype)

    tq = ATTN_Q_ROWS
    tk = ATTN_KV_ROWS
    vrows = vt_ref.shape[2]
    ones = jnp.ones((ONES_ROWS, vrows), BF16)
    bufs = ((sa_ref, ma_ref), (sb_ref, mb_ref))
    units = [(hh, slice(u * ATTN_UNIT, (u + 1) * ATTN_UNIT)) for hh in range(2) for u in range(tq // ATTN_UNIT)]

    def run(group):
        first = group * ATTN_GROUP
        tasks = [(c, t) for c in range(first, first + ATTN_GROUP) for t in range(c + 1)]

        def key_rows(c, j, qs):
            return qs.stop if j == c else tk

        def scores(n, hh, qs):
            c, j = tasks[n]
            dst, bmax = bufs[n % 2]
            rows = key_rows(c, j, qs)
            q0 = (c - first) * tq
            s = jnp.dot(k_ref[j * tk:j * tk + rows, hh * LANES:(hh + 1) * LANES],
                        qt_ref[hh * LANES:(hh + 1) * LANES, q0 + qs.start:q0 + qs.stop],
                        preferred_element_type=F32)
            if j == c:
                ck = lax.broadcasted_iota(jnp.int32, s.shape, 0) // CHUNK
                cq = (lax.broadcasted_iota(jnp.int32, s.shape, 1) + qs.start) // CHUNK
                s = jnp.where(ck <= cq, s, -jnp.inf)
            dst[hh, :rows, qs] = s
            bmax[hh, :, qs] = jnp.max(s, axis=0, keepdims=True)

        def update(n, hh, qs):
            c, j = tasks[n]
            src, bmax = bufs[n % 2]
            rows = key_rows(c, j, qs)
            slot = c % 2
            m = m_ref[slot, hh, :, qs]
            m_new = jnp.maximum(m, bmax[hh, :, qs])
            alpha = jnp.exp2(m - m_new)
            p = jnp.exp2(src[hh, :rows, qs] - m_new).astype(BF16)
            m_ref[slot, hh, :, qs] = m_new
            pv = None
            for ch in range(rows // vrows):
                vt = vt_ref[j * (tk // vrows) + ch, hh * MLA_V:(hh + 1) * MLA_V, :]
                part = jnp.dot(jnp.concatenate([vt, ones], axis=0), p[ch * vrows:(ch + 1) * vrows],
                               preferred_element_type=F32)
                pv = part if pv is None else pv + part
            acc_ref[slot, hh, :, qs] = alpha * acc_ref[slot, hh, :, qs] + pv

        def start_block(c):
            m_ref[c % 2] = jnp.full(m_ref.shape[1:], -jnp.inf, F32)
            acc_ref[c % 2] = jnp.zeros(acc_ref.shape[1:], F32)

        def finish_block(c):
            acc = acc_ref[c % 2]
            ot = jnp.concatenate([acc[hh, :MLA_V] * (1.0 / acc[hh, MLA_V:MLA_V + 1]) for hh in range(2)],
                                 axis=0)
            o_ref[(c - first) * tq:(c - first + 1) * tq, :] = ot.T.astype(o_ref.dtype)

        start_block(first)
        for hh, qs in units:
            scores(0, hh, qs)
        for n, (c, j) in enumerate(tasks):
            has_next = n + 1 < len(tasks)
            if has_next and tasks[n + 1][1] == 0:
                start_block(tasks[n + 1][0])
            for hh, qs in units:
                if has_next:
                    scores(n + 1, hh, qs)
                update(n, hh, qs)
            if j == c:
                finish_block(c)

    for group in range(n_groups):
        pl.when(pl.program_id(2) == group)(functools.partial(run, group))


def _attention(qt, k3, vt, b, s, cast_weights):
    assert ATTN_Q_ROWS == ATTN_KV_ROWS and ATTN_UNIT % vt.shape[2] == 0
    tq = ATTN_Q_ROWS
    pairs = MLA_HEADS // 2
    vchunks, _, vrows = vt.shape
    rows = ATTN_GROUP * tq
    n_groups = s // rows
    steps = b * pairs * n_groups
    step = lambda bi, p, g: (bi * pairs + p) * n_groups + g
    cast_specs = [pl.BlockSpec((w.shape[0] // steps, w.shape[1]), lambda bi, p, g: (step(bi, p, g), 0))
                  for w in cast_weights]
    assert all(w.shape[0] % (steps * BF16_SUBLANES) == 0 for w in cast_weights)
    return pl.pallas_call(
        functools.partial(_attn_kernel, n_groups=n_groups, n_cast=len(cast_weights)),
        grid=(b, pairs, n_groups),
        in_specs=[pl.BlockSpec((2 * LANES, rows), lambda bi, p, g: (p, bi * n_groups + g)),
                  pl.BlockSpec((None, s, 2 * LANES), lambda bi, p, g: (bi, 0, p)),
                  pl.BlockSpec((vchunks // b, 2 * MLA_V, vrows), lambda bi, p, g: (bi, p, 0))] + cast_specs,
        out_specs=[pl.BlockSpec((None, rows, LANES), lambda bi, p, g: (bi, g, p))] + cast_specs,
        out_shape=[jax.ShapeDtypeStruct((b, s, MLA_HEADS * MLA_V), BF16)]
        + [jax.ShapeDtypeStruct(w.shape, BF16) for w in cast_weights],
        scratch_shapes=[pltpu.VMEM((2, ATTN_KV_ROWS, tq), F32),
                        pltpu.VMEM((2, ATTN_KV_ROWS, tq), F32),
                        pltpu.VMEM((2, 1, tq), F32),
                        pltpu.VMEM((2, 1, tq), F32),
                        pltpu.VMEM((2, 2, 1, tq), F32),
                        pltpu.VMEM((2, 2, MLA_V + ONES_ROWS, tq), F32)],
        compiler_params=_params(("parallel", "parallel", "parallel")),
        name="attention",
    )(qt, k3, vt, *cast_weights)


def _proj_ffn_kernel(*refs, n_in):
    in_refs = refs[:n_in]
    wout_ref, x_ref, nffn_ref, wg_ref, wu_ref, wd_ref, out_ref = refs[n_in:]
    mix = None
    off = 0
    for r in in_refs:
        kdim = r.shape[1]
        part = jnp.dot(r[...], wout_ref[off:off + kdim, :], preferred_element_type=F32)
        mix = part if mix is None else mix + part
        off += kdim
    x1 = x_ref[...] + mix
    h = (x1 * _rms_scale(x1) * nffn_ref[...]).astype(BF16)
    acc = x1
    c0 = 0
    for c in FF_CHUNKS:
        g = jnp.dot(h, wg_ref[:, c0:c0 + c], preferred_element_type=F32)
        u = jnp.dot(h, wu_ref[:, c0:c0 + c], preferred_element_type=F32)
        act = (_silu(g) * u).astype(BF16)
        acc = acc + jnp.dot(act, wd_ref[c0:c0 + c, :], preferred_element_type=F32)
        c0 += c
    out_ref[...] = acc


def _proj_ffn(mix_ins, wout, x2, nffn, wg, wu, wd, layer):
    tokens, d = x2.shape
    rows = FFN_ROWS
    assert sum(FF_CHUNKS) == wg.shape[2]
    row_spec = lambda w: pl.BlockSpec((rows, w), lambda i: (i, 0))
    layer_spec = lambda w: pl.BlockSpec((None,) + w.shape[1:], lambda i: (layer, 0, 0),
                                        pipeline_mode=pl.Buffered(1))
    return pl.pallas_call(
        functools.partial(_proj_ffn_kernel, n_in=len(mix_ins)),
        grid=(tokens // rows,),
        in_specs=[row_spec(m.shape[1]) for m in mix_ins]
        + [_const_spec(wout.shape), row_spec(d), _const_spec(nffn.shape),
           layer_spec(wg), layer_spec(wu), layer_spec(wd)],
        out_specs=row_spec(d),
        out_shape=jax.ShapeDtypeStruct((tokens, d), F32),
        compiler_params=_params(("parallel",)),
        name="proj_ffn",
    )(*mix_ins, wout, x2, nffn, wg, wu, wd)


def _odd_in_kernel(x_ref, nmix_ref, win_ref, cos_ref, sin_ref, q_ref, k_ref, v_ref, g_ref):
    x = x_ref[...]
    h = (x * _rms_scale(x) * nmix_ref[...]).astype(BF16)
    cos, sin = cos_ref[...], sin_ref[...]
    nq = RET_HEADS * RET_QK
    nv = RET_HEADS * RET_V
    half = RET_QK // 2

    def rope_store(dst, col0, scale):
        for hd in range(RET_HEADS):
            c = col0 + hd * RET_QK
            t = jnp.dot(h, win_ref[:, c:c + RET_QK], preferred_element_type=F32)
            t1, t2 = t[:, :half], t[:, half:]
            dst[:, hd * RET_QK:hd * RET_QK + half] = ((t1 * cos - t2 * sin) * scale).astype(dst.dtype)
            dst[:, hd * RET_QK + half:(hd + 1) * RET_QK] = ((t2 * cos + t1 * sin) * scale).astype(dst.dtype)

    rope_store(q_ref, 0, 1.0)
    rope_store(k_ref, nq, RET_QK ** -0.5)
    for hd in range(RET_HEADS):
        cs = slice(hd * RET_V, (hd + 1) * RET_V)
        v_ref[:, cs] = jnp.dot(h, win_ref[:, 2 * nq + hd * RET_V:2 * nq + (hd + 1) * RET_V],
                               preferred_element_type=F32).astype(v_ref.dtype)
        gate = jnp.dot(h, win_ref[:, 2 * nq + nv + hd * RET_V:2 * nq + nv + (hd + 1) * RET_V],
                       preferred_element_type=F32)
        g_ref[:, cs] = _silu(gate).astype(g_ref.dtype)


def _odd_in(x2, nmix, win, cos, sin, seq):
    tokens, d = x2.shape
    rows = ODD_IN_ROWS
    per_seq = seq // rows
    nq = RET_HEADS * RET_QK
    nv = RET_HEADS * RET_V
    row_spec = lambda w: pl.BlockSpec((rows, w), lambda i: (i, 0))
    tab_spec = pl.BlockSpec((rows, RET_QK // 2), lambda i: (i % per_seq, 0))
    return pl.pallas_call(
        _odd_in_kernel,
        grid=(tokens // rows,),
        in_specs=[row_spec(d), _const_spec(nmix.shape), _const_spec(win.shape), tab_spec, tab_spec],
        out_specs=[row_spec(nq), row_spec(nq), row_spec(nv), row_spec(nv)],
        out_shape=[jax.ShapeDtypeStruct((tokens, nq), BF16), jax.ShapeDtypeStruct((tokens, nq), BF16),
                   jax.ShapeDtypeStruct((tokens, nv), BF16), jax.ShapeDtypeStruct((tokens, nv), BF16)],
        compiler_params=_params(("parallel",)),
        name="odd_in",
    )(x2, nmix, win, cos, sin)


def _retention_kernel(q_ref, k_ref, v_ref, g_ref, ron_ref, o_ref, state_ref, decay_ref):
    tc = RET_CHUNK
    j = pl.program_id(1)

    def log_gamma(hd):
        return jnp.log(jnp.full((1, 1), 1.0 - 2.0 ** (-5.0 - hd), F32))

    @pl.when(j == 0)
    def _():
        state_ref[...] = jnp.zeros_like(state_ref)
        diff = (lax.broadcasted_iota(jnp.int32, (tc, tc), 0)
                - lax.broadcasted_iota(jnp.int32, (tc, tc), 1))
        dpos = jnp.maximum(diff, 0).astype(F32)
        for hd in range(RET_HEADS):
            decay_ref[hd] = jnp.where(diff >= 0, jnp.exp(log_gamma(hd) * dpos), 0.0)

    pos = lax.broadcasted_iota(jnp.int32, (tc, 1), 0).astype(F32)
    scales = []
    for hd in range(RET_HEADS):
        lg = log_gamma(hd)
        scales.append((jnp.exp(lg * (pos + 1.0)), jnp.exp(lg * (tc - 1.0 - pos)), jnp.exp(lg * float(tc))))
    for c in range(q_ref.shape[0] // tc):
        rs = slice(c * tc, (c + 1) * tc)
        for hd in range(RET_HEADS):
            xi, zeta, g_chunk = scales[hd]
            qs = slice(hd * RET_QK, (hd + 1) * RET_QK)
            vs = slice(hd * RET_V, (hd + 1) * RET_V)
            qh, kh, vh = q_ref[rs, qs], k_ref[rs, qs], v_ref[rs, vs]
            state = state_ref[hd]
            a = lax.dot_general(qh, kh, NT_DIMS, preferred_element_type=F32) * decay_ref[hd]
            o = (jnp.dot(a.astype(BF16), vh, preferred_element_type=F32)
                 + jnp.dot(qh, state.astype(BF16), preferred_element_type=F32) * xi)
            kz = (kh.astype(F32) * zeta).astype(BF16)
            state_ref[hd] = state * g_chunk + lax.dot_general(
                kz, vh, (((0,), (0,)), ((), ())), preferred_element_type=F32)
            on = o * _rms_scale(o) * ron_ref[:, vs]
            o_ref[rs, vs] = (on * g_ref[rs, vs].astype(F32)).astype(o_ref.dtype)


def _retention(q3, k3, v3, g3, ron):
    b, s, nq = q3.shape
    nv = v3.shape[2]
    tc = RET_ROWS
    blk = lambda w: pl.BlockSpec((None, tc, w), lambda bi, j: (bi, j, 0))
    return pl.pallas_call(
        _retention_kernel,
        grid=(b, s // tc),
        in_specs=[blk(nq), blk(nq), blk(nv), blk(nv), _const_spec(ron.shape)],
        out_specs=blk(nv),
        out_shape=jax.ShapeDtypeStruct((b, s, nv), BF16),
        scratch_shapes=[pltpu.VMEM((RET_HEADS, RET_QK, RET_V), F32),
                        pltpu.VMEM((RET_HEADS, RET_CHUNK, RET_CHUNK), F32)],
        compiler_params=_params(("arbitrary", "arbitrary")),
        name="retention",
    )(q3, k3, v3, g3, ron)


def _rope_angles(seq, half):
    inv = ROPE_THETA ** (-jnp.arange(half, dtype=F32) / half)
    return jnp.arange(seq, dtype=F32)[:, None] * inv[None, :]


def _lane_bcast(g, width):
    return jnp.broadcast_to(g[:, None], (g.shape[0], width))


def kernel(x, norm_mix, norm_ffn, even_w_in, gm_v_norm, gm_w_s, gm_b_s, mla_q_a_norm, mla_w_uq,
           mla_kv_a_norm, mla_w_ukv, mla_q_norm, mla_k_norm, even_w_out, odd_w_in, ret_out_norm,
           odd_w_out, ffn_w_gate, ffn_w_up, ffn_w_down):
    b, s, d = x.shape
    tokens = b * s
    x2 = x.reshape(tokens, d)
    row = lambda t: t.reshape(1, -1)

    w_in = even_w_in[0]
    wuv = w_in[:, :2 * GM_WIDTH].astype(BF16)
    wct = w_in[:, 2 * GM_WIDTH:].T.astype(BF16)
    wuq3 = mla_w_uq[0].reshape(MLA_Q_RANK, MLA_HEADS, MLA_QK)
    wuqt = jnp.pad(wuq3, ((0, 0), (0, 0), (0, LANES - MLA_QK))).reshape(MLA_Q_RANK, -1).T.astype(BF16)
    wukv3 = mla_w_ukv[0].reshape(MLA_KV_RANK, MLA_HEADS, MLA_NOPE + MLA_V)
    wukvt = jnp.concatenate([wukv3[:, :, :MLA_NOPE].reshape(MLA_KV_RANK, -1),
                             wukv3[:, :, MLA_NOPE:].reshape(MLA_KV_RANK, -1)], axis=1).T.astype(BF16)
    er = EVEN_IN_SUB
    pad_head = lambda g: jnp.pad(g, (0, LANES - MLA_QK))
    ang = _rope_angles(s, MLA_ROPE // 2)
    a, qt, k, vt = _even_in(
        x2, row(norm_mix[0]), wuv, wct, row(gm_v_norm[0]), gm_w_s[0], gm_b_s[0].T,
        _lane_bcast(mla_q_a_norm[0], er), wuqt, _lane_bcast(mla_kv_a_norm[0], er), wukvt,
        _lane_bcast(pad_head(mla_q_norm[0]), er), _lane_bcast(pad_head(mla_k_norm[0]), er),
        jnp.cos(ang).T, jnp.sin(ang).T, s)
    later = [ffn_w_gate, ffn_w_up, ffn_w_down, even_w_out[0], odd_w_in[0], odd_w_out[0]]
    flat = lambda w: w if w.ndim == 2 else w.reshape(-1, ffn_w_gate.shape[-1])
    o, *cast = _attention(qt, k.reshape(b, s, MLA_HEADS * LANES), vt, b, s, [flat(w) for w in later])
    wg, wu, wd, w_eo, w_oi, w_oo = [c.reshape(w.shape) for c, w in zip(cast, later)]
    x2 = _proj_ffn([a, o.reshape(tokens, -1)], w_eo, x2, row(norm_ffn[0]), wg, wu, wd, 0)

    ang = _rope_angles(s, RET_QK // 2)
    rq, rk, rv, rg = _odd_in(x2, row(norm_mix[1]), w_oi, jnp.cos(ang), jnp.sin(ang), s)
    nq = RET_HEADS * RET_QK
    nv = RET_HEADS * RET_V
    og = _retention(rq.reshape(b, s, nq), rk.reshape(b, s, nq), rv.reshape(b, s, nv),
                    rg.reshape(b, s, nv), row(ret_out_norm[0]))
    x2 = _proj_ffn([og.reshape(tokens, nv)], w_oo, x2, row(norm_ffn[1]), wg, wu, wd, 1)
    return x2.reshape(b, s, d)
```

```python
import functools
import math

import jax
import jax.numpy as jnp
from jax import lax
from jax.experimental import pallas as pl
from jax.experimental.pallas import tpu as pltpu

F32 = jnp.float32
BF16 = jnp.bfloat16

EPS = 1e-6
ROPE_THETA = 10000.0
CHUNK = 64

GM_GROUPS = 4
GM_GROUP_DIM = 128
GM_WIDTH = GM_GROUPS * GM_GROUP_DIM
GM_BLOCK = 128
MLA_HEADS = 8
MLA_Q_RANK = 384
MLA_KV_RANK = 256
MLA_NOPE = 64
MLA_ROPE = 32
MLA_V = 64
MLA_QK = MLA_NOPE + MLA_ROPE
RET_HEADS = 4
RET_QK = 256
RET_V = 512

LANES = 128
BF16_SUBLANES = 16
ONES_ROWS = BF16_SUBLANES
V7X_VMEM_LIMIT = 56 * 1024 * 1024

EVEN_IN_ROWS = 512
EVEN_IN_SUB = 256
ATTN_Q_ROWS = 512
ATTN_KV_ROWS = 512
ATTN_UNIT = 256
ATTN_GROUP = 2
FFN_ROWS = 512
ODD_IN_ROWS = 512
RET_ROWS = 512
RET_CHUNK = 256
FF_CHUNKS = (768, 768, 768, 512)

NT_DIMS = (((1,), (1,)), ((), ()))


def _rms_scale(t):
    return lax.rsqrt(jnp.mean(t * t, axis=-1, keepdims=True) + EPS)


def _gelu_tanh(t):
    return 0.5 * t * (1.0 + jnp.tanh(math.sqrt(2.0 / math.pi) * (t + 0.044715 * (t * t * t))))


def _silu(t):
    return t * (1.0 / (1.0 + jnp.exp(-t)))


def _const_spec(shape):
    nd = len(shape)
    return pl.BlockSpec(shape, lambda *_: (0,) * nd, pipeline_mode=pl.Buffered(1))


def _params(sem):
    return pltpu.CompilerParams(dimension_semantics=sem, vmem_limit_bytes=V7X_VMEM_LIMIT)


def _even_in_kernel(x_ref, nmix_ref, wuv_ref, wct_ref, gvn_ref, ws_ref, bs_ref, gqa_ref, wuqt_ref,
                    gkva_ref, wukvt_ref, gq_ref, gk_ref, cos_ref, sin_ref,
                    a_ref, qt_ref, k_ref, vt_ref):
    subs = range(x_ref.shape[0] // EVEN_IN_SUB)
    proj = [_even_in_project(sub, x_ref, nmix_ref, wuv_ref, wct_ref) for sub in subs]
    heads = [_even_in_gate_and_expand(sub, proj[sub], gvn_ref, ws_ref, bs_ref, gqa_ref, wuqt_ref, gkva_ref,
                                      wukvt_ref, a_ref) for sub in subs]
    for sub in subs:
        _even_in_heads(sub, heads[sub], gq_ref, gk_ref, cos_ref, sin_ref, qt_ref, k_ref, vt_ref)


def _even_in_project(sub, x_ref, nmix_ref, wuv_ref, wct_ref):
    rows = EVEN_IN_SUB
    x = x_ref[sub * rows:(sub + 1) * rows, :]
    h = (x * _rms_scale(x) * nmix_ref[...]).astype(BF16)
    zuv = jnp.dot(h, wuv_ref[...], preferred_element_type=F32)
    zc = lax.dot_general(wct_ref[...], h, NT_DIMS, preferred_element_type=F32)
    return zuv, zc


def _even_in_gate_and_expand(sub, proj, gvn_ref, ws_ref, bs_ref, gqa_ref, wuqt_ref, gkva_ref, wukvt_ref, a_ref):
    rows = EVEN_IN_SUB
    zuv, zc = proj
    u = _gelu_tanh(zuv[:, :GM_WIDTH])
    v = _gelu_tanh(zuv[:, GM_WIDTH:])
    t_out = lax.broadcasted_iota(jnp.int32, (GM_BLOCK, GM_BLOCK), 0) // CHUNK
    t_in = lax.broadcasted_iota(jnp.int32, (GM_BLOCK, GM_BLOCK), 1) // CHUNK
    causal = t_in <= t_out
    nblk = rows // GM_BLOCK
    for g in range(GM_GROUPS):
        cs = slice(g * GM_GROUP_DIM, (g + 1) * GM_GROUP_DIM)
        vg = v[:, cs]
        vn = (vg * _rms_scale(vg) * gvn_ref[:, cs]).astype(BF16)
        wg = jnp.where(causal, ws_ref[g], 0.0).astype(BF16)
        vcat = jnp.concatenate([vn[j * GM_BLOCK:(j + 1) * GM_BLOCK] for j in range(nblk)], axis=1)
        s = jnp.dot(wg, vcat, preferred_element_type=F32) + bs_ref[:, g:g + 1]
        for j in range(nblk):
            rs = slice(j * GM_BLOCK, (j + 1) * GM_BLOCK)
            dst = slice(sub * rows + j * GM_BLOCK, sub * rows + (j + 1) * GM_BLOCK)
            a_ref[dst, cs] = (u[rs, cs] * s[:, j * GM_BLOCK:(j + 1) * GM_BLOCK]).astype(a_ref.dtype)

    o1 = MLA_Q_RANK
    o2 = o1 + MLA_KV_RANK
    cq, ckv, kpe = zc[:o1], zc[o1:o2], zc[o2:]

    def col_rms(t, n):
        return lax.rsqrt(jnp.sum(t * t, axis=0, keepdims=True) * (1.0 / n) + EPS)

    cqn = (cq * col_rms(cq, MLA_Q_RANK) * gqa_ref[...]).astype(BF16)
    ckvn = (ckv * col_rms(ckv, MLA_KV_RANK) * gkva_ref[...]).astype(BF16)
    qt = jnp.dot(wuqt_ref[...], cqn, preferred_element_type=F32)
    kvt = jnp.dot(wukvt_ref[...], ckvn, preferred_element_type=F32)
    return qt, kvt, kpe


def _even_in_heads(sub, expanded, gq_ref, gk_ref, cos_ref, sin_ref, qt_ref, k_ref, vt_ref):
    rows = EVEN_IN_SUB
    qt, kvt, kpe = expanded
    tok = slice(sub * rows, (sub + 1) * rows)
    cos, sin = cos_ref[:, tok], sin_ref[:, tok]
    half = MLA_ROPE // 2
    n1, n2 = MLA_NOPE, MLA_NOPE + half

    def rope(x1, x2):
        return x1 * cos - x2 * sin, x2 * cos + x1 * sin

    gq = gq_ref[...] * (MLA_QK ** -0.5 * math.log2(math.e))
    for hd in range(MLA_HEADS):
        t = qt[hd * LANES:(hd + 1) * LANES]
        r = lax.rsqrt(jnp.sum(t * t, axis=0, keepdims=True) * (1.0 / MLA_QK) + EPS)
        tn = t * gq
        r1, r2 = rope(tn[n1:n2], tn[n2:MLA_QK])
        base = hd * LANES
        qt_ref[base:base + n1, tok] = (tn[:n1] * r).astype(qt_ref.dtype)
        qt_ref[base + n1:base + n2, tok] = (r1 * r).astype(qt_ref.dtype)
        qt_ref[base + n2:base + MLA_QK, tok] = (r2 * r).astype(qt_ref.dtype)
        qt_ref[base + MLA_QK:base + LANES, tok] = jnp.zeros((LANES - MLA_QK, rows), qt_ref.dtype)

    gk = gk_ref[...]
    kpe_ss = jnp.sum(kpe * kpe, axis=0, keepdims=True)
    kr1, kr2 = rope(kpe[:half] * gk[n1:n2], kpe[half:] * gk[n2:MLA_QK])
    pad = jnp.zeros((LANES - MLA_QK, rows), F32)
    for hd in range(MLA_HEADS):
        t = kvt[hd * MLA_NOPE:(hd + 1) * MLA_NOPE]
        r = lax.rsqrt((jnp.sum(t * t, axis=0, keepdims=True) + kpe_ss) * (1.0 / MLA_QK) + EPS)
        kt = jnp.concatenate([t * gk[:n1] * r, kr1 * r, kr2 * r, pad], axis=0)
        k_ref[tok, hd * LANES:(hd + 1) * LANES] = kt.T.astype(k_ref.dtype)
    vt_ref[sub] = kvt[MLA_HEADS * MLA_NOPE:].astype(vt_ref.dtype)


def _even_in(x2, nmix, wuv, wct, gvn, ws, bs_t, gqa, wuqt, gkva, wukvt, gq, gk, cos_t, sin_t, seq):
    tokens, d = x2.shape
    rows = EVEN_IN_ROWS
    per_seq = seq // rows
    row_spec = lambda w: pl.BlockSpec((rows, w), lambda i: (i, 0))
    tab_spec = pl.BlockSpec((MLA_ROPE // 2, rows), lambda i: (0, i % per_seq))
    hw = MLA_HEADS * LANES
    vw = MLA_HEADS * MLA_V
    consts = [nmix, wuv, wct, gvn, ws, bs_t, gqa, wuqt, gkva, wukvt, gq, gk]
    return pl.pallas_call(
        _even_in_kernel,
        grid=(tokens // rows,),
        in_specs=[row_spec(d)] + [_const_spec(c.shape) for c in consts] + [tab_spec, tab_spec],
        out_specs=[row_spec(GM_WIDTH),
                   pl.BlockSpec((hw, rows), lambda i: (0, i)),
                   row_spec(hw),
                   pl.BlockSpec((rows // EVEN_IN_SUB, vw, EVEN_IN_SUB), lambda i: (i, 0, 0))],
        out_shape=[jax.ShapeDtypeStruct((tokens, GM_WIDTH), BF16),
                   jax.ShapeDtypeStruct((hw, tokens), BF16),
                   jax.ShapeDtypeStruct((tokens, hw), BF16),
                   jax.ShapeDtypeStruct((tokens // EVEN_IN_SUB, vw, EVEN_IN_SUB), BF16)],
        compiler_params=_params(("parallel",)),
        name="even_in",
    )(x2, *consts, cos_t, sin_t)


def _attn_kernel(*refs, n_groups, n_cast):
    qt_ref, k_ref, vt_ref = refs[:3]
    w_refs = refs[3:3 + n_cast]
    o_ref = refs[3 + n_cast]
    wb_refs = refs[4 + n_cast:4 + 2 * n_cast]
    sa_ref, sb_ref, ma_ref, mb_ref, m_ref, acc_ref = refs[4 + 2 * n_cast:]

    def cast_weights():
        for w_ref, wb_ref in zip(w_refs, wb_refs):
            wb_ref[...] = w_ref[...].astype(wb_ref.dtype)

    tq = ATTN_Q_ROWS
    tk = ATTN_KV_ROWS
    vrows = vt_ref.shape[2]
    ones = jnp.ones((ONES_ROWS, vrows), BF16)
    bufs = ((sa_ref, ma_ref), (sb_ref, mb_ref))
    units = [(hh, slice(u * ATTN_UNIT, (u + 1) * ATTN_UNIT)) for hh in range(2) for u in range(tq // ATTN_UNIT)]

    def run(group):
        first = group * ATTN_GROUP
        tasks = [(c, t) for c in range(first, first + ATTN_GROUP) for t in range(c + 1)]

        def key_rows(c, j, qs):
            return qs.stop if j == c else tk

        def scores(n, hh, qs):
            c, j = tasks[n]
            dst, bmax = bufs[n % 2]
            rows = key_rows(c, j, qs)
            q0 = (c - first) * tq
            s = jnp.dot(k_ref[j * tk:j * tk + rows, hh * LANES:(hh + 1) * LANES],
                        qt_ref[hh * LANES:(hh + 1) * LANES, q0 + qs.start:q0 + qs.stop],
                        preferred_element_type=F32)
            if j == c:
                ck = lax.broadcasted_iota(jnp.int32, s.shape, 0) // CHUNK
                cq = (lax.broadcasted_iota(jnp.int32, s.shape, 1) + qs.start) // CHUNK
                s = jnp.where(ck <= cq, s, -jnp.inf)
            dst[hh, :rows, qs] = s
            bmax[hh, :, qs] = jnp.max(s, axis=0, keepdims=True)

        def update(n, hh, qs):
            c, j = tasks[n]
            src, bmax = bufs[n % 2]
            rows = key_rows(c, j, qs)
            slot = c % 2
            m = m_ref[slot, hh, :, qs]
            m_new = jnp.maximum(m, bmax[hh, :, qs])
            alpha = jnp.exp2(m - m_new)
            p = jnp.exp2(src[hh, :rows, qs] - m_new).astype(BF16)
            m_ref[slot, hh, :, qs] = m_new
            pv = None
            for ch in range(rows // vrows):
                vt = vt_ref[j * (tk // vrows) + ch, hh * MLA_V:(hh + 1) * MLA_V, :]
                part = jnp.dot(jnp.concatenate([vt, ones], axis=0), p[ch * vrows:(ch + 1) * vrows],
                               preferred_element_type=F32)
                pv = part if pv is None else pv + part
            acc_ref[slot, hh, :, qs] = alpha * acc_ref[slot, hh, :, qs] + pv

        def start_block(c):
            m_ref[c % 2] = jnp.full(m_ref.shape[1:], -jnp.inf, F32)
            acc_ref[c % 2] = jnp.zeros(acc_ref.shape[1:], F32)

        def finish_block(c):
            acc = acc_ref[c % 2]
            ot = jnp.concatenate([acc[hh, :MLA_V] * (1.0 / acc[hh, MLA_V:MLA_V + 1]) for hh in range(2)],
                                 axis=0)
            o_ref[(c - first) * tq:(c - first + 1) * tq, :] = ot.T.astype(o_ref.dtype)

        start_block(first)
        for hh, qs in units:
            scores(0, hh, qs)
        for n, (c, j) in enumerate(tasks):
            has_next = n + 1 < len(tasks)
            if has_next and tasks[n + 1][1] == 0:
                start_block(tasks[n + 1][0])
            for hh, qs in units:
                if has_next:
                    scores(n + 1, hh, qs)
                update(n, hh, qs)
            if n == 0:
                cast_weights()
            if j == c:
                finish_block(c)

    for group in range(n_groups):
        pl.when(pl.program_id(2) == group)(functools.partial(run, group))


def _attention(qt, k3, vt, b, s, cast_weights):
    assert ATTN_Q_ROWS == ATTN_KV_ROWS and ATTN_UNIT % vt.shape[2] == 0
    tq = ATTN_Q_ROWS
    pairs = MLA_HEADS // 2
    vchunks, _, vrows = vt.shape
    rows = ATTN_GROUP * tq
    n_groups = s // rows
    steps = b * pairs * n_groups
    step = lambda bi, p, g: (bi * pairs + p) * n_groups + g

    def cast_spec(w):
        share = next(sh for sh in (1, 2, 4, 8) if w.shape[0] % (steps // sh * BF16_SUBLANES) == 0)
        return pl.BlockSpec((w.shape[0] // (steps // share), w.shape[1]),
                            lambda bi, p, g: (step(bi, p, g) // share, 0))

    cast_specs = [cast_spec(w) for w in cast_weights]
    return pl.pallas_call(
        functools.partial(_attn_kernel, n_groups=n_groups, n_cast=len(cast_weights)),
        grid=(b, pairs, n_groups),
        in_specs=[pl.BlockSpec((2 * LANES, rows), lambda bi, p, g: (p, bi * n_groups + g)),
                  pl.BlockSpec((None, s, 2 * LANES), lambda bi, p, g: (bi, 0, p)),
                  pl.BlockSpec((vchunks // b, 2 * MLA_V, vrows), lambda bi, p, g: (bi, p, 0))] + cast_specs,
        out_specs=[pl.BlockSpec((None, rows, LANES), lambda bi, p, g: (bi, g, p))] + cast_specs,
        out_shape=[jax.ShapeDtypeStruct((b, s, MLA_HEADS * MLA_V), BF16)]
        + [jax.ShapeDtypeStruct(w.shape, BF16) for w in cast_weights],
        scratch_shapes=[pltpu.VMEM((2, ATTN_KV_ROWS, tq), F32),
                        pltpu.VMEM((2, ATTN_KV_ROWS, tq), F32),
                        pltpu.VMEM((2, 1, tq), F32),
                        pltpu.VMEM((2, 1, tq), F32),
                        pltpu.VMEM((2, 2, 1, tq), F32),
                        pltpu.VMEM((2, 2, MLA_V + ONES_ROWS, tq), F32)],
        compiler_params=_params(("arbitrary", "arbitrary", "arbitrary")),
        name="attention",
    )(qt, k3, vt, *cast_weights)


def _proj_ffn_kernel(*refs, n_in):
    in_refs = refs[:n_in]
    wout_ref, x_ref, nffn_ref, wg_ref, wu_ref, wd_ref, out_ref = refs[n_in:]
    mix = None
    off = 0
    for r in in_refs:
        kdim = r.shape[1]
        part = jnp.dot(r[...], wout_ref[off:off + kdim, :], preferred_element_type=F32)
        mix = part if mix is None else mix + part
        off += kdim
    x1 = x_ref[...] + mix
    h = (x1 * _rms_scale(x1) * nffn_ref[...]).astype(BF16)
    acc = x1
    c0 = 0
    for c in FF_CHUNKS:
        g = jnp.dot(h, wg_ref[:, c0:c0 + c], preferred_element_type=F32)
        u = jnp.dot(h, wu_ref[:, c0:c0 + c], preferred_element_type=F32)
        act = (_silu(g) * u).astype(BF16)
        acc = acc + jnp.dot(act, wd_ref[c0:c0 + c, :], preferred_element_type=F32)
        c0 += c
    out_ref[...] = acc


def _proj_ffn(mix_ins, wout, x2, nffn, wg, wu, wd, layer):
    tokens, d = x2.shape
    rows = FFN_ROWS
    assert sum(FF_CHUNKS) == wg.shape[2]
    row_spec = lambda w: pl.BlockSpec((rows, w), lambda i: (i, 0))
    layer_spec = lambda w: pl.BlockSpec((None,) + w.shape[1:], lambda i: (layer, 0, 0),
                                        pipeline_mode=pl.Buffered(1))
    return pl.pallas_call(
        functools.partial(_proj_ffn_kernel, n_in=len(mix_ins)),
        grid=(tokens // rows,),
        in_specs=[row_spec(m.shape[1]) for m in mix_ins]
        + [_const_spec(wout.shape), row_spec(d), _const_spec(nffn.shape),
           layer_spec(wg), layer_spec(wu), layer_spec(wd)],
        out_specs=row_spec(d),
        out_shape=jax.ShapeDtypeStruct((tokens, d), F32),
        compiler_params=_params(("parallel",)),
        name="proj_ffn",
    )(*mix_ins, wout, x2, nffn, wg, wu, wd)


def _odd_in_kernel(x_ref, nmix_ref, win_ref, cos_ref, sin_ref, q_ref, k_ref, v_ref, g_ref):
    x = x_ref[...]
    h = (x * _rms_scale(x) * nmix_ref[...]).astype(BF16)
    cos, sin = cos_ref[...], sin_ref[...]
    nq = RET_HEADS * RET_QK
    nv = RET_HEADS * RET_V
    half = RET_QK // 2

    def rope_store(dst, col0, scale):
        for hd in range(RET_HEADS):
            c = col0 + hd * RET_QK
            t = jnp.dot(h, win_ref[:, c:c + RET_QK], preferred_element_type=F32)
            t1, t2 = t[:, :half], t[:, half:]
            dst[:, hd * RET_QK:hd * RET_QK + half] = ((t1 * cos - t2 * sin) * scale).astype(dst.dtype)
            dst[:, hd * RET_QK + half:(hd + 1) * RET_QK] = ((t2 * cos + t1 * sin) * scale).astype(dst.dtype)

    rope_store(q_ref, 0, 1.0)
    rope_store(k_ref, nq, RET_QK ** -0.5)
    for hd in range(RET_HEADS):
        cs = slice(hd * RET_V, (hd + 1) * RET_V)
        v_ref[:, cs] = jnp.dot(h, win_ref[:, 2 * nq + hd * RET_V:2 * nq + (hd + 1) * RET_V],
                               preferred_element_type=F32).astype(v_ref.dtype)
        gate = jnp.dot(h, win_ref[:, 2 * nq + nv + hd * RET_V:2 * nq + nv + (hd + 1) * RET_V],
                       preferred_element_type=F32)
        g_ref[:, cs] = _silu(gate).astype(g_ref.dtype)


def _odd_in(x2, nmix, win, cos, sin, seq):
    tokens, d = x2.shape
    rows = ODD_IN_ROWS
    per_seq = seq // rows
    nq = RET_HEADS * RET_QK
    nv = RET_HEADS * RET_V
    row_spec = lambda w: pl.BlockSpec((rows, w), lambda i: (i, 0))
    tab_spec = pl.BlockSpec((rows, RET_QK // 2), lambda i: (i % per_seq, 0))
    return pl.pallas_call(
        _odd_in_kernel,
        grid=(tokens // rows,),
        in_specs=[row_spec(d), _const_spec(nmix.shape), _const_spec(win.shape), tab_spec, tab_spec],
        out_specs=[row_spec(nq), row_spec(nq), row_spec(nv), row_spec(nv)],
        out_shape=[jax.ShapeDtypeStruct((tokens, nq), BF16), jax.ShapeDtypeStruct((tokens, nq), BF16),
                   jax.ShapeDtypeStruct((tokens, nv), BF16), jax.ShapeDtypeStruct((tokens, nv), BF16)],
        compiler_params=_params(("parallel",)),
        name="odd_in",
    )(x2, nmix, win, cos, sin)


def _retention_kernel(q_ref, k_ref, v_ref, g_ref, ron_ref, o_ref, state_ref, decay_ref):
    tc = RET_CHUNK
    j = pl.program_id(1)

    def log_gamma(hd):
        return jnp.log(jnp.full((1, 1), 1.0 - 2.0 ** (-5.0 - hd), F32))

    @pl.when(j == 0)
    def _():
        state_ref[...] = jnp.zeros_like(state_ref)
        diff = (lax.broadcasted_iota(jnp.int32, (tc, tc), 0)
                - lax.broadcasted_iota(jnp.int32, (tc, tc), 1))
        dpos = jnp.maximum(diff, 0).astype(F32)
        for hd in range(RET_HEADS):
            decay_ref[hd] = jnp.where(diff >= 0, jnp.exp(log_gamma(hd) * dpos), 0.0)

    pos = lax.broadcasted_iota(jnp.int32, (tc, 1), 0).astype(F32)
    scales = []
    for hd in range(RET_HEADS):
        lg = log_gamma(hd)
        scales.append((jnp.exp(lg * (pos + 1.0)), jnp.exp(lg * (tc - 1.0 - pos)), jnp.exp(lg * float(tc))))
    for c in range(q_ref.shape[0] // tc):
        rs = slice(c * tc, (c + 1) * tc)
        for hd in range(RET_HEADS):
            xi, zeta, g_chunk = scales[hd]
            qs = slice(hd * RET_QK, (hd + 1) * RET_QK)
            vs = slice(hd * RET_V, (hd + 1) * RET_V)
            qh, kh, vh = q_ref[rs, qs], k_ref[rs, qs], v_ref[rs, vs]
            state = state_ref[hd]
            a = lax.dot_general(qh, kh, NT_DIMS, preferred_element_type=F32) * decay_ref[hd]
            o = (jnp.dot(a.astype(BF16), vh, preferred_element_type=F32)
                 + jnp.dot(qh, state.astype(BF16), preferred_element_type=F32) * xi)
            kz = (kh.astype(F32) * zeta).astype(BF16)
            state_ref[hd] = state * g_chunk + lax.dot_general(
                kz, vh, (((0,), (0,)), ((), ())), preferred_element_type=F32)
            on = o * _rms_scale(o) * ron_ref[:, vs]
            o_ref[rs, vs] = (on * g_ref[rs, vs].astype(F32)).astype(o_ref.dtype)


def _retention(q3, k3, v3, g3, ron):
    b, s, nq = q3.shape
    nv = v3.shape[2]
    tc = RET_ROWS
    blk = lambda w: pl.BlockSpec((None, tc, w), lambda bi, j: (bi, j, 0))
    return pl.pallas_call(
        _retention_kernel,
        grid=(b, s // tc),
        in_specs=[blk(nq), blk(nq), blk(nv), blk(nv), _const_spec(ron.shape)],
        out_specs=blk(nv),
        out_shape=jax.ShapeDtypeStruct((b, s, nv), BF16),
        scratch_shapes=[pltpu.VMEM((RET_HEADS, RET_QK, RET_V), F32),
                        pltpu.VMEM((RET_HEADS, RET_CHUNK, RET_CHUNK), F32)],
        compiler_params=_params(("arbitrary", "arbitrary")),
        name="retention",
    )(q3, k3, v3, g3, ron)


def _rope_angles(seq, half):
    inv = ROPE_THETA ** (-jnp.arange(half, dtype=F32) / half)
    return jnp.arange(seq, dtype=F32)[:, None] * inv[None, :]


def _lane_bcast(g, width):
    return jnp.broadcast_to(g[:, None], (g.shape[0], width))


def kernel(x, norm_mix, norm_ffn, even_w_in, gm_v_norm, gm_w_s, gm_b_s, mla_q_a_norm, mla_w_uq,
           mla_kv_a_norm, mla_w_ukv, mla_q_norm, mla_k_norm, even_w_out, odd_w_in, ret_out_norm,
           odd_w_out, ffn_w_gate, ffn_w_up, ffn_w_down):
    b, s, d = x.shape
    tokens = b * s
    x2 = x.reshape(tokens, d)
    row = lambda t: t.reshape(1, -1)

    w_in = even_w_in[0]
    wuv = w_in[:, :2 * GM_WIDTH].astype(BF16)
    wct = w_in[:, 2 * GM_WIDTH:].T.astype(BF16)
    wuq3 = mla_w_uq[0].reshape(MLA_Q_RANK, MLA_HEADS, MLA_QK)
    wuqt = jnp.pad(wuq3, ((0, 0), (0, 0), (0, LANES - MLA_QK))).reshape(MLA_Q_RANK, -1).T.astype(BF16)
    wukv3 = mla_w_ukv[0].reshape(MLA_KV_RANK, MLA_HEADS, MLA_NOPE + MLA_V)
    wukvt = jnp.concatenate([wukv3[:, :, :MLA_NOPE].reshape(MLA_KV_RANK, -1),
                             wukv3[:, :, MLA_NOPE:].reshape(MLA_KV_RANK, -1)], axis=1).T.astype(BF16)
    er = EVEN_IN_SUB
    pad_head = lambda g: jnp.pad(g, (0, LANES - MLA_QK))
    ang = _rope_angles(s, MLA_ROPE // 2)
    a, qt, k, vt = _even_in(
        x2, row(norm_mix[0]), wuv, wct, row(gm_v_norm[0]), gm_w_s[0], gm_b_s[0].T,
        _lane_bcast(mla_q_a_norm[0], er), wuqt, _lane_bcast(mla_kv_a_norm[0], er), wukvt,
        _lane_bcast(pad_head(mla_q_norm[0]), er), _lane_bcast(pad_head(mla_k_norm[0]), er),
        jnp.cos(ang).T, jnp.sin(ang).T, s)
    later = [ffn_w_gate, ffn_w_up, ffn_w_down, even_w_out[0], odd_w_in[0], odd_w_out[0]]
    flat = lambda w: w.reshape(-1, w.shape[-1])
    o, *cast = _attention(qt, k.reshape(b, s, MLA_HEADS * LANES), vt, b, s, [flat(w) for w in later])
    wg, wu, wd, w_eo, w_oi, w_oo = [c.reshape(w.shape) for c, w in zip(cast, later)]
    x2 = _proj_ffn([a, o.reshape(tokens, -1)], w_eo, x2, row(norm_ffn[0]), wg, wu, wd, 0)

    ang = _rope_angles(s, RET_QK // 2)
    rq, rk, rv, rg = _odd_in(x2, row(norm_mix[1]), w_oi, jnp.cos(ang), jnp.sin(ang), s)
    nq = RET_HEADS * RET_QK
    nv = RET_HEADS * RET_V
    og = _retention(rq.reshape(b, s, nq), rk.reshape(b, s, nq), rv.reshape(b, s, nv),
                    rg.reshape(b, s, nv), row(ret_out_norm[0]))
    x2 = _proj_ffn([og.reshape(tokens, nv)], w_oo, x2, row(norm_ffn[1]), wg, wu, wd, 1)
    return x2.reshape(b, s, d)
```

```python
import functools
import math

import jax
import jax.numpy as jnp
from jax import lax
from jax.experimental import pallas as pl
from jax.experimental.pallas import tpu as pltpu

F32 = jnp.float32
BF16 = jnp.bfloat16

EPS = 1e-6
ROPE_THETA = 10000.0
CHUNK = 64

GM_GROUPS = 4
GM_GROUP_DIM = 128
GM_WIDTH = GM_GROUPS * GM_GROUP_DIM
GM_BLOCK = 128
MLA_HEADS = 8
MLA_Q_RANK = 384
MLA_KV_RANK = 256
MLA_NOPE = 64
MLA_ROPE = 32
MLA_V = 64
MLA_QK = MLA_NOPE + MLA_ROPE
RET_HEADS = 4
RET_QK = 256
RET_V = 512

LANES = 128
BF16_SUBLANES = 16
ONES_ROWS = BF16_SUBLANES
V7X_VMEM_LIMIT = 56 * 1024 * 1024

EVEN_IN_ROWS = 512
EVEN_IN_SUB = 256
ATTN_Q_ROWS = 512
ATTN_KV_ROWS = 512
ATTN_UNIT = 256
ATTN_GROUP = 2
FFN_ROWS = 512
ODD_IN_ROWS = 512
ODD_IN_COLS = 1024
RET_ROWS = 512
RET_CHUNK = 256
FF_CHUNKS = (768, 768, 768, 512)

NT_DIMS = (((1,), (1,)), ((), ()))


def _rms_scale(t):
    return lax.rsqrt(jnp.mean(t * t, axis=-1, keepdims=True) + EPS)


def _gelu_tanh(t):
    return 0.5 * t * (1.0 + jnp.tanh(math.sqrt(2.0 / math.pi) * (t + 0.044715 * (t * t * t))))


def _silu(t):
    return t * (1.0 / (1.0 + jnp.exp(-t)))


def _const_spec(shape):
    nd = len(shape)
    return pl.BlockSpec(shape, lambda *_: (0,) * nd, pipeline_mode=pl.Buffered(1))


def _params(sem):
    return pltpu.CompilerParams(dimension_semantics=sem, vmem_limit_bytes=V7X_VMEM_LIMIT)


def _even_in_kernel(x_ref, nmix_ref, wuv_ref, wct_ref, gvn_ref, ws_ref, bs_ref, gqa_ref, wuqt_ref,
                    gkva_ref, wukvt_ref, gq_ref, gk_ref, cos_ref, sin_ref,
                    a_ref, qt_ref, k_ref, vt_ref):
    subs = range(x_ref.shape[0] // EVEN_IN_SUB)
    proj = [_even_in_project(sub, x_ref, nmix_ref, wuv_ref, wct_ref) for sub in subs]
    heads = [_even_in_gate_and_expand(sub, proj[sub], gvn_ref, ws_ref, bs_ref, gqa_ref, wuqt_ref, gkva_ref,
                                      wukvt_ref, a_ref) for sub in subs]
    for sub in subs:
        _even_in_heads(sub, heads[sub], gq_ref, gk_ref, cos_ref, sin_ref, qt_ref, k_ref, vt_ref)


def _even_in_project(sub, x_ref, nmix_ref, wuv_ref, wct_ref):
    rows = EVEN_IN_SUB
    x = x_ref[sub * rows:(sub + 1) * rows, :]
    h = (x * _rms_scale(x) * nmix_ref[...]).astype(BF16)
    zuv = jnp.dot(h, wuv_ref[...], preferred_element_type=F32)
    zc = lax.dot_general(wct_ref[...], h, NT_DIMS, preferred_element_type=F32)
    return zuv, zc


def _even_in_gate_and_expand(sub, proj, gvn_ref, ws_ref, bs_ref, gqa_ref, wuqt_ref, gkva_ref, wukvt_ref, a_ref):
    rows = EVEN_IN_SUB
    zuv, zc = proj
    u = _gelu_tanh(zuv[:, :GM_WIDTH])
    v = _gelu_tanh(zuv[:, GM_WIDTH:])
    t_out = lax.broadcasted_iota(jnp.int32, (GM_BLOCK, GM_BLOCK), 0) // CHUNK
    t_in = lax.broadcasted_iota(jnp.int32, (GM_BLOCK, GM_BLOCK), 1) // CHUNK
    causal = t_in <= t_out
    nblk = rows // GM_BLOCK
    for g in range(GM_GROUPS):
        cs = slice(g * GM_GROUP_DIM, (g + 1) * GM_GROUP_DIM)
        vg = v[:, cs]
        vn = (vg * _rms_scale(vg) * gvn_ref[:, cs]).astype(BF16)
        wg = jnp.where(causal, ws_ref[g], 0.0).astype(BF16)
        vcat = jnp.concatenate([vn[j * GM_BLOCK:(j + 1) * GM_BLOCK] for j in range(nblk)], axis=1)
        s = jnp.dot(wg, vcat, preferred_element_type=F32) + bs_ref[:, g:g + 1]
        for j in range(nblk):
            rs = slice(j * GM_BLOCK, (j + 1) * GM_BLOCK)
            dst = slice(sub * rows + j * GM_BLOCK, sub * rows + (j + 1) * GM_BLOCK)
            a_ref[dst, cs] = (u[rs, cs] * s[:, j * GM_BLOCK:(j + 1) * GM_BLOCK]).astype(a_ref.dtype)

    o1 = MLA_Q_RANK
    o2 = o1 + MLA_KV_RANK
    cq, ckv, kpe = zc[:o1], zc[o1:o2], zc[o2:]

    def col_rms(t, n):
        return lax.rsqrt(jnp.sum(t * t, axis=0, keepdims=True) * (1.0 / n) + EPS)

    cqn = (cq * col_rms(cq, MLA_Q_RANK) * gqa_ref[...]).astype(BF16)
    ckvn = (ckv * col_rms(ckv, MLA_KV_RANK) * gkva_ref[...]).astype(BF16)
    qt = jnp.dot(wuqt_ref[...], cqn, preferred_element_type=F32)
    kvt = jnp.dot(wukvt_ref[...], ckvn, preferred_element_type=F32)
    return qt, kvt, kpe


def _even_in_heads(sub, expanded, gq_ref, gk_ref, cos_ref, sin_ref, qt_ref, k_ref, vt_ref):
    rows = EVEN_IN_SUB
    qt, kvt, kpe = expanded
    tok = slice(sub * rows, (sub + 1) * rows)
    cos, sin = cos_ref[:, tok], sin_ref[:, tok]
    half = MLA_ROPE // 2
    n1, n2 = MLA_NOPE, MLA_NOPE + half

    def rope(x1, x2):
        return x1 * cos - x2 * sin, x2 * cos + x1 * sin

    gq = gq_ref[...] * (MLA_QK ** -0.5 * math.log2(math.e))
    for hd in range(MLA_HEADS):
        t = qt[hd * LANES:(hd + 1) * LANES]
        r = lax.rsqrt(jnp.sum(t * t, axis=0, keepdims=True) * (1.0 / MLA_QK) + EPS)
        tn = t * gq
        r1, r2 = rope(tn[n1:n2], tn[n2:MLA_QK])
        base = hd * LANES
        qt_ref[base:base + n1, tok] = (tn[:n1] * r).astype(qt_ref.dtype)
        qt_ref[base + n1:base + n2, tok] = (r1 * r).astype(qt_ref.dtype)
        qt_ref[base + n2:base + MLA_QK, tok] = (r2 * r).astype(qt_ref.dtype)
        qt_ref[base + MLA_QK:base + LANES, tok] = jnp.zeros((LANES - MLA_QK, rows), qt_ref.dtype)

    gk = gk_ref[...]
    kpe_ss = jnp.sum(kpe * kpe, axis=0, keepdims=True)
    kr1, kr2 = rope(kpe[:half] * gk[n1:n2], kpe[half:] * gk[n2:MLA_QK])
    pad = jnp.zeros((LANES - MLA_QK, rows), F32)
    for hd in range(MLA_HEADS):
        t = kvt[hd * MLA_NOPE:(hd + 1) * MLA_NOPE]
        r = lax.rsqrt((jnp.sum(t * t, axis=0, keepdims=True) + kpe_ss) * (1.0 / MLA_QK) + EPS)
        kt = jnp.concatenate([t * gk[:n1] * r, kr1 * r, kr2 * r, pad], axis=0)
        k_ref[tok, hd * LANES:(hd + 1) * LANES] = kt.T.astype(k_ref.dtype)
    vt_ref[sub] = kvt[MLA_HEADS * MLA_NOPE:].astype(vt_ref.dtype)


def _even_in(x2, nmix, wuv, wct, gvn, ws, bs_t, gqa, wuqt, gkva, wukvt, gq, gk, cos_t, sin_t, seq):
    tokens, d = x2.shape
    rows = EVEN_IN_ROWS
    per_seq = seq // rows
    row_spec = lambda w: pl.BlockSpec((rows, w), lambda i: (i, 0))
    tab_spec = pl.BlockSpec((MLA_ROPE // 2, rows), lambda i: (0, i % per_seq))
    hw = MLA_HEADS * LANES
    vw = MLA_HEADS * MLA_V
    consts = [nmix, wuv, wct, gvn, ws, bs_t, gqa, wuqt, gkva, wukvt, gq, gk]
    return pl.pallas_call(
        _even_in_kernel,
        grid=(tokens // rows,),
        in_specs=[row_spec(d)] + [_const_spec(c.shape) for c in consts] + [tab_spec, tab_spec],
        out_specs=[row_spec(GM_WIDTH),
                   pl.BlockSpec((hw, rows), lambda i: (0, i)),
                   row_spec(hw),
                   pl.BlockSpec((rows // EVEN_IN_SUB, vw, EVEN_IN_SUB), lambda i: (i, 0, 0))],
        out_shape=[jax.ShapeDtypeStruct((tokens, GM_WIDTH), BF16),
                   jax.ShapeDtypeStruct((hw, tokens), BF16),
                   jax.ShapeDtypeStruct((tokens, hw), BF16),
                   jax.ShapeDtypeStruct((tokens // EVEN_IN_SUB, vw, EVEN_IN_SUB), BF16)],
        compiler_params=_params(("parallel",)),
        name="even_in",
    )(x2, *consts, cos_t, sin_t)


def _attn_kernel(*refs, n_groups, n_cast):
    qt_ref, k_ref, vt_ref = refs[:3]
    w_refs = refs[3:3 + n_cast]
    o_ref = refs[3 + n_cast]
    wb_refs = refs[4 + n_cast:4 + 2 * n_cast]
    sa_ref, sb_ref, ma_ref, mb_ref, m_ref, acc_ref = refs[4 + 2 * n_cast:]

    def cast_weights():
        for w_ref, wb_ref in zip(w_refs, wb_refs):
            wb_ref[...] = w_ref[...].astype(wb_ref.dtype)

    tq = ATTN_Q_ROWS
    tk = ATTN_KV_ROWS
    vrows = vt_ref.shape[2]
    ones = jnp.ones((ONES_ROWS, vrows), BF16)
    bufs = ((sa_ref, ma_ref), (sb_ref, mb_ref))
    units = [(hh, slice(u * ATTN_UNIT, (u + 1) * ATTN_UNIT)) for hh in range(2) for u in range(tq // ATTN_UNIT)]

    def run(group):
        first = group * ATTN_GROUP
        tasks = [(c, t) for c in range(first, first + ATTN_GROUP) for t in range(c + 1)]

        def key_rows(c, j, qs):
            return qs.stop if j == c else tk

        def scores(n, hh, qs):
            c, j = tasks[n]
            dst, bmax = bufs[n % 2]
            rows = key_rows(c, j, qs)
            q0 = (c - first) * tq
            s = jnp.dot(k_ref[j * tk:j * tk + rows, hh * LANES:(hh + 1) * LANES],
                        qt_ref[hh * LANES:(hh + 1) * LANES, q0 + qs.start:q0 + qs.stop],
                        preferred_element_type=F32)
            if j == c:
                ck = lax.broadcasted_iota(jnp.int32, s.shape, 0) // CHUNK
                cq = (lax.broadcasted_iota(jnp.int32, s.shape, 1) + qs.start) // CHUNK
                s = jnp.where(ck <= cq, s, -jnp.inf)
            dst[hh, :rows, qs] = s
            bmax[hh, :, qs] = jnp.max(s, axis=0, keepdims=True)

        def update(n, hh, qs):
            c, j = tasks[n]
            src, bmax = bufs[n % 2]
            rows = key_rows(c, j, qs)
            slot = c % 2
            m = m_ref[slot, hh, :, qs]
            m_new = jnp.maximum(m, bmax[hh, :, qs])
            alpha = jnp.exp2(m - m_new)
            p = jnp.exp2(src[hh, :rows, qs] - m_new).astype(BF16)
            m_ref[slot, hh, :, qs] = m_new
            pv = None
            for ch in range(rows // vrows):
                vt = vt_ref[j * (tk // vrows) + ch, hh * MLA_V:(hh + 1) * MLA_V, :]
                part = jnp.dot(jnp.concatenate([vt, ones], axis=0), p[ch * vrows:(ch + 1) * vrows],
                               preferred_element_type=F32)
                pv = part if pv is None else pv + part
            acc_ref[slot, hh, :, qs] = alpha * acc_ref[slot, hh, :, qs] + pv

        def start_block(c):
            m_ref[c % 2] = jnp.full(m_ref.shape[1:], -jnp.inf, F32)
            acc_ref[c % 2] = jnp.zeros(acc_ref.shape[1:], F32)

        def finish_block(c):
            acc = acc_ref[c % 2]
            ot = jnp.concatenate([acc[hh, :MLA_V] * (1.0 / acc[hh, MLA_V:MLA_V + 1]) for hh in range(2)],
                                 axis=0)
            o_ref[(c - first) * tq:(c - first + 1) * tq, :] = ot.T.astype(o_ref.dtype)

        start_block(first)
        for hh, qs in units:
            scores(0, hh, qs)
        for n, (c, j) in enumerate(tasks):
            has_next = n + 1 < len(tasks)
            if has_next and tasks[n + 1][1] == 0:
                start_block(tasks[n + 1][0])
            for hh, qs in units:
                if has_next:
                    scores(n + 1, hh, qs)
                update(n, hh, qs)
            if n == 0:
                cast_weights()
            if j == c:
                finish_block(c)

    for group in range(n_groups):
        pl.when(pl.program_id(2) == group)(functools.partial(run, group))


def _attention(qt, k3, vt, b, s, cast_weights):
    assert ATTN_Q_ROWS == ATTN_KV_ROWS and ATTN_UNIT % vt.shape[2] == 0
    tq = ATTN_Q_ROWS
    pairs = MLA_HEADS // 2
    vchunks, _, vrows = vt.shape
    rows = ATTN_GROUP * tq
    n_groups = s // rows
    steps = b * pairs * n_groups
    step = lambda bi, p, g: (bi * pairs + p) * n_groups + g

    def cast_spec(w):
        share = next(sh for sh in (1, 2, 4, 8) if w.shape[0] % (steps // sh * BF16_SUBLANES) == 0)
        return pl.BlockSpec((w.shape[0] // (steps // share), w.shape[1]),
                            lambda bi, p, g: (step(bi, p, g) // share, 0))

    cast_specs = [cast_spec(w) for w in cast_weights]
    return pl.pallas_call(
        functools.partial(_attn_kernel, n_groups=n_groups, n_cast=len(cast_weights)),
        grid=(b, pairs, n_groups),
        in_specs=[pl.BlockSpec((2 * LANES, rows), lambda bi, p, g: (p, bi * n_groups + g)),
                  pl.BlockSpec((None, s, 2 * LANES), lambda bi, p, g: (bi, 0, p)),
                  pl.BlockSpec((vchunks // b, 2 * MLA_V, vrows), lambda bi, p, g: (bi, p, 0))] + cast_specs,
        out_specs=[pl.BlockSpec((None, rows, LANES), lambda bi, p, g: (bi, g, p))] + cast_specs,
        out_shape=[jax.ShapeDtypeStruct((b, s, MLA_HEADS * MLA_V), BF16)]
        + [jax.ShapeDtypeStruct(w.shape, BF16) for w in cast_weights],
        scratch_shapes=[pltpu.VMEM((2, ATTN_KV_ROWS, tq), F32),
                        pltpu.VMEM((2, ATTN_KV_ROWS, tq), F32),
                        pltpu.VMEM((2, 1, tq), F32),
                        pltpu.VMEM((2, 1, tq), F32),
                        pltpu.VMEM((2, 2, 1, tq), F32),
                        pltpu.VMEM((2, 2, MLA_V + ONES_ROWS, tq), F32)],
        compiler_params=_params(("arbitrary", "arbitrary", "arbitrary")),
        name="attention",
    )(qt, k3, vt, *cast_weights)


def _proj_ffn_kernel(*refs, n_in):
    in_refs = refs[:n_in]
    wout_ref, x_ref, nffn_ref, wg_ref, wu_ref, wd_ref, out_ref = refs[n_in:]
    mix = None
    off = 0
    for r in in_refs:
        kdim = r.shape[1]
        part = jnp.dot(r[...], wout_ref[off:off + kdim, :], preferred_element_type=F32)
        mix = part if mix is None else mix + part
        off += kdim
    x1 = x_ref[...] + mix
    h = (x1 * _rms_scale(x1) * nffn_ref[...]).astype(BF16)
    acc = x1
    c0 = 0
    for c in FF_CHUNKS:
        g = jnp.dot(h, wg_ref[:, c0:c0 + c], preferred_element_type=F32)
        u = jnp.dot(h, wu_ref[:, c0:c0 + c], preferred_element_type=F32)
        act = (_silu(g) * u).astype(BF16)
        acc = acc + jnp.dot(act, wd_ref[c0:c0 + c, :], preferred_element_type=F32)
        c0 += c
    out_ref[...] = acc


def _proj_ffn(mix_ins, wout, x2, nffn, wg, wu, wd, layer):
    tokens, d = x2.shape
    rows = FFN_ROWS
    assert sum(FF_CHUNKS) == wg.shape[2]
    row_spec = lambda w: pl.BlockSpec((rows, w), lambda i: (i, 0))
    layer_spec = lambda w: pl.BlockSpec((None,) + w.shape[1:], lambda i: (layer, 0, 0),
                                        pipeline_mode=pl.Buffered(1))
    return pl.pallas_call(
        functools.partial(_proj_ffn_kernel, n_in=len(mix_ins)),
        grid=(tokens // rows,),
        in_specs=[row_spec(m.shape[1]) for m in mix_ins]
        + [_const_spec(wout.shape), row_spec(d), _const_spec(nffn.shape),
           layer_spec(wg), layer_spec(wu), layer_spec(wd)],
        out_specs=row_spec(d),
        out_shape=jax.ShapeDtypeStruct((tokens, d), F32),
        compiler_params=_params(("parallel",)),
        name="proj_ffn",
    )(*mix_ins, wout, x2, nffn, wg, wu, wd)


def _odd_in_kernel(x_ref, nmix_ref, win_ref, cos_ref, sin_ref, q_ref, k_ref, v_ref, g_ref):
    x = x_ref[...]
    h = (x * _rms_scale(x) * nmix_ref[...]).astype(BF16)
    cos, sin = cos_ref[...], sin_ref[...]
    nq = RET_HEADS * RET_QK
    nv = RET_HEADS * RET_V
    half = RET_QK // 2

    def project(col0, width):
        return jnp.dot(h, win_ref[:, col0:col0 + width], preferred_element_type=F32)

    def rope_store(dst, col0, scale):
        t_all = project(col0, nq)
        for hd in range(RET_HEADS):
            t1 = t_all[:, hd * RET_QK:hd * RET_QK + half]
            t2 = t_all[:, hd * RET_QK + half:(hd + 1) * RET_QK]
            dst[:, hd * RET_QK:hd * RET_QK + half] = ((t1 * cos - t2 * sin) * scale).astype(dst.dtype)
            dst[:, hd * RET_QK + half:(hd + 1) * RET_QK] = ((t2 * cos + t1 * sin) * scale).astype(dst.dtype)

    rope_store(q_ref, 0, 1.0)
    rope_store(k_ref, nq, RET_QK ** -0.5)
    for c0 in range(0, nv, ODD_IN_COLS):
        g_ref[:, c0:c0 + ODD_IN_COLS] = _silu(project(2 * nq + nv + c0, ODD_IN_COLS)).astype(g_ref.dtype)
    for c0 in range(0, nv, ODD_IN_COLS):
        v_ref[:, c0:c0 + ODD_IN_COLS] = project(2 * nq + c0, ODD_IN_COLS).astype(v_ref.dtype)


def _odd_in(x2, nmix, win, cos, sin, seq):
    tokens, d = x2.shape
    rows = ODD_IN_ROWS
    per_seq = seq // rows
    nq = RET_HEADS * RET_QK
    nv = RET_HEADS * RET_V
    row_spec = lambda w: pl.BlockSpec((rows, w), lambda i: (i, 0))
    tab_spec = pl.BlockSpec((rows, RET_QK // 2), lambda i: (i % per_seq, 0))
    return pl.pallas_call(
        _odd_in_kernel,
        grid=(tokens // rows,),
        in_specs=[row_spec(d), _const_spec(nmix.shape), _const_spec(win.shape), tab_spec, tab_spec],
        out_specs=[row_spec(nq), row_spec(nq), row_spec(nv), row_spec(nv)],
        out_shape=[jax.ShapeDtypeStruct((tokens, nq), BF16), jax.ShapeDtypeStruct((tokens, nq), BF16),
                   jax.ShapeDtypeStruct((tokens, nv), BF16), jax.ShapeDtypeStruct((tokens, nv), BF16)],
        compiler_params=_params(("parallel",)),
        name="odd_in",
    )(x2, nmix, win, cos, sin)


def _retention_kernel(q_ref, k_ref, v_ref, g_ref, ron_ref, o_ref, state_ref, decay_ref, zeta_ref):
    tc = RET_CHUNK
    j = pl.program_id(1)

    def log_gamma(hd):
        return jnp.log(jnp.full((1, 1), 1.0 - 2.0 ** (-5.0 - hd), F32))

    @pl.when(j == 0)
    def _():
        state_ref[...] = jnp.zeros_like(state_ref)
        diff = (lax.broadcasted_iota(jnp.int32, (tc, tc), 0)
                - lax.broadcasted_iota(jnp.int32, (tc, tc), 1))
        dpos = jnp.maximum(diff, 0).astype(F32)
        kpos = lax.broadcasted_iota(jnp.int32, (tc, RET_QK), 0).astype(F32)
        for hd in range(RET_HEADS):
            decay_ref[hd] = jnp.where(diff >= 0, jnp.exp(log_gamma(hd) * dpos), 0.0).astype(decay_ref.dtype)
            zeta_ref[hd] = jnp.exp(log_gamma(hd) * (tc - 1.0 - kpos)).astype(zeta_ref.dtype)

    pos = lax.broadcasted_iota(jnp.int32, (tc, 1), 0).astype(F32)
    scales = []
    for hd in range(RET_HEADS):
        lg = log_gamma(hd)
        scales.append((jnp.exp(lg * (pos + 1.0)), jnp.exp(lg * float(tc))))
    for c in range(q_ref.shape[0] // tc):
        rs = slice(c * tc, (c + 1) * tc)
        for hd in range(RET_HEADS):
            xi, g_chunk = scales[hd]
            qs = slice(hd * RET_QK, (hd + 1) * RET_QK)
            vs = slice(hd * RET_V, (hd + 1) * RET_V)
            qh, kh, vh = q_ref[rs, qs], k_ref[rs, qs], v_ref[rs, vs]
            state = state_ref[hd]
            a = lax.dot_general(qh, kh, NT_DIMS, preferred_element_type=F32).astype(BF16) * decay_ref[hd]
            o = (jnp.dot(a, vh, preferred_element_type=F32)
                 + jnp.dot(qh, state.astype(BF16), preferred_element_type=F32) * xi)
            state_ref[hd] = state * g_chunk + lax.dot_general(
                kh * zeta_ref[hd], vh, (((0,), (0,)), ((), ())), preferred_element_type=F32)
            on = (o * _rms_scale(o) * ron_ref[:, vs]).astype(BF16)
            o_ref[rs, vs] = on * g_ref[rs, vs]


def _retention(q3, k3, v3, g3, ron):
    b, s, nq = q3.shape
    nv = v3.shape[2]
    tc = RET_ROWS
    blk = lambda w: pl.BlockSpec((None, tc, w), lambda bi, j: (bi, j, 0))
    return pl.pallas_call(
        _retention_kernel,
        grid=(b, s // tc),
        in_specs=[blk(nq), blk(nq), blk(nv), blk(nv), _const_spec(ron.shape)],
        out_specs=blk(nv),
        out_shape=jax.ShapeDtypeStruct((b, s, nv), BF16),
        scratch_shapes=[pltpu.VMEM((RET_HEADS, RET_QK, RET_V), F32),
                        pltpu.VMEM((RET_HEADS, RET_CHUNK, RET_CHUNK), BF16),
                        pltpu.VMEM((RET_HEADS, RET_CHUNK, RET_QK), BF16)],
        compiler_params=_params(("arbitrary", "arbitrary")),
        name="retention",
    )(q3, k3, v3, g3, ron)


def _rope_angles(seq, half):
    inv = ROPE_THETA ** (-jnp.arange(half, dtype=F32) / half)
    return jnp.arange(seq, dtype=F32)[:, None] * inv[None, :]


def _lane_bcast(g, width):
    return jnp.broadcast_to(g[:, None], (g.shape[0], width))


def kernel(x, norm_mix, norm_ffn, even_w_in, gm_v_norm, gm_w_s, gm_b_s, mla_q_a_norm, mla_w_uq,
           mla_kv_a_norm, mla_w_ukv, mla_q_norm, mla_k_norm, even_w_out, odd_w_in, ret_out_norm,
           odd_w_out, ffn_w_gate, ffn_w_up, ffn_w_down):
    b, s, d = x.shape
    tokens = b * s
    x2 = x.reshape(tokens, d)
    row = lambda t: t.reshape(1, -1)

    w_in = even_w_in[0]
    wuv = w_in[:, :2 * GM_WIDTH].astype(BF16)
    wct = w_in[:, 2 * GM_WIDTH:].T.astype(BF16)
    wuq3 = mla_w_uq[0].reshape(MLA_Q_RANK, MLA_HEADS, MLA_QK)
    wuqt = jnp.pad(wuq3, ((0, 0), (0, 0), (0, LANES - MLA_QK))).reshape(MLA_Q_RANK, -1).T.astype(BF16)
    wukv3 = mla_w_ukv[0].reshape(MLA_KV_RANK, MLA_HEADS, MLA_NOPE + MLA_V)
    wukvt = jnp.concatenate([wukv3[:, :, :MLA_NOPE].reshape(MLA_KV_RANK, -1),
                             wukv3[:, :, MLA_NOPE:].reshape(MLA_KV_RANK, -1)], axis=1).T.astype(BF16)
    er = EVEN_IN_SUB
    pad_head = lambda g: jnp.pad(g, (0, LANES - MLA_QK))
    ang = _rope_angles(s, MLA_ROPE // 2)
    a, qt, k, vt = _even_in(
        x2, row(norm_mix[0]), wuv, wct, row(gm_v_norm[0]), gm_w_s[0], gm_b_s[0].T,
        _lane_bcast(mla_q_a_norm[0], er), wuqt, _lane_bcast(mla_kv_a_norm[0], er), wukvt,
        _lane_bcast(pad_head(mla_q_norm[0]), er), _lane_bcast(pad_head(mla_k_norm[0]), er),
        jnp.cos(ang).T, jnp.sin(ang).T, s)
    later = [ffn_w_gate, ffn_w_up, ffn_w_down, even_w_out[0], odd_w_in[0], odd_w_out[0]]
    flat = lambda w: w.reshape(-1, w.shape[-1])
    o, *cast = _attention(qt, k.reshape(b, s, MLA_HEADS * LANES), vt, b, s, [flat(w) for w in later])
    wg, wu, wd, w_eo, w_oi, w_oo = [c.reshape(w.shape) for c, w in zip(cast, later)]
    x2 = _proj_ffn([a, o.reshape(tokens, -1)], w_eo, x2, row(norm_ffn[0]), wg, wu, wd, 0)

    ang = _rope_angles(s, RET_QK // 2)
    rq, rk, rv, rg = _odd_in(x2, row(norm_mix[1]), w_oi, jnp.cos(ang), jnp.sin(ang), s)
    nq = RET_HEADS * RET_QK
    nv = RET_HEADS * RET_V
    og = _retention(rq.reshape(b, s, nq), rk.reshape(b, s, nq), rv.reshape(b, s, nv),
                    rg.reshape(b, s, nv), row(ret_out_norm[0]))
    x2 = _proj_ffn([og.reshape(tokens, nv)], w_oo, x2, row(norm_ffn[1]), wg, wu, wd, 1)
    return x2.reshape(b, s, d)
```

```python
import functools
import math

import jax
import jax.numpy as jnp
from jax import lax
from jax.experimental import pallas as pl
from jax.experimental.pallas import tpu as pltpu

F32 = jnp.float32
BF16 = jnp.bfloat16

EPS = 1e-6
ROPE_THETA = 10000.0
CHUNK = 64

GM_GROUPS = 4
GM_GROUP_DIM = 128
GM_WIDTH = GM_GROUPS * GM_GROUP_DIM
GM_BLOCK = 128
MLA_HEADS = 8
MLA_Q_RANK = 384
MLA_KV_RANK = 256
MLA_NOPE = 64
MLA_ROPE = 32
MLA_V = 64
MLA_QK = MLA_NOPE + MLA_ROPE
RET_HEADS = 4
RET_QK = 256
RET_V = 512

LANES = 128
BF16_SUBLANES = 16
ONES_ROWS = BF16_SUBLANES
V7X_VMEM_LIMIT = 56 * 1024 * 1024

EVEN_IN_ROWS = 512
EVEN_IN_SUB = 256
ATTN_Q_ROWS = 512
ATTN_KV_ROWS = 512
ATTN_UNIT = 256
ATTN_GROUP = 2
FFN_ROWS = 512
ODD_IN_ROWS = 512
ODD_IN_COLS = 1024
RET_ROWS = 512
RET_CHUNK = 256
FF_CHUNKS = (768, 768, 768, 512)

NT_DIMS = (((1,), (1,)), ((), ()))


def _rms_scale(t):
    return lax.rsqrt(jnp.mean(t * t, axis=-1, keepdims=True) + EPS)


def _gelu_tanh(t):
    return 0.5 * t * (1.0 + jnp.tanh(math.sqrt(2.0 / math.pi) * (t + 0.044715 * (t * t * t))))


def _silu(t):
    return t * (1.0 / (1.0 + jnp.exp(-t)))


def _const_spec(shape):
    nd = len(shape)
    return pl.BlockSpec(shape, lambda *_: (0,) * nd, pipeline_mode=pl.Buffered(1))


def _params(sem):
    return pltpu.CompilerParams(dimension_semantics=sem, vmem_limit_bytes=V7X_VMEM_LIMIT)


def _even_in_kernel(x_ref, nmix_ref, wuv_ref, wct_ref, gvn_ref, ws_ref, bs_ref, gqa_ref, wuqt_ref,
                    gkva_ref, wukvt_ref, gq_ref, gk_ref, cos_ref, sin_ref,
                    a_ref, qt_ref, k_ref, vt_ref):
    subs = range(x_ref.shape[0] // EVEN_IN_SUB)
    proj = [_even_in_project(sub, x_ref, nmix_ref, wuv_ref, wct_ref) for sub in subs]
    heads = [_even_in_gate_and_expand(sub, proj[sub], gvn_ref, ws_ref, bs_ref, gqa_ref, wuqt_ref, gkva_ref,
                                      wukvt_ref, a_ref) for sub in subs]
    for sub in subs:
        _even_in_heads(sub, heads[sub], gq_ref, gk_ref, cos_ref, sin_ref, qt_ref, k_ref, vt_ref)


def _even_in_project(sub, x_ref, nmix_ref, wuv_ref, wct_ref):
    rows = EVEN_IN_SUB
    x = x_ref[sub * rows:(sub + 1) * rows, :]
    h = (x * _rms_scale(x) * nmix_ref[...]).astype(BF16)
    zuv = jnp.dot(h, wuv_ref[...], preferred_element_type=F32)
    zc = lax.dot_general(wct_ref[...], h, NT_DIMS, preferred_element_type=F32)
    return zuv, zc


def _even_in_gate_and_expand(sub, proj, gvn_ref, ws_ref, bs_ref, gqa_ref, wuqt_ref, gkva_ref, wukvt_ref, a_ref):
    rows = EVEN_IN_SUB
    zuv, zc = proj
    u = _gelu_tanh(zuv[:, :GM_WIDTH])
    v = _gelu_tanh(zuv[:, GM_WIDTH:])
    t_out = lax.broadcasted_iota(jnp.int32, (GM_BLOCK, GM_BLOCK), 0) // CHUNK
    t_in = lax.broadcasted_iota(jnp.int32, (GM_BLOCK, GM_BLOCK), 1) // CHUNK
    causal = t_in <= t_out
    nblk = rows // GM_BLOCK
    for g in range(GM_GROUPS):
        cs = slice(g * GM_GROUP_DIM, (g + 1) * GM_GROUP_DIM)
        vg = v[:, cs]
        vn = (vg * _rms_scale(vg) * gvn_ref[:, cs]).astype(BF16)
        wg = jnp.where(causal, ws_ref[g], 0.0).astype(BF16)
        vcat = jnp.concatenate([vn[j * GM_BLOCK:(j + 1) * GM_BLOCK] for j in range(nblk)], axis=1)
        s = jnp.dot(wg, vcat, preferred_element_type=F32) + bs_ref[:, g:g + 1]
        for j in range(nblk):
            rs = slice(j * GM_BLOCK, (j + 1) * GM_BLOCK)
            dst = slice(sub * rows + j * GM_BLOCK, sub * rows + (j + 1) * GM_BLOCK)
            a_ref[dst, cs] = (u[rs, cs] * s[:, j * GM_BLOCK:(j + 1) * GM_BLOCK]).astype(a_ref.dtype)

    o1 = MLA_Q_RANK
    o2 = o1 + MLA_KV_RANK
    cq, ckv, kpe = zc[:o1], zc[o1:o2], zc[o2:]

    def col_rms(t, n):
        return lax.rsqrt(jnp.sum(t * t, axis=0, keepdims=True) * (1.0 / n) + EPS)

    cqn = (cq * col_rms(cq, MLA_Q_RANK) * gqa_ref[...]).astype(BF16)
    ckvn = (ckv * col_rms(ckv, MLA_KV_RANK) * gkva_ref[...]).astype(BF16)
    qt = jnp.dot(wuqt_ref[...], cqn, preferred_element_type=F32)
    kvt = jnp.dot(wukvt_ref[...], ckvn, preferred_element_type=F32)
    return qt, kvt, kpe


def _even_in_heads(sub, expanded, gq_ref, gk_ref, cos_ref, sin_ref, qt_ref, k_ref, vt_ref):
    rows = EVEN_IN_SUB
    qt, kvt, kpe = expanded
    tok = slice(sub * rows, (sub + 1) * rows)
    cos, sin = cos_ref[:, tok], sin_ref[:, tok]
    half = MLA_ROPE // 2
    n1, n2 = MLA_NOPE, MLA_NOPE + half

    def rope(x1, x2):
        return x1 * cos - x2 * sin, x2 * cos + x1 * sin

    gq = gq_ref[...] * (MLA_QK ** -0.5 * math.log2(math.e))
    for hd in range(MLA_HEADS):
        t = qt[hd * LANES:(hd + 1) * LANES]
        r = lax.rsqrt(jnp.sum(t * t, axis=0, keepdims=True) * (1.0 / MLA_QK) + EPS)
        tn = t * gq
        r1, r2 = rope(tn[n1:n2], tn[n2:MLA_QK])
        base = hd * LANES
        qt_ref[base:base + n1, tok] = (tn[:n1] * r).astype(qt_ref.dtype)
        qt_ref[base + n1:base + n2, tok] = (r1 * r).astype(qt_ref.dtype)
        qt_ref[base + n2:base + MLA_QK, tok] = (r2 * r).astype(qt_ref.dtype)
        qt_ref[base + MLA_QK:base + LANES, tok] = jnp.zeros((LANES - MLA_QK, rows), qt_ref.dtype)

    gk = gk_ref[...]
    kpe_ss = jnp.sum(kpe * kpe, axis=0, keepdims=True)
    kr1, kr2 = rope(kpe[:half] * gk[n1:n2], kpe[half:] * gk[n2:MLA_QK])
    pad = jnp.zeros((LANES - MLA_QK, rows), F32)
    for hd in range(MLA_HEADS):
        t = kvt[hd * MLA_NOPE:(hd + 1) * MLA_NOPE]
        r = lax.rsqrt((jnp.sum(t * t, axis=0, keepdims=True) + kpe_ss) * (1.0 / MLA_QK) + EPS)
        kt = jnp.concatenate([t * gk[:n1] * r, kr1 * r, kr2 * r, pad], axis=0)
        k_ref[tok, hd * LANES:(hd + 1) * LANES] = kt.T.astype(k_ref.dtype)
    vt_ref[sub] = kvt[MLA_HEADS * MLA_NOPE:].astype(vt_ref.dtype)


def _even_in(x2, nmix, wuv, wct, gvn, ws, bs_t, gqa, wuqt, gkva, wukvt, gq, gk, cos_t, sin_t, seq):
    tokens, d = x2.shape
    rows = EVEN_IN_ROWS
    per_seq = seq // rows
    row_spec = lambda w: pl.BlockSpec((rows, w), lambda i: (i, 0))
    tab_spec = pl.BlockSpec((MLA_ROPE // 2, rows), lambda i: (0, i % per_seq))
    hw = MLA_HEADS * LANES
    vw = MLA_HEADS * MLA_V
    consts = [nmix, wuv, wct, gvn, ws, bs_t, gqa, wuqt, gkva, wukvt, gq, gk]
    return pl.pallas_call(
        _even_in_kernel,
        grid=(tokens // rows,),
        in_specs=[row_spec(d)] + [_const_spec(c.shape) for c in consts] + [tab_spec, tab_spec],
        out_specs=[row_spec(GM_WIDTH),
                   pl.BlockSpec((hw, rows), lambda i: (0, i)),
                   row_spec(hw),
                   pl.BlockSpec((rows // EVEN_IN_SUB, vw, EVEN_IN_SUB), lambda i: (i, 0, 0))],
        out_shape=[jax.ShapeDtypeStruct((tokens, GM_WIDTH), BF16),
                   jax.ShapeDtypeStruct((hw, tokens), BF16),
                   jax.ShapeDtypeStruct((tokens, hw), BF16),
                   jax.ShapeDtypeStruct((tokens // EVEN_IN_SUB, vw, EVEN_IN_SUB), BF16)],
        compiler_params=_params(("parallel",)),
        name="even_in",
    )(x2, *consts, cos_t, sin_t)


def _attn_kernel(*refs, n_groups, n_cast):
    qt_ref, k_ref, vt_ref = refs[:3]
    w_refs = refs[3:3 + n_cast]
    o_ref = refs[3 + n_cast]
    wb_refs = refs[4 + n_cast:4 + 2 * n_cast]
    sa_ref, sb_ref, ma_ref, mb_ref, m_ref, acc_ref = refs[4 + 2 * n_cast:]

    def cast_weights():
        for w_ref, wb_ref in zip(w_refs, wb_refs):
            wb_ref[...] = w_ref[...].astype(wb_ref.dtype)

    tq = ATTN_Q_ROWS
    tk = ATTN_KV_ROWS
    vrows = vt_ref.shape[2]
    ones = jnp.ones((ONES_ROWS, vrows), BF16)
    bufs = ((sa_ref, ma_ref), (sb_ref, mb_ref))
    units = [(hh, slice(u * ATTN_UNIT, (u + 1) * ATTN_UNIT)) for hh in range(2) for u in range(tq // ATTN_UNIT)]

    def run(group):
        first = group * ATTN_GROUP
        tasks = [(c, t) for c in range(first, first + ATTN_GROUP) for t in range(c + 1)]

        def key_rows(c, j, qs):
            return qs.stop if j == c else tk

        def scores(n, hh, qs):
            c, j = tasks[n]
            dst, bmax = bufs[n % 2]
            rows = key_rows(c, j, qs)
            q0 = (c - first) * tq
            s = jnp.dot(k_ref[j * tk:j * tk + rows, hh * LANES:(hh + 1) * LANES],
                        qt_ref[hh * LANES:(hh + 1) * LANES, q0 + qs.start:q0 + qs.stop],
                        preferred_element_type=F32)
            if j == c:
                ck = lax.broadcasted_iota(jnp.int32, s.shape, 0) // CHUNK
                cq = (lax.broadcasted_iota(jnp.int32, s.shape, 1) + qs.start) // CHUNK
                s = jnp.where(ck <= cq, s, -jnp.inf)
            dst[hh, :rows, qs] = s
            bmax[hh, :, qs] = jnp.max(s, axis=0, keepdims=True)

        def update(n, hh, qs):
            c, j = tasks[n]
            src, bmax = bufs[n % 2]
            rows = key_rows(c, j, qs)
            slot = c % 2
            m = m_ref[slot, hh, :, qs]
            m_new = jnp.maximum(m, bmax[hh, :, qs])
            alpha = jnp.exp2(m - m_new)
            p = jnp.exp2(src[hh, :rows, qs] - m_new).astype(BF16)
            m_ref[slot, hh, :, qs] = m_new
            pv = None
            for ch in range(rows // vrows):
                vt = vt_ref[j * (tk // vrows) + ch, hh * MLA_V:(hh + 1) * MLA_V, :]
                part = jnp.dot(jnp.concatenate([vt, ones], axis=0), p[ch * vrows:(ch + 1) * vrows],
                               preferred_element_type=F32)
                pv = part if pv is None else pv + part
            acc_ref[slot, hh, :, qs] = alpha * acc_ref[slot, hh, :, qs] + pv

        def start_block(c):
            m_ref[c % 2] = jnp.full(m_ref.shape[1:], -jnp.inf, F32)
            acc_ref[c % 2] = jnp.zeros(acc_ref.shape[1:], F32)

        def finish_block(c):
            acc = acc_ref[c % 2]
            ot = jnp.concatenate([acc[hh, :MLA_V] * (1.0 / acc[hh, MLA_V:MLA_V + 1]) for hh in range(2)],
                                 axis=0)
            o_ref[(c - first) * tq:(c - first + 1) * tq, :] = ot.T.astype(o_ref.dtype)

        start_block(first)
        for hh, qs in units:
            scores(0, hh, qs)
        for n, (c, j) in enumerate(tasks):
            has_next = n + 1 < len(tasks)
            if has_next and tasks[n + 1][1] == 0:
                start_block(tasks[n + 1][0])
            for hh, qs in units:
                if has_next:
                    scores(n + 1, hh, qs)
                update(n, hh, qs)
            if n == 0:
                cast_weights()
            if j == c:
                finish_block(c)

    for group in range(n_groups):
        pl.when(pl.program_id(2) == group)(functools.partial(run, group))


def _attention(qt, k3, vt, b, s, cast_weights):
    assert ATTN_Q_ROWS == ATTN_KV_ROWS and ATTN_UNIT % vt.shape[2] == 0
    tq = ATTN_Q_ROWS
    pairs = MLA_HEADS // 2
    vchunks, _, vrows = vt.shape
    rows = ATTN_GROUP * tq
    n_groups = s // rows
    steps = b * pairs * n_groups
    step = lambda bi, p, g: (bi * pairs + p) * n_groups + g

    def cast_spec(w):
        share = next(sh for sh in (1, 2, 4, 8) if w.shape[0] % (steps // sh * BF16_SUBLANES) == 0)
        return pl.BlockSpec((w.shape[0] // (steps // share), w.shape[1]),
                            lambda bi, p, g: (step(bi, p, g) // share, 0))

    cast_specs = [cast_spec(w) for w in cast_weights]
    return pl.pallas_call(
        functools.partial(_attn_kernel, n_groups=n_groups, n_cast=len(cast_weights)),
        grid=(b, pairs, n_groups),
        in_specs=[pl.BlockSpec((2 * LANES, rows), lambda bi, p, g: (p, bi * n_groups + g)),
                  pl.BlockSpec((None, s, 2 * LANES), lambda bi, p, g: (bi, 0, p)),
                  pl.BlockSpec((vchunks // b, 2 * MLA_V, vrows), lambda bi, p, g: (bi, p, 0))] + cast_specs,
        out_specs=[pl.BlockSpec((None, rows, LANES), lambda bi, p, g: (bi, g, p))] + cast_specs,
        out_shape=[jax.ShapeDtypeStruct((b, s, MLA_HEADS * MLA_V), BF16)]
        + [jax.ShapeDtypeStruct(w.shape, BF16) for w in cast_weights],
        scratch_shapes=[pltpu.VMEM((2, ATTN_KV_ROWS, tq), F32),
                        pltpu.VMEM((2, ATTN_KV_ROWS, tq), F32),
                        pltpu.VMEM((2, 1, tq), F32),
                        pltpu.VMEM((2, 1, tq), F32),
                        pltpu.VMEM((2, 2, 1, tq), F32),
                        pltpu.VMEM((2, 2, MLA_V + ONES_ROWS, tq), F32)],
        compiler_params=_params(("arbitrary", "arbitrary", "arbitrary")),
        name="attention",
    )(qt, k3, vt, *cast_weights)


def _proj_ffn_kernel(*refs, n_in):
    in_refs = refs[:n_in]
    wout_ref, x_ref, nffn_ref, wg_ref, wu_ref, wd_ref, out_ref = refs[n_in:]
    mix = None
    off = 0
    for r in in_refs:
        kdim = r.shape[1]
        part = jnp.dot(r[...], wout_ref[off:off + kdim, :], preferred_element_type=F32)
        mix = part if mix is None else mix + part
        off += kdim
    x1 = x_ref[...] + mix
    h = (x1 * _rms_scale(x1) * nffn_ref[...]).astype(BF16)
    acc = x1
    c0 = 0
    for c in FF_CHUNKS:
        g = jnp.dot(h, wg_ref[:, c0:c0 + c], preferred_element_type=F32)
        u = jnp.dot(h, wu_ref[:, c0:c0 + c], preferred_element_type=F32)
        act = (_silu(g) * u).astype(BF16)
        acc = acc + jnp.dot(act, wd_ref[c0:c0 + c, :], preferred_element_type=F32)
        c0 += c
    out_ref[...] = acc


def _proj_ffn(mix_ins, wout, x2, nffn, wg, wu, wd, layer):
    tokens, d = x2.shape
    rows = FFN_ROWS
    assert sum(FF_CHUNKS) == wg.shape[2]
    row_spec = lambda w: pl.BlockSpec((rows, w), lambda i: (i, 0))
    layer_spec = lambda w: pl.BlockSpec((None,) + w.shape[1:], lambda i: (layer, 0, 0),
                                        pipeline_mode=pl.Buffered(1))
    return pl.pallas_call(
        functools.partial(_proj_ffn_kernel, n_in=len(mix_ins)),
        grid=(tokens // rows,),
        in_specs=[row_spec(m.shape[1]) for m in mix_ins]
        + [_const_spec(wout.shape), row_spec(d), _const_spec(nffn.shape),
           layer_spec(wg), layer_spec(wu), layer_spec(wd)],
        out_specs=row_spec(d),
        out_shape=jax.ShapeDtypeStruct((tokens, d), F32),
        compiler_params=_params(("parallel",)),
        name="proj_ffn",
    )(*mix_ins, wout, x2, nffn, wg, wu, wd)


def _odd_in_kernel(x_ref, nmix_ref, win_ref, cos_ref, sin_ref, q_ref, k_ref, v_ref, g_ref):
    x = x_ref[...]
    h = (x * _rms_scale(x) * nmix_ref[...]).astype(BF16)
    cos, sin = cos_ref[...], sin_ref[...]
    nq = RET_HEADS * RET_QK
    nv = RET_HEADS * RET_V
    half = RET_QK // 2

    def project(col0, width):
        return jnp.dot(h, win_ref[:, col0:col0 + width], preferred_element_type=F32)

    def rope_store(dst, col0, scale):
        t_all = project(col0, nq)
        for hd in range(RET_HEADS):
            t1 = t_all[:, hd * RET_QK:hd * RET_QK + half]
            t2 = t_all[:, hd * RET_QK + half:(hd + 1) * RET_QK]
            dst[:, hd * RET_QK:hd * RET_QK + half] = ((t1 * cos - t2 * sin) * scale).astype(dst.dtype)
            dst[:, hd * RET_QK + half:(hd + 1) * RET_QK] = ((t2 * cos + t1 * sin) * scale).astype(dst.dtype)

    rope_store(q_ref, 0, 1.0)
    rope_store(k_ref, nq, RET_QK ** -0.5)
    for c0 in range(0, nv, ODD_IN_COLS):
        g_ref[:, c0:c0 + ODD_IN_COLS] = _silu(project(2 * nq + nv + c0, ODD_IN_COLS)).astype(g_ref.dtype)
    for c0 in range(0, nv, ODD_IN_COLS):
        v_ref[:, c0:c0 + ODD_IN_COLS] = project(2 * nq + c0, ODD_IN_COLS).astype(v_ref.dtype)


def _odd_in(x2, nmix, win, cos, sin, seq):
    tokens, d = x2.shape
    rows = ODD_IN_ROWS
    per_seq = seq // rows
    nq = RET_HEADS * RET_QK
    nv = RET_HEADS * RET_V
    row_spec = lambda w: pl.BlockSpec((rows, w), lambda i: (i, 0))
    tab_spec = pl.BlockSpec((rows, RET_QK // 2), lambda i: (i % per_seq, 0))
    return pl.pallas_call(
        _odd_in_kernel,
        grid=(tokens // rows,),
        in_specs=[row_spec(d), _const_spec(nmix.shape), _const_spec(win.shape), tab_spec, tab_spec],
        out_specs=[row_spec(nq), row_spec(nq), row_spec(nv), row_spec(nv)],
        out_shape=[jax.ShapeDtypeStruct((tokens, nq), BF16), jax.ShapeDtypeStruct((tokens, nq), BF16),
                   jax.ShapeDtypeStruct((tokens, nv), BF16), jax.ShapeDtypeStruct((tokens, nv), BF16)],
        compiler_params=_params(("parallel",)),
        name="odd_in",
    )(x2, nmix, win, cos, sin)


def _retention_kernel(q_ref, k_ref, v_ref, g_ref, ron_ref, o_ref, state_ref, decay_ref, zeta_ref, xi_ref):
    tc = RET_CHUNK
    j = pl.program_id(1)

    def log_gamma(hd):
        return jnp.log(jnp.full((1, 1), 1.0 - 2.0 ** (-5.0 - hd), F32))

    @pl.when(j == 0)
    def _():
        state_ref[...] = jnp.zeros_like(state_ref)
        diff = (lax.broadcasted_iota(jnp.int32, (tc, tc), 0)
                - lax.broadcasted_iota(jnp.int32, (tc, tc), 1))
        dpos = jnp.maximum(diff, 0).astype(F32)
        kpos = lax.broadcasted_iota(jnp.int32, (tc, RET_QK), 0).astype(F32)
        for hd in range(RET_HEADS):
            decay_ref[hd] = jnp.where(diff >= 0, jnp.exp(log_gamma(hd) * dpos), 0.0).astype(decay_ref.dtype)
            zeta_ref[hd] = jnp.exp(log_gamma(hd) * (tc - 1.0 - kpos)).astype(zeta_ref.dtype)
            xi_ref[hd] = jnp.exp(log_gamma(hd) * (kpos + 1.0)).astype(xi_ref.dtype)

    chunk_decay = [jnp.exp(log_gamma(hd) * float(tc)) for hd in range(RET_HEADS)]
    heads = range(RET_HEADS)
    qcol = [slice(hd * RET_QK, (hd + 1) * RET_QK) for hd in heads]
    vcol = [slice(hd * RET_V, (hd + 1) * RET_V) for hd in heads]
    def in_chunk(rs):
        return [lax.dot_general(q_ref[rs, qcol[hd]], k_ref[rs, qcol[hd]], NT_DIMS,
                                preferred_element_type=F32).astype(BF16) * decay_ref[hd] for hd in heads]

    def outputs(rs, a):
        return [jnp.dot(jnp.concatenate([a[hd], q_ref[rs, qcol[hd]] * xi_ref[hd]], axis=1),
                        jnp.concatenate([v_ref[rs, vcol[hd]], state_ref[hd].astype(BF16)], axis=0),
                        preferred_element_type=F32) for hd in heads]

    def advance_state(rs):
        for hd in heads:
            state_ref[hd] = state_ref[hd] * chunk_decay[hd] + lax.dot_general(
                k_ref[rs, qcol[hd]] * zeta_ref[hd], v_ref[rs, vcol[hd]], (((0,), (0,)), ((), ())),
                preferred_element_type=F32)

    def norm_gate_store(rs, o):
        for hd in heads:
            on = (o[hd] * _rms_scale(o[hd]) * ron_ref[:, vcol[hd]]).astype(BF16)
            o_ref[rs, vcol[hd]] = on * g_ref[rs, vcol[hd]]

    chunks = [slice(c * tc, (c + 1) * tc) for c in range(q_ref.shape[0] // tc)]
    a = in_chunk(chunks[0])
    for ci, rs in enumerate(chunks):
        o = outputs(rs, a)
        advance_state(rs)
        if ci + 1 < len(chunks):
            a = in_chunk(chunks[ci + 1])
        norm_gate_store(rs, o)


def _retention(q3, k3, v3, g3, ron):
    b, s, nq = q3.shape
    nv = v3.shape[2]
    tc = RET_ROWS
    blk = lambda w: pl.BlockSpec((None, tc, w), lambda bi, j: (bi, j, 0))
    return pl.pallas_call(
        _retention_kernel,
        grid=(b, s // tc),
        in_specs=[blk(nq), blk(nq), blk(nv), blk(nv), _const_spec(ron.shape)],
        out_specs=blk(nv),
        out_shape=jax.ShapeDtypeStruct((b, s, nv), BF16),
        scratch_shapes=[pltpu.VMEM((RET_HEADS, RET_QK, RET_V), F32),
                        pltpu.VMEM((RET_HEADS, RET_CHUNK, RET_CHUNK), BF16),
                        pltpu.VMEM((RET_HEADS, RET_CHUNK, RET_QK), BF16),
                        pltpu.VMEM((RET_HEADS, RET_CHUNK, RET_QK), BF16)],
        compiler_params=_params(("arbitrary", "arbitrary")),
        name="retention",
    )(q3, k3, v3, g3, ron)


def _rope_angles(seq, half):
    inv = ROPE_THETA ** (-jnp.arange(half, dtype=F32) / half)
    return jnp.arange(seq, dtype=F32)[:, None] * inv[None, :]


def _lane_bcast(g, width):
    return jnp.broadcast_to(g[:, None], (g.shape[0], width))


def kernel(x, norm_mix, norm_ffn, even_w_in, gm_v_norm, gm_w_s, gm_b_s, mla_q_a_norm, mla_w_uq,
           mla_kv_a_norm, mla_w_ukv, mla_q_norm, mla_k_norm, even_w_out, odd_w_in, ret_out_norm,
           odd_w_out, ffn_w_gate, ffn_w_up, ffn_w_down):
    b, s, d = x.shape
    tokens = b * s
    x2 = x.reshape(tokens, d)
    row = lambda t: t.reshape(1, -1)

    w_in = even_w_in[0]
    wuv = w_in[:, :2 * GM_WIDTH].astype(BF16)
    wct = w_in[:, 2 * GM_WIDTH:].T.astype(BF16)
    wuq3 = mla_w_uq[0].reshape(MLA_Q_RANK, MLA_HEADS, MLA_QK)
    wuqt = jnp.pad(wuq3, ((0, 0), (0, 0), (0, LANES - MLA_QK))).reshape(MLA_Q_RANK, -1).T.astype(BF16)
    wukv3 = mla_w_ukv[0].reshape(MLA_KV_RANK, MLA_HEADS, MLA_NOPE + MLA_V)
    wukvt = jnp.concatenate([wukv3[:, :, :MLA_NOPE].reshape(MLA_KV_RANK, -1),
                             wukv3[:, :, MLA_NOPE:].reshape(MLA_KV_RANK, -1)], axis=1).T.astype(BF16)
    er = EVEN_IN_SUB
    pad_head = lambda g: jnp.pad(g, (0, LANES - MLA_QK))
    ang = _rope_angles(s, MLA_ROPE // 2)
    a, qt, k, vt = _even_in(
        x2, row(norm_mix[0]), wuv, wct, row(gm_v_norm[0]), gm_w_s[0], gm_b_s[0].T,
        _lane_bcast(mla_q_a_norm[0], er), wuqt, _lane_bcast(mla_kv_a_norm[0], er), wukvt,
        _lane_bcast(pad_head(mla_q_norm[0]), er), _lane_bcast(pad_head(mla_k_norm[0]), er),
        jnp.cos(ang).T, jnp.sin(ang).T, s)
    later = [ffn_w_gate, ffn_w_up, ffn_w_down, even_w_out[0], odd_w_in[0], odd_w_out[0]]
    flat = lambda w: w.reshape(-1, w.shape[-1])
    o, *cast = _attention(qt, k.reshape(b, s, MLA_HEADS * LANES), vt, b, s, [flat(w) for w in later])
    wg, wu, wd, w_eo, w_oi, w_oo = [c.reshape(w.shape) for c, w in zip(cast, later)]
    x2 = _proj_ffn([a, o.reshape(tokens, -1)], w_eo, x2, row(norm_ffn[0]), wg, wu, wd, 0)

    ang = _rope_angles(s, RET_QK // 2)
    rq, rk, rv, rg = _odd_in(x2, row(norm_mix[1]), w_oi, jnp.cos(ang), jnp.sin(ang), s)
    nq = RET_HEADS * RET_QK
    nv = RET_HEADS * RET_V
    og = _retention(rq.reshape(b, s, nq), rk.reshape(b, s, nq), rv.reshape(b, s, nv),
                    rg.reshape(b, s, nv), row(ret_out_norm[0]))
    x2 = _proj_ffn([og.reshape(tokens, nv)], w_oo, x2, row(norm_ffn[1]), wg, wu, wd, 1)
    return x2.reshape(b, s, d)
```

```python
import functools
import math

import jax
import jax.numpy as jnp
import numpy as np
from jax import lax
from jax.experimental import pallas as pl
from jax.experimental.pallas import tpu as pltpu

F32 = jnp.float32
BF16 = jnp.bfloat16

EPS = 1e-6
ROPE_THETA = 10000.0
CHUNK = 64

GM_GROUPS = 4
GM_GROUP_DIM = 128
GM_WIDTH = GM_GROUPS * GM_GROUP_DIM
GM_BLOCK = 128
MLA_HEADS = 8
MLA_Q_RANK = 384
MLA_KV_RANK = 256
MLA_NOPE = 64
MLA_ROPE = 32
MLA_V = 64
MLA_QK = MLA_NOPE + MLA_ROPE
RET_HEADS = 4
RET_QK = 256
RET_V = 512

LANES = 128
BF16_SUBLANES = 16
ONES_ROWS = BF16_SUBLANES
V7X_VMEM_LIMIT = 56 * 1024 * 1024

EVEN_IN_ROWS = 512
EVEN_IN_SUB = 256
ATTN_Q_ROWS = 512
ATTN_KV_ROWS = 512
ATTN_UNIT = 256
ATTN_GROUP = 2
FFN_ROWS = 512
ODD_IN_ROWS = 512
ODD_IN_COLS = 1024
RET_ROWS = 512
RET_CHUNK = 256
FF_CHUNKS = (768, 768, 768, 512)

NT_DIMS = (((1,), (1,)), ((), ()))


def _rms_scale(t):
    return lax.rsqrt(jnp.mean(t * t, axis=-1, keepdims=True) + EPS)


def _gelu_tanh(t):
    return 0.5 * t * (1.0 + jnp.tanh(math.sqrt(2.0 / math.pi) * (t + 0.044715 * (t * t * t))))


def _silu(t):
    return t * (1.0 / (1.0 + jnp.exp(-t)))


def _const_spec(shape):
    nd = len(shape)
    return pl.BlockSpec(shape, lambda *_: (0,) * nd, pipeline_mode=pl.Buffered(1))


def _params(sem):
    return pltpu.CompilerParams(dimension_semantics=sem, vmem_limit_bytes=V7X_VMEM_LIMIT)


def _even_in_kernel(x_ref, nmix_ref, wuv_ref, wct_ref, gvn_ref, ws_ref, bs_ref, gqa_ref, wuqt_ref,
                    gkva_ref, wukvt_ref, gq_ref, gk_ref, cos_ref, sin_ref,
                    a_ref, qt_ref, k_ref, vt_ref):
    subs = range(x_ref.shape[0] // EVEN_IN_SUB)
    proj = [_even_in_project(sub, x_ref, nmix_ref, wuv_ref, wct_ref) for sub in subs]
    heads = [_even_in_gate_and_expand(sub, proj[sub], gvn_ref, ws_ref, bs_ref, gqa_ref, wuqt_ref, gkva_ref,
                                      wukvt_ref, a_ref) for sub in subs]
    for sub in subs:
        _even_in_heads(sub, heads[sub], gq_ref, gk_ref, cos_ref, sin_ref, qt_ref, k_ref, vt_ref)


def _even_in_project(sub, x_ref, nmix_ref, wuv_ref, wct_ref):
    rows = EVEN_IN_SUB
    x = x_ref[sub * rows:(sub + 1) * rows, :]
    h = (x * _rms_scale(x) * nmix_ref[...]).astype(BF16)
    zuv = jnp.dot(h, wuv_ref[...], preferred_element_type=F32)
    zc = lax.dot_general(wct_ref[...], h, NT_DIMS, preferred_element_type=F32)
    return zuv, zc


def _even_in_gate_and_expand(sub, proj, gvn_ref, ws_ref, bs_ref, gqa_ref, wuqt_ref, gkva_ref, wukvt_ref, a_ref):
    rows = EVEN_IN_SUB
    zuv, zc = proj
    u = _gelu_tanh(zuv[:, :GM_WIDTH])
    v = _gelu_tanh(zuv[:, GM_WIDTH:])
    t_out = lax.broadcasted_iota(jnp.int32, (GM_BLOCK, GM_BLOCK), 0) // CHUNK
    t_in = lax.broadcasted_iota(jnp.int32, (GM_BLOCK, GM_BLOCK), 1) // CHUNK
    causal = t_in <= t_out
    nblk = rows // GM_BLOCK
    for g in range(GM_GROUPS):
        cs = slice(g * GM_GROUP_DIM, (g + 1) * GM_GROUP_DIM)
        vg = v[:, cs]
        vn = (vg * _rms_scale(vg) * gvn_ref[:, cs]).astype(BF16)
        wg = jnp.where(causal, ws_ref[g], 0.0).astype(BF16)
        vcat = jnp.concatenate([vn[j * GM_BLOCK:(j + 1) * GM_BLOCK] for j in range(nblk)], axis=1)
        s = jnp.dot(wg, vcat, preferred_element_type=F32) + bs_ref[:, g:g + 1]
        for j in range(nblk):
            rs = slice(j * GM_BLOCK, (j + 1) * GM_BLOCK)
            dst = slice(sub * rows + j * GM_BLOCK, sub * rows + (j + 1) * GM_BLOCK)
            a_ref[dst, cs] = (u[rs, cs] * s[:, j * GM_BLOCK:(j + 1) * GM_BLOCK]).astype(a_ref.dtype)

    o1 = MLA_Q_RANK
    o2 = o1 + MLA_KV_RANK
    cq, ckv, kpe = zc[:o1], zc[o1:o2], zc[o2:]

    def col_rms(t, n):
        return lax.rsqrt(jnp.sum(t * t, axis=0, keepdims=True) * (1.0 / n) + EPS)

    cqn = (cq * col_rms(cq, MLA_Q_RANK) * gqa_ref[...]).astype(BF16)
    ckvn = (ckv * col_rms(ckv, MLA_KV_RANK) * gkva_ref[...]).astype(BF16)
    qt = jnp.dot(wuqt_ref[...], cqn, preferred_element_type=F32)
    kvt = jnp.dot(wukvt_ref[...], ckvn, preferred_element_type=F32)
    return qt, kvt, kpe


def _even_in_heads(sub, expanded, gq_ref, gk_ref, cos_ref, sin_ref, qt_ref, k_ref, vt_ref):
    rows = EVEN_IN_SUB
    qt, kvt, kpe = expanded
    tok = slice(sub * rows, (sub + 1) * rows)
    cos, sin = cos_ref[:, tok], sin_ref[:, tok]
    half = MLA_ROPE // 2
    n1, n2 = MLA_NOPE, MLA_NOPE + half

    def rope(x1, x2):
        return x1 * cos - x2 * sin, x2 * cos + x1 * sin

    gq = gq_ref[...] * (MLA_QK ** -0.5 * math.log2(math.e))
    for hd in range(MLA_HEADS):
        t = qt[hd * LANES:(hd + 1) * LANES]
        r = lax.rsqrt(jnp.sum(t * t, axis=0, keepdims=True) * (1.0 / MLA_QK) + EPS)
        tn = t * gq
        r1, r2 = rope(tn[n1:n2], tn[n2:MLA_QK])
        base = hd * LANES
        qt_ref[base:base + n1, tok] = (tn[:n1] * r).astype(qt_ref.dtype)
        qt_ref[base + n1:base + n2, tok] = (r1 * r).astype(qt_ref.dtype)
        qt_ref[base + n2:base + MLA_QK, tok] = (r2 * r).astype(qt_ref.dtype)
        qt_ref[base + MLA_QK:base + LANES, tok] = jnp.zeros((LANES - MLA_QK, rows), qt_ref.dtype)

    gk = gk_ref[...]
    kpe_ss = jnp.sum(kpe * kpe, axis=0, keepdims=True)
    kr1, kr2 = rope(kpe[:half] * gk[n1:n2], kpe[half:] * gk[n2:MLA_QK])
    pad = jnp.zeros((LANES - MLA_QK, rows), F32)
    for hd in range(MLA_HEADS):
        t = kvt[hd * MLA_NOPE:(hd + 1) * MLA_NOPE]
        r = lax.rsqrt((jnp.sum(t * t, axis=0, keepdims=True) + kpe_ss) * (1.0 / MLA_QK) + EPS)
        kt = jnp.concatenate([t * gk[:n1] * r, kr1 * r, kr2 * r, pad], axis=0)
        k_ref[tok, hd * LANES:(hd + 1) * LANES] = kt.T.astype(k_ref.dtype)
    vt_ref[sub] = kvt[MLA_HEADS * MLA_NOPE:].astype(vt_ref.dtype)


def _even_in(x2, nmix, wuv, wct, gvn, ws, bs_t, gqa, wuqt, gkva, wukvt, gq, gk, cos_t, sin_t, seq):
    tokens, d = x2.shape
    rows = EVEN_IN_ROWS
    per_seq = seq // rows
    row_spec = lambda w: pl.BlockSpec((rows, w), lambda i: (i, 0))
    tab_spec = pl.BlockSpec((MLA_ROPE // 2, rows), lambda i: (0, i % per_seq))
    hw = MLA_HEADS * LANES
    vw = MLA_HEADS * MLA_V
    consts = [nmix, wuv, wct, gvn, ws, bs_t, gqa, wuqt, gkva, wukvt, gq, gk]
    return pl.pallas_call(
        _even_in_kernel,
        grid=(tokens // rows,),
        in_specs=[row_spec(d)] + [_const_spec(c.shape) for c in consts] + [tab_spec, tab_spec],
        out_specs=[row_spec(GM_WIDTH),
                   pl.BlockSpec((hw, rows), lambda i: (0, i)),
                   row_spec(hw),
                   pl.BlockSpec((rows // EVEN_IN_SUB, vw, EVEN_IN_SUB), lambda i: (i, 0, 0))],
        out_shape=[jax.ShapeDtypeStruct((tokens, GM_WIDTH), BF16),
                   jax.ShapeDtypeStruct((hw, tokens), BF16),
                   jax.ShapeDtypeStruct((tokens, hw), BF16),
                   jax.ShapeDtypeStruct((tokens // EVEN_IN_SUB, vw, EVEN_IN_SUB), BF16)],
        compiler_params=_params(("parallel",)),
        name="even_in",
    )(x2, *consts, cos_t, sin_t)


def _attn_kernel(*refs, n_groups, n_cast):
    qt_ref, k_ref, vt_ref = refs[:3]
    w_refs = refs[3:3 + n_cast]
    o_ref = refs[3 + n_cast]
    wb_refs = refs[4 + n_cast:4 + 2 * n_cast]
    sa_ref, sb_ref, ma_ref, mb_ref, m_ref, acc_ref = refs[4 + 2 * n_cast:]

    for w_ref, wb_ref in zip(w_refs, wb_refs):
        wb_ref[...] = w_ref[...].astype(wb_ref.dtype)

    tq = ATTN_Q_ROWS
    tk = ATTN_KV_ROWS
    vrows = vt_ref.shape[2]
    ones = jnp.ones((ONES_ROWS, vrows), BF16)
    bufs = ((sa_ref, ma_ref), (sb_ref, mb_ref))
    units = [(hh, slice(u * ATTN_UNIT, (u + 1) * ATTN_UNIT)) for hh in range(2) for u in range(tq // ATTN_UNIT)]

    def run(group):
        first = group * ATTN_GROUP
        tasks = [(c, t) for c in range(first, first + ATTN_GROUP) for t in range(c + 1)]

        def key_rows(c, j, qs):
            return qs.stop if j == c else tk

        def scores(n, hh, qs):
            c, j = tasks[n]
            dst, bmax = bufs[n % 2]
            rows = key_rows(c, j, qs)
            q0 = (c - first) * tq
            s = jnp.dot(k_ref[j * tk:j * tk + rows, hh * LANES:(hh + 1) * LANES],
                        qt_ref[hh * LANES:(hh + 1) * LANES, q0 + qs.start:q0 + qs.stop],
                        preferred_element_type=F32)
            if j == c:
                ck = lax.broadcasted_iota(jnp.int32, s.shape, 0) // CHUNK
                cq = (lax.broadcasted_iota(jnp.int32, s.shape, 1) + qs.start) // CHUNK
                s = jnp.where(ck <= cq, s, -jnp.inf)
            dst[hh, :rows, qs] = s
            bmax[hh, :, qs] = jnp.max(s, axis=0, keepdims=True)

        def update(n, hh, qs):
            c, j = tasks[n]
            src, bmax = bufs[n % 2]
            rows = key_rows(c, j, qs)
            slot = c % 2
            m = m_ref[slot, hh, :, qs]
            m_new = jnp.maximum(m, bmax[hh, :, qs])
            alpha = jnp.exp2(m - m_new)
            p = jnp.exp2(src[hh, :rows, qs] - m_new).astype(BF16)
            m_ref[slot, hh, :, qs] = m_new
            pv = None
            for ch in range(rows // vrows):
                vt = vt_ref[j * (tk // vrows) + ch, hh * MLA_V:(hh + 1) * MLA_V, :]
                part = jnp.dot(jnp.concatenate([vt, ones], axis=0), p[ch * vrows:(ch + 1) * vrows],
                               preferred_element_type=F32)
                pv = part if pv is None else pv + part
            acc_ref[slot, hh, :, qs] = alpha * acc_ref[slot, hh, :, qs] + pv

        def start_block(c):
            m_ref[c % 2] = jnp.full(m_ref.shape[1:], -jnp.inf, F32)
            acc_ref[c % 2] = jnp.zeros(acc_ref.shape[1:], F32)

        def finish_block(c):
            acc = acc_ref[c % 2]
            ot = jnp.concatenate([acc[hh, :MLA_V] * (1.0 / acc[hh, MLA_V:MLA_V + 1]) for hh in range(2)],
                                 axis=0)
            o_ref[(c - first) * tq:(c - first + 1) * tq, :] = ot.T.astype(o_ref.dtype)

        start_block(first)
        for hh, qs in units:
            scores(0, hh, qs)
        for n, (c, j) in enumerate(tasks):
            has_next = n + 1 < len(tasks)
            if has_next and tasks[n + 1][1] == 0:
                start_block(tasks[n + 1][0])
            for hh, qs in units:
                if has_next:
                    scores(n + 1, hh, qs)
                update(n, hh, qs)
            if j == c:
                finish_block(c)

    for group in range(n_groups):
        pl.when(pl.program_id(2) == group)(functools.partial(run, group))


def _attention(qt, k3, vt, b, s, cast_weights):
    assert ATTN_Q_ROWS == ATTN_KV_ROWS and ATTN_UNIT % vt.shape[2] == 0
    tq = ATTN_Q_ROWS
    pairs = MLA_HEADS // 2
    vchunks, _, vrows = vt.shape
    rows = ATTN_GROUP * tq
    n_groups = s // rows
    steps = b * pairs * n_groups
    step = lambda bi, p, g: (bi * pairs + p) * n_groups + g

    def cast_spec(w):
        share = next(sh for sh in (1, 2, 4, 8) if w.shape[0] % (steps // sh * BF16_SUBLANES) == 0)
        return pl.BlockSpec((w.shape[0] // (steps // share), w.shape[1]),
                            lambda bi, p, g: (step(bi, p, g) // share, 0))

    cast_specs = [cast_spec(w) for w in cast_weights]
    return pl.pallas_call(
        functools.partial(_attn_kernel, n_groups=n_groups, n_cast=len(cast_weights)),
        grid=(b, pairs, n_groups),
        in_specs=[pl.BlockSpec((2 * LANES, rows), lambda bi, p, g: (p, bi * n_groups + g)),
                  pl.BlockSpec((None, s, 2 * LANES), lambda bi, p, g: (bi, 0, p)),
                  pl.BlockSpec((vchunks // b, 2 * MLA_V, vrows), lambda bi, p, g: (bi, p, 0))] + cast_specs,
        out_specs=[pl.BlockSpec((None, rows, LANES), lambda bi, p, g: (bi, g, p))] + cast_specs,
        out_shape=[jax.ShapeDtypeStruct((b, s, MLA_HEADS * MLA_V), BF16)]
        + [jax.ShapeDtypeStruct(w.shape, BF16) for w in cast_weights],
        scratch_shapes=[pltpu.VMEM((2, ATTN_KV_ROWS, tq), F32),
                        pltpu.VMEM((2, ATTN_KV_ROWS, tq), F32),
                        pltpu.VMEM((2, 1, tq), F32),
                        pltpu.VMEM((2, 1, tq), F32),
                        pltpu.VMEM((2, 2, 1, tq), F32),
                        pltpu.VMEM((2, 2, MLA_V + ONES_ROWS, tq), F32)],
        compiler_params=_params(("arbitrary", "arbitrary", "arbitrary")),
        name="attention",
    )(qt, k3, vt, *cast_weights)


def _proj_ffn_kernel(*refs, n_in):
    in_refs = refs[:n_in]
    wout_ref, x_ref, nffn_ref, wg_ref, wu_ref, wd_ref, out_ref = refs[n_in:]
    mix = None
    off = 0
    for r in in_refs:
        kdim = r.shape[1]
        part = jnp.dot(r[...], wout_ref[off:off + kdim, :], preferred_element_type=F32)
        mix = part if mix is None else mix + part
        off += kdim
    x1 = x_ref[...] + mix
    h = (x1 * _rms_scale(x1) * nffn_ref[...]).astype(BF16)
    acc = x1
    c0 = 0
    for c in FF_CHUNKS:
        g = jnp.dot(h, wg_ref[:, c0:c0 + c], preferred_element_type=F32)
        u = jnp.dot(h, wu_ref[:, c0:c0 + c], preferred_element_type=F32)
        act = (_silu(g) * u).astype(BF16)
        acc = acc + jnp.dot(act, wd_ref[c0:c0 + c, :], preferred_element_type=F32)
        c0 += c
    out_ref[...] = acc


def _proj_ffn(mix_ins, wout, x2, nffn, wg, wu, wd, layer):
    tokens, d = x2.shape
    rows = FFN_ROWS
    assert sum(FF_CHUNKS) == wg.shape[2]
    row_spec = lambda w: pl.BlockSpec((rows, w), lambda i: (i, 0))
    layer_spec = lambda w: pl.BlockSpec((None,) + w.shape[1:], lambda i: (layer, 0, 0),
                                        pipeline_mode=pl.Buffered(1))
    return pl.pallas_call(
        functools.partial(_proj_ffn_kernel, n_in=len(mix_ins)),
        grid=(tokens // rows,),
        in_specs=[row_spec(m.shape[1]) for m in mix_ins]
        + [_const_spec(wout.shape), row_spec(d), _const_spec(nffn.shape),
           layer_spec(wg), layer_spec(wu), layer_spec(wd)],
        out_specs=row_spec(d),
        out_shape=jax.ShapeDtypeStruct((tokens, d), F32),
        compiler_params=_params(("parallel",)),
        name="proj_ffn",
    )(*mix_ins, wout, x2, nffn, wg, wu, wd)


def _odd_in_kernel(x_ref, nmix_ref, win_ref, cos_ref, sin_ref, q_ref, k_ref, v_ref, g_ref):
    x = x_ref[...]
    h = (x * _rms_scale(x) * nmix_ref[...]).astype(BF16)
    cos, sin = cos_ref[...], sin_ref[...]
    nq = RET_HEADS * RET_QK
    nv = RET_HEADS * RET_V
    half = RET_QK // 2

    def project(col0, width):
        return jnp.dot(h, win_ref[:, col0:col0 + width], preferred_element_type=F32)

    def rope_store(dst, col0, scale):
        t_all = project(col0, nq)
        for hd in range(RET_HEADS):
            t1 = t_all[:, hd * RET_QK:hd * RET_QK + half]
            t2 = t_all[:, hd * RET_QK + half:(hd + 1) * RET_QK]
            dst[:, hd * RET_QK:hd * RET_QK + half] = ((t1 * cos - t2 * sin) * scale).astype(dst.dtype)
            dst[:, hd * RET_QK + half:(hd + 1) * RET_QK] = ((t2 * cos + t1 * sin) * scale).astype(dst.dtype)

    rope_store(q_ref, 0, 1.0)
    rope_store(k_ref, nq, RET_QK ** -0.5)
    for c0 in range(0, nv, ODD_IN_COLS):
        g_ref[:, c0:c0 + ODD_IN_COLS] = _silu(project(2 * nq + nv + c0, ODD_IN_COLS)).astype(g_ref.dtype)
    for c0 in range(0, nv, ODD_IN_COLS):
        v_ref[:, c0:c0 + ODD_IN_COLS] = project(2 * nq + c0, ODD_IN_COLS).astype(v_ref.dtype)


def _odd_in(x2, nmix, win, cos, sin, seq):
    tokens, d = x2.shape
    rows = ODD_IN_ROWS
    per_seq = seq // rows
    nq = RET_HEADS * RET_QK
    nv = RET_HEADS * RET_V
    row_spec = lambda w: pl.BlockSpec((rows, w), lambda i: (i, 0))
    tab_spec = pl.BlockSpec((rows, RET_QK // 2), lambda i: (i % per_seq, 0))
    return pl.pallas_call(
        _odd_in_kernel,
        grid=(tokens // rows,),
        in_specs=[row_spec(d), _const_spec(nmix.shape), _const_spec(win.shape), tab_spec, tab_spec],
        out_specs=[row_spec(nq), row_spec(nq), row_spec(nv), row_spec(nv)],
        out_shape=[jax.ShapeDtypeStruct((tokens, nq), BF16), jax.ShapeDtypeStruct((tokens, nq), BF16),
                   jax.ShapeDtypeStruct((tokens, nv), BF16), jax.ShapeDtypeStruct((tokens, nv), BF16)],
        compiler_params=_params(("parallel",)),
        name="odd_in",
    )(x2, nmix, win, cos, sin)


def _retention_kernel(q_ref, k_ref, v_ref, g_ref, ron_ref, o_ref, state_ref, decay_ref, zeta_ref, xi_ref):
    tc = RET_CHUNK
    j = pl.program_id(1)

    def log_gamma(hd):
        return jnp.log(jnp.full((1, 1), 1.0 - 2.0 ** (-5.0 - hd), F32))

    @pl.when(j == 0)
    def _():
        state_ref[...] = jnp.zeros_like(state_ref)
        diff = (lax.broadcasted_iota(jnp.int32, (tc, tc), 0)
                - lax.broadcasted_iota(jnp.int32, (tc, tc), 1))
        dpos = jnp.maximum(diff, 0).astype(F32)
        kpos = lax.broadcasted_iota(jnp.int32, (tc, RET_QK), 0).astype(F32)
        for hd in range(RET_HEADS):
            decay_ref[hd] = jnp.where(diff >= 0, jnp.exp(log_gamma(hd) * dpos), 0.0).astype(decay_ref.dtype)
            zeta_ref[hd] = jnp.exp(log_gamma(hd) * (tc - 1.0 - kpos)).astype(zeta_ref.dtype)
            xi_ref[hd] = jnp.exp(log_gamma(hd) * (kpos + 1.0)).astype(xi_ref.dtype)

    chunk_decay = [jnp.exp(log_gamma(hd) * float(tc)) for hd in range(RET_HEADS)]
    heads = range(RET_HEADS)
    qcol = [slice(hd * RET_QK, (hd + 1) * RET_QK) for hd in heads]
    vcol = [slice(hd * RET_V, (hd + 1) * RET_V) for hd in heads]
    def in_chunk(rs):
        return [lax.dot_general(q_ref[rs, qcol[hd]], k_ref[rs, qcol[hd]], NT_DIMS,
                                preferred_element_type=F32).astype(BF16) * decay_ref[hd] for hd in heads]

    def outputs(rs, a):
        return [jnp.dot(jnp.concatenate([a[hd], q_ref[rs, qcol[hd]] * xi_ref[hd]], axis=1),
                        jnp.concatenate([v_ref[rs, vcol[hd]], state_ref[hd].astype(BF16)], axis=0),
                        preferred_element_type=F32) for hd in heads]

    def advance_state(rs):
        for hd in heads:
            state_ref[hd] = state_ref[hd] * chunk_decay[hd] + lax.dot_general(
                k_ref[rs, qcol[hd]] * zeta_ref[hd], v_ref[rs, vcol[hd]], (((0,), (0,)), ((), ())),
                preferred_element_type=F32)

    def norm_gate_store(rs, o):
        for hd in heads:
            on = (o[hd] * _rms_scale(o[hd]) * ron_ref[:, vcol[hd]]).astype(BF16)
            o_ref[rs, vcol[hd]] = on * g_ref[rs, vcol[hd]]

    chunks = [slice(c * tc, (c + 1) * tc) for c in range(q_ref.shape[0] // tc)]
    a = in_chunk(chunks[0])
    for ci, rs in enumerate(chunks):
        o = outputs(rs, a)
        advance_state(rs)
        if ci + 1 < len(chunks):
            a = in_chunk(chunks[ci + 1])
        norm_gate_store(rs, o)


def _retention(q3, k3, v3, g3, ron):
    b, s, nq = q3.shape
    nv = v3.shape[2]
    tc = RET_ROWS
    blk = lambda w: pl.BlockSpec((None, tc, w), lambda bi, j: (bi, j, 0))
    return pl.pallas_call(
        _retention_kernel,
        grid=(b, s // tc),
        in_specs=[blk(nq), blk(nq), blk(nv), blk(nv), _const_spec(ron.shape)],
        out_specs=blk(nv),
        out_shape=jax.ShapeDtypeStruct((b, s, nv), BF16),
        scratch_shapes=[pltpu.VMEM((RET_HEADS, RET_QK, RET_V), F32),
                        pltpu.VMEM((RET_HEADS, RET_CHUNK, RET_CHUNK), BF16),
                        pltpu.VMEM((RET_HEADS, RET_CHUNK, RET_QK), BF16),
                        pltpu.VMEM((RET_HEADS, RET_CHUNK, RET_QK), BF16)],
        compiler_params=_params(("arbitrary", "arbitrary")),
        name="retention",
    )(q3, k3, v3, g3, ron)


def _rope_tables(seq, half):
    inv = ROPE_THETA ** (-np.arange(half, dtype=np.float64) / half)
    ang = np.arange(seq, dtype=np.float64)[:, None] * inv[None, :]
    return np.cos(ang).astype(np.float32), np.sin(ang).astype(np.float32)


def _lane_bcast(g, width):
    return jnp.broadcast_to(g[:, None], (g.shape[0], width))


def kernel(x, norm_mix, norm_ffn, even_w_in, gm_v_norm, gm_w_s, gm_b_s, mla_q_a_norm, mla_w_uq,
           mla_kv_a_norm, mla_w_ukv, mla_q_norm, mla_k_norm, even_w_out, odd_w_in, ret_out_norm,
           odd_w_out, ffn_w_gate, ffn_w_up, ffn_w_down):
    b, s, d = x.shape
    tokens = b * s
    x2 = x.reshape(tokens, d)
    row = lambda t: t.reshape(1, -1)

    w_in = even_w_in[0]
    wuv = w_in[:, :2 * GM_WIDTH].astype(BF16)
    wct = w_in[:, 2 * GM_WIDTH:].T.astype(BF16)
    wuq3 = mla_w_uq[0].reshape(MLA_Q_RANK, MLA_HEADS, MLA_QK)
    wuqt = jnp.pad(wuq3, ((0, 0), (0, 0), (0, LANES - MLA_QK))).reshape(MLA_Q_RANK, -1).T.astype(BF16)
    wukv3 = mla_w_ukv[0].reshape(MLA_KV_RANK, MLA_HEADS, MLA_NOPE + MLA_V)
    wukvt = jnp.concatenate([wukv3[:, :, :MLA_NOPE].reshape(MLA_KV_RANK, -1),
                             wukv3[:, :, MLA_NOPE:].reshape(MLA_KV_RANK, -1)], axis=1).T.astype(BF16)
    er = EVEN_IN_SUB
    pad_head = lambda g: jnp.pad(g, (0, LANES - MLA_QK))
    cos0, sin0 = _rope_tables(s, MLA_ROPE // 2)
    a, qt, k, vt = _even_in(
        x2, row(norm_mix[0]), wuv, wct, row(gm_v_norm[0]), gm_w_s[0], gm_b_s[0].T,
        _lane_bcast(mla_q_a_norm[0], er), wuqt, _lane_bcast(mla_kv_a_norm[0], er), wukvt,
        _lane_bcast(pad_head(mla_q_norm[0]), er), _lane_bcast(pad_head(mla_k_norm[0]), er),
        jnp.asarray(cos0.T), jnp.asarray(sin0.T), s)
    later = [ffn_w_gate, ffn_w_up, ffn_w_down, even_w_out[0], odd_w_in[0], odd_w_out[0]]
    flat = lambda w: w.reshape(-1, w.shape[-1])
    o, *cast = _attention(qt, k.reshape(b, s, MLA_HEADS * LANES), vt, b, s, [flat(w) for w in later])
    wg, wu, wd, w_eo, w_oi, w_oo = [c.reshape(w.shape) for c, w in zip(cast, later)]
    x2 = _proj_ffn([a, o.reshape(tokens, -1)], w_eo, x2, row(norm_ffn[0]), wg, wu, wd, 0)

    cos1, sin1 = _rope_tables(s, RET_QK // 2)
    rq, rk, rv, rg = _odd_in(x2, row(norm_mix[1]), w_oi, jnp.asarray(cos1), jnp.asarray(sin1), s)
    nq = RET_HEADS * RET_QK
    nv = RET_HEADS * RET_V
    og = _retention(rq.reshape(b, s, nq), rk.reshape(b, s, nq), rv.reshape(b, s, nv),
                    rg.reshape(b, s, nv), row(ret_out_norm[0]))
    x2 = _proj_ffn([og.reshape(tokens, nv)], w_oo, x2, row(norm_ffn[1]), wg, wu, wd, 1)
    return x2.reshape(b, s, d)
```

```python
import functools
import math

import jax
import jax.numpy as jnp
import numpy as np
from jax import lax
from jax.experimental import pallas as pl
from jax.experimental.pallas import tpu as pltpu

F32 = jnp.float32
BF16 = jnp.bfloat16

EPS = 1e-6
ROPE_THETA = 10000.0
CHUNK = 64

GM_GROUPS = 4
GM_GROUP_DIM = 128
GM_WIDTH = GM_GROUPS * GM_GROUP_DIM
GM_BLOCK = 128
MLA_HEADS = 8
MLA_Q_RANK = 384
MLA_KV_RANK = 256
MLA_NOPE = 64
MLA_ROPE = 32
MLA_V = 64
MLA_QK = MLA_NOPE + MLA_ROPE
RET_HEADS = 4
RET_QK = 256
RET_V = 512

LANES = 128
BF16_SUBLANES = 16
ONES_ROWS = BF16_SUBLANES
V7X_VMEM_LIMIT = 56 * 1024 * 1024

EVEN_IN_ROWS = 512
EVEN_IN_SUB = 256
ATTN_Q_ROWS = 512
ATTN_KV_ROWS = 512
ATTN_UNIT = 256
ATTN_GROUP = 4
FFN_ROWS = 512
ODD_IN_ROWS = 1024
ODD_IN_COLS = 1024
RET_ROWS = 1024
RET_CHUNK = 256
FF_CHUNKS = (768, 768, 768, 512)

NT_DIMS = (((1,), (1,)), ((), ()))


def _rms_scale(t):
    return lax.rsqrt(jnp.mean(t * t, axis=-1, keepdims=True) + EPS)


def _gelu_tanh(t):
    return 0.5 * t * (1.0 + jnp.tanh(math.sqrt(2.0 / math.pi) * (t + 0.044715 * (t * t * t))))


def _silu(t):
    return t * (1.0 / (1.0 + jnp.exp(-t)))


def _const_spec(shape):
    nd = len(shape)
    return pl.BlockSpec(shape, lambda *_: (0,) * nd, pipeline_mode=pl.Buffered(1))


def _params(sem):
    return pltpu.CompilerParams(dimension_semantics=sem, vmem_limit_bytes=V7X_VMEM_LIMIT)


def _even_in_kernel(x_ref, nmix_ref, wuv_ref, wct_ref, gvn_ref, ws_ref, bs_ref, gqa_ref, wuqt_ref,
                    gkva_ref, wukvt_ref, gq_ref, gk_ref, cos_ref, sin_ref,
                    a_ref, qt_ref, k_ref, vt_ref):
    subs = range(x_ref.shape[0] // EVEN_IN_SUB)
    proj = [_even_in_project(sub, x_ref, nmix_ref, wuv_ref, wct_ref) for sub in subs]
    heads = [_even_in_gate_and_expand(sub, proj[sub], gvn_ref, ws_ref, bs_ref, gqa_ref, wuqt_ref, gkva_ref,
                                      wukvt_ref, a_ref) for sub in subs]
    for sub in subs:
        _even_in_heads(sub, heads[sub], gq_ref, gk_ref, cos_ref, sin_ref, qt_ref, k_ref, vt_ref)


def _even_in_project(sub, x_ref, nmix_ref, wuv_ref, wct_ref):
    rows = EVEN_IN_SUB
    x = x_ref[sub * rows:(sub + 1) * rows, :]
    h = (x * _rms_scale(x) * nmix_ref[...]).astype(BF16)
    zuv = jnp.dot(h, wuv_ref[...], preferred_element_type=F32)
    zc = lax.dot_general(wct_ref[...], h, NT_DIMS, preferred_element_type=F32)
    return zuv, zc


def _even_in_gate_and_expand(sub, proj, gvn_ref, ws_ref, bs_ref, gqa_ref, wuqt_ref, gkva_ref, wukvt_ref, a_ref):
    rows = EVEN_IN_SUB
    zuv, zc = proj
    u = _gelu_tanh(zuv[:, :GM_WIDTH])
    v = _gelu_tanh(zuv[:, GM_WIDTH:])
    t_out = lax.broadcasted_iota(jnp.int32, (GM_BLOCK, GM_BLOCK), 0) // CHUNK
    t_in = lax.broadcasted_iota(jnp.int32, (GM_BLOCK, GM_BLOCK), 1) // CHUNK
    causal = t_in <= t_out
    nblk = rows // GM_BLOCK
    for g in range(GM_GROUPS):
        cs = slice(g * GM_GROUP_DIM, (g + 1) * GM_GROUP_DIM)
        vg = v[:, cs]
        vn = (vg * _rms_scale(vg) * gvn_ref[:, cs]).astype(BF16)
        wg = jnp.where(causal, ws_ref[g], 0.0).astype(BF16)
        vcat = jnp.concatenate([vn[j * GM_BLOCK:(j + 1) * GM_BLOCK] for j in range(nblk)], axis=1)
        s = jnp.dot(wg, vcat, preferred_element_type=F32) + bs_ref[:, g:g + 1]
        for j in range(nblk):
            rs = slice(j * GM_BLOCK, (j + 1) * GM_BLOCK)
            dst = slice(sub * rows + j * GM_BLOCK, sub * rows + (j + 1) * GM_BLOCK)
            a_ref[dst, cs] = (u[rs, cs] * s[:, j * GM_BLOCK:(j + 1) * GM_BLOCK]).astype(a_ref.dtype)

    o1 = MLA_Q_RANK
    o2 = o1 + MLA_KV_RANK
    cq, ckv, kpe = zc[:o1], zc[o1:o2], zc[o2:]

    def col_rms(t, n):
        return lax.rsqrt(jnp.sum(t * t, axis=0, keepdims=True) * (1.0 / n) + EPS)

    cqn = (cq * col_rms(cq, MLA_Q_RANK) * gqa_ref[...]).astype(BF16)
    ckvn = (ckv * col_rms(ckv, MLA_KV_RANK) * gkva_ref[...]).astype(BF16)
    qt = jnp.dot(wuqt_ref[...], cqn, preferred_element_type=F32)
    kvt = jnp.dot(wukvt_ref[...], ckvn, preferred_element_type=F32)
    return qt, kvt, kpe


def _even_in_heads(sub, expanded, gq_ref, gk_ref, cos_ref, sin_ref, qt_ref, k_ref, vt_ref):
    rows = EVEN_IN_SUB
    qt, kvt, kpe = expanded
    tok = slice(sub * rows, (sub + 1) * rows)
    cos, sin = cos_ref[:, tok], sin_ref[:, tok]
    half = MLA_ROPE // 2
    n1, n2 = MLA_NOPE, MLA_NOPE + half

    def rope(x1, x2):
        return x1 * cos - x2 * sin, x2 * cos + x1 * sin

    gq = gq_ref[...] * (MLA_QK ** -0.5 * math.log2(math.e))
    for hd in range(MLA_HEADS):
        t = qt[hd * LANES:(hd + 1) * LANES]
        r = lax.rsqrt(jnp.sum(t * t, axis=0, keepdims=True) * (1.0 / MLA_QK) + EPS)
        tn = t * gq
        r1, r2 = rope(tn[n1:n2], tn[n2:MLA_QK])
        base = hd * LANES
        qt_ref[base:base + n1, tok] = (tn[:n1] * r).astype(qt_ref.dtype)
        qt_ref[base + n1:base + n2, tok] = (r1 * r).astype(qt_ref.dtype)
        qt_ref[base + n2:base + MLA_QK, tok] = (r2 * r).astype(qt_ref.dtype)
        qt_ref[base + MLA_QK:base + LANES, tok] = jnp.zeros((LANES - MLA_QK, rows), qt_ref.dtype)

    gk = gk_ref[...]
    kpe_ss = jnp.sum(kpe * kpe, axis=0, keepdims=True)
    kr1, kr2 = rope(kpe[:half] * gk[n1:n2], kpe[half:] * gk[n2:MLA_QK])
    pad = jnp.zeros((LANES - MLA_QK, rows), F32)
    for hd in range(MLA_HEADS):
        t = kvt[hd * MLA_NOPE:(hd + 1) * MLA_NOPE]
        r = lax.rsqrt((jnp.sum(t * t, axis=0, keepdims=True) + kpe_ss) * (1.0 / MLA_QK) + EPS)
        kt = jnp.concatenate([t * gk[:n1] * r, kr1 * r, kr2 * r, pad], axis=0)
        k_ref[tok, hd * LANES:(hd + 1) * LANES] = kt.T.astype(k_ref.dtype)
    vt_ref[sub] = kvt[MLA_HEADS * MLA_NOPE:].astype(vt_ref.dtype)


def _even_in(x2, nmix, wuv, wct, gvn, ws, bs_t, gqa, wuqt, gkva, wukvt, gq, gk, cos_t, sin_t, seq):
    tokens, d = x2.shape
    rows = EVEN_IN_ROWS
    per_seq = seq // rows
    row_spec = lambda w: pl.BlockSpec((rows, w), lambda i: (i, 0))
    tab_spec = pl.BlockSpec((MLA_ROPE // 2, rows), lambda i: (0, i % per_seq))
    hw = MLA_HEADS * LANES
    vw = MLA_HEADS * MLA_V
    consts = [nmix, wuv, wct, gvn, ws, bs_t, gqa, wuqt, gkva, wukvt, gq, gk]
    return pl.pallas_call(
        _even_in_kernel,
        grid=(tokens // rows,),
        in_specs=[row_spec(d)] + [_const_spec(c.shape) for c in consts] + [tab_spec, tab_spec],
        out_specs=[row_spec(GM_WIDTH),
                   pl.BlockSpec((hw, rows), lambda i: (0, i)),
                   row_spec(hw),
                   pl.BlockSpec((rows // EVEN_IN_SUB, vw, EVEN_IN_SUB), lambda i: (i, 0, 0))],
        out_shape=[jax.ShapeDtypeStruct((tokens, GM_WIDTH), BF16),
                   jax.ShapeDtypeStruct((hw, tokens), BF16),
                   jax.ShapeDtypeStruct((tokens, hw), BF16),
                   jax.ShapeDtypeStruct((tokens // EVEN_IN_SUB, vw, EVEN_IN_SUB), BF16)],
        compiler_params=_params(("parallel",)),
        name="even_in",
    )(x2, *consts, cos_t, sin_t)


def _attn_kernel(*refs, n_groups, n_cast):
    qt_ref, k_ref, vt_ref = refs[:3]
    w_refs = refs[3:3 + n_cast]
    o_ref = refs[3 + n_cast]
    wb_refs = refs[4 + n_cast:4 + 2 * n_cast]
    sa_ref, sb_ref, ma_ref, mb_ref, m_ref, acc_ref = refs[4 + 2 * n_cast:]

    for w_ref, wb_ref in zip(w_refs, wb_refs):
        wb_ref[...] = w_ref[...].astype(wb_ref.dtype)

    tq = ATTN_Q_ROWS
    tk = ATTN_KV_ROWS
    vrows = vt_ref.shape[2]
    ones = jnp.ones((ONES_ROWS, vrows), BF16)
    bufs = ((sa_ref, ma_ref), (sb_ref, mb_ref))
    units = [(hh, slice(u * ATTN_UNIT, (u + 1) * ATTN_UNIT)) for hh in range(2) for u in range(tq // ATTN_UNIT)]

    def run(group):
        first = group * ATTN_GROUP
        tasks = [(c, t) for c in range(first, first + ATTN_GROUP) for t in range(c + 1)]

        def key_rows(c, j, qs):
            return qs.stop if j == c else tk

        def scores(n, hh, qs):
            c, j = tasks[n]
            dst, bmax = bufs[n % 2]
            rows = key_rows(c, j, qs)
            q0 = (c - first) * tq
            s = jnp.dot(k_ref[j * tk:j * tk + rows, hh * LANES:(hh + 1) * LANES],
                        qt_ref[hh * LANES:(hh + 1) * LANES, q0 + qs.start:q0 + qs.stop],
                        preferred_element_type=F32)
            if j == c:
                ck = lax.broadcasted_iota(jnp.int32, s.shape, 0) // CHUNK
                cq = (lax.broadcasted_iota(jnp.int32, s.shape, 1) + qs.start) // CHUNK
                s = jnp.where(ck <= cq, s, -jnp.inf)
            dst[hh, :rows, qs] = s
            bmax[hh, :, qs] = jnp.max(s, axis=0, keepdims=True)

        def update(n, hh, qs):
            c, j = tasks[n]
            src, bmax = bufs[n % 2]
            rows = key_rows(c, j, qs)
            slot = c % 2
            m = m_ref[slot, hh, :, qs]
            m_new = jnp.maximum(m, bmax[hh, :, qs])
            alpha = jnp.exp2(m - m_new)
            p = jnp.exp2(src[hh, :rows, qs] - m_new).astype(BF16)
            m_ref[slot, hh, :, qs] = m_new
            pv = None
            for ch in range(rows // vrows):
                vt = vt_ref[j * (tk // vrows) + ch, hh * MLA_V:(hh + 1) * MLA_V, :]
                part = jnp.dot(jnp.concatenate([vt, ones], axis=0), p[ch * vrows:(ch + 1) * vrows],
                               preferred_element_type=F32)
                pv = part if pv is None else pv + part
            acc_ref[slot, hh, :, qs] = alpha * acc_ref[slot, hh, :, qs] + pv

        def start_block(c):
            m_ref[c % 2] = jnp.full(m_ref.shape[1:], -jnp.inf, F32)
            acc_ref[c % 2] = jnp.zeros(acc_ref.shape[1:], F32)

        def finish_block(c):
            acc = acc_ref[c % 2]
            ot = jnp.concatenate([acc[hh, :MLA_V] * (1.0 / acc[hh, MLA_V:MLA_V + 1]) for hh in range(2)],
                                 axis=0)
            o_ref[(c - first) * tq:(c - first + 1) * tq, :] = ot.T.astype(o_ref.dtype)

        start_block(first)
        for hh, qs in units:
            scores(0, hh, qs)
        for n, (c, j) in enumerate(tasks):
            has_next = n + 1 < len(tasks)
            if has_next and tasks[n + 1][1] == 0:
                start_block(tasks[n + 1][0])
            for hh, qs in units:
                if has_next:
                    scores(n + 1, hh, qs)
                update(n, hh, qs)
            if j == c:
                finish_block(c)

    for group in range(n_groups):
        pl.when(pl.program_id(2) == group)(functools.partial(run, group))


def _attention(qt, k3, vt, b, s, cast_weights):
    assert ATTN_Q_ROWS == ATTN_KV_ROWS and ATTN_UNIT % vt.shape[2] == 0
    tq = ATTN_Q_ROWS
    pairs = MLA_HEADS // 2
    vchunks, _, vrows = vt.shape
    rows = ATTN_GROUP * tq
    n_groups = s // rows
    steps = b * pairs * n_groups
    step = lambda bi, p, g: (bi * pairs + p) * n_groups + g

    def cast_spec(w):
        share = next(sh for sh in (1, 2, 4, 8) if w.shape[0] % (steps // sh * BF16_SUBLANES) == 0)
        return pl.BlockSpec((w.shape[0] // (steps // share), w.shape[1]),
                            lambda bi, p, g: (step(bi, p, g) // share, 0))

    cast_specs = [cast_spec(w) for w in cast_weights]
    return pl.pallas_call(
        functools.partial(_attn_kernel, n_groups=n_groups, n_cast=len(cast_weights)),
        grid=(b, pairs, n_groups),
        in_specs=[pl.BlockSpec((2 * LANES, rows), lambda bi, p, g: (p, bi * n_groups + g)),
                  pl.BlockSpec((None, s, 2 * LANES), lambda bi, p, g: (bi, 0, p)),
                  pl.BlockSpec((vchunks // b, 2 * MLA_V, vrows), lambda bi, p, g: (bi, p, 0))] + cast_specs,
        out_specs=[pl.BlockSpec((None, rows, LANES), lambda bi, p, g: (bi, g, p))] + cast_specs,
        out_shape=[jax.ShapeDtypeStruct((b, s, MLA_HEADS * MLA_V), BF16)]
        + [jax.ShapeDtypeStruct(w.shape, BF16) for w in cast_weights],
        scratch_shapes=[pltpu.VMEM((2, ATTN_KV_ROWS, tq), F32),
                        pltpu.VMEM((2, ATTN_KV_ROWS, tq), F32),
                        pltpu.VMEM((2, 1, tq), F32),
                        pltpu.VMEM((2, 1, tq), F32),
                        pltpu.VMEM((2, 2, 1, tq), F32),
                        pltpu.VMEM((2, 2, MLA_V + ONES_ROWS, tq), F32)],
        compiler_params=_params(("arbitrary", "arbitrary", "arbitrary")),
        name="attention",
    )(qt, k3, vt, *cast_weights)


def _proj_ffn_kernel(*refs, n_in):
    in_refs = refs[:n_in]
    wout_ref, x_ref, nffn_ref, wg_ref, wu_ref, wd_ref, out_ref = refs[n_in:]
    mix = None
    off = 0
    for r in in_refs:
        kdim = r.shape[1]
        part = jnp.dot(r[...], wout_ref[off:off + kdim, :], preferred_element_type=F32)
        mix = part if mix is None else mix + part
        off += kdim
    x1 = x_ref[...] + mix
    h = (x1 * _rms_scale(x1) * nffn_ref[...]).astype(BF16)
    acc = x1
    c0 = 0
    for c in FF_CHUNKS:
        g = jnp.dot(h, wg_ref[:, c0:c0 + c], preferred_element_type=F32)
        u = jnp.dot(h, wu_ref[:, c0:c0 + c], preferred_element_type=F32)
        act = (_silu(g) * u).astype(BF16)
        acc = acc + jnp.dot(act, wd_ref[c0:c0 + c, :], preferred_element_type=F32)
        c0 += c
    out_ref[...] = acc


def _proj_ffn(mix_ins, wout, x2, nffn, wg, wu, wd, layer):
    tokens, d = x2.shape
    rows = FFN_ROWS
    assert sum(FF_CHUNKS) == wg.shape[2]
    row_spec = lambda w: pl.BlockSpec((rows, w), lambda i: (i, 0))
    layer_spec = lambda w: pl.BlockSpec((None,) + w.shape[1:], lambda i: (layer, 0, 0),
                                        pipeline_mode=pl.Buffered(1))
    return pl.pallas_call(
        functools.partial(_proj_ffn_kernel, n_in=len(mix_ins)),
        grid=(tokens // rows,),
        in_specs=[row_spec(m.shape[1]) for m in mix_ins]
        + [_const_spec(wout.shape), row_spec(d), _const_spec(nffn.shape),
           layer_spec(wg), layer_spec(wu), layer_spec(wd)],
        out_specs=row_spec(d),
        out_shape=jax.ShapeDtypeStruct((tokens, d), F32),
        compiler_params=_params(("parallel",)),
        name="proj_ffn",
    )(*mix_ins, wout, x2, nffn, wg, wu, wd)


def _odd_in_kernel(x_ref, nmix_ref, win_ref, cos_ref, sin_ref, q_ref, k_ref, v_ref, g_ref):
    x = x_ref[...]
    h = (x * _rms_scale(x) * nmix_ref[...]).astype(BF16)
    cos, sin = cos_ref[...], sin_ref[...]
    nq = RET_HEADS * RET_QK
    nv = RET_HEADS * RET_V
    half = RET_QK // 2

    def project(col0, width):
        return jnp.dot(h, win_ref[:, col0:col0 + width], preferred_element_type=F32)

    def rope_store(dst, col0, scale):
        t_all = project(col0, nq)
        for hd in range(RET_HEADS):
            t1 = t_all[:, hd * RET_QK:hd * RET_QK + half]
            t2 = t_all[:, hd * RET_QK + half:(hd + 1) * RET_QK]
            dst[:, hd * RET_QK:hd * RET_QK + half] = ((t1 * cos - t2 * sin) * scale).astype(dst.dtype)
            dst[:, hd * RET_QK + half:(hd + 1) * RET_QK] = ((t2 * cos + t1 * sin) * scale).astype(dst.dtype)

    rope_store(q_ref, 0, 1.0)
    rope_store(k_ref, nq, RET_QK ** -0.5)
    for c0 in range(0, nv, ODD_IN_COLS):
        g_ref[:, c0:c0 + ODD_IN_COLS] = _silu(project(2 * nq + nv + c0, ODD_IN_COLS)).astype(g_ref.dtype)
    for c0 in range(0, nv, ODD_IN_COLS):
        v_ref[:, c0:c0 + ODD_IN_COLS] = project(2 * nq + c0, ODD_IN_COLS).astype(v_ref.dtype)


def _odd_in(x2, nmix, win, cos, sin, seq):
    tokens, d = x2.shape
    rows = ODD_IN_ROWS
    per_seq = seq // rows
    nq = RET_HEADS * RET_QK
    nv = RET_HEADS * RET_V
    row_spec = lambda w: pl.BlockSpec((rows, w), lambda i: (i, 0))
    tab_spec = pl.BlockSpec((rows, RET_QK // 2), lambda i: (i % per_seq, 0))
    return pl.pallas_call(
        _odd_in_kernel,
        grid=(tokens // rows,),
        in_specs=[row_spec(d), _const_spec(nmix.shape), _const_spec(win.shape), tab_spec, tab_spec],
        out_specs=[row_spec(nq), row_spec(nq), row_spec(nv), row_spec(nv)],
        out_shape=[jax.ShapeDtypeStruct((tokens, nq), BF16), jax.ShapeDtypeStruct((tokens, nq), BF16),
                   jax.ShapeDtypeStruct((tokens, nv), BF16), jax.ShapeDtypeStruct((tokens, nv), BF16)],
        compiler_params=_params(("parallel",)),
        name="odd_in",
    )(x2, nmix, win, cos, sin)


def _retention_kernel(q_ref, k_ref, v_ref, g_ref, ron_ref, o_ref, state_ref, decay_ref, zeta_ref, xi_ref):
    tc = RET_CHUNK
    j = pl.program_id(1)

    def log_gamma(hd):
        return jnp.log(jnp.full((1, 1), 1.0 - 2.0 ** (-5.0 - hd), F32))

    @pl.when(j == 0)
    def _():
        state_ref[...] = jnp.zeros_like(state_ref)
        diff = (lax.broadcasted_iota(jnp.int32, (tc, tc), 0)
                - lax.broadcasted_iota(jnp.int32, (tc, tc), 1))
        dpos = jnp.maximum(diff, 0).astype(F32)
        kpos = lax.broadcasted_iota(jnp.int32, (tc, RET_QK), 0).astype(F32)
        for hd in range(RET_HEADS):
            decay_ref[hd] = jnp.where(diff >= 0, jnp.exp(log_gamma(hd) * dpos), 0.0).astype(decay_ref.dtype)
            zeta_ref[hd] = jnp.exp(log_gamma(hd) * (tc - 1.0 - kpos)).astype(zeta_ref.dtype)
            xi_ref[hd] = jnp.exp(log_gamma(hd) * (kpos + 1.0)).astype(xi_ref.dtype)

    chunk_decay = [jnp.exp(log_gamma(hd) * float(tc)) for hd in range(RET_HEADS)]
    heads = range(RET_HEADS)
    qcol = [slice(hd * RET_QK, (hd + 1) * RET_QK) for hd in heads]
    vcol = [slice(hd * RET_V, (hd + 1) * RET_V) for hd in heads]
    def in_chunk(rs):
        return [lax.dot_general(q_ref[rs, qcol[hd]], k_ref[rs, qcol[hd]], NT_DIMS,
                                preferred_element_type=F32).astype(BF16) * decay_ref[hd] for hd in heads]

    def outputs(rs, a):
        return [jnp.dot(jnp.concatenate([a[hd], q_ref[rs, qcol[hd]] * xi_ref[hd]], axis=1),
                        jnp.concatenate([v_ref[rs, vcol[hd]], state_ref[hd].astype(BF16)], axis=0),
                        preferred_element_type=F32) for hd in heads]

    def advance_state(rs):
        for hd in heads:
            state_ref[hd] = state_ref[hd] * chunk_decay[hd] + lax.dot_general(
                k_ref[rs, qcol[hd]] * zeta_ref[hd], v_ref[rs, vcol[hd]], (((0,), (0,)), ((), ())),
                preferred_element_type=F32)

    def norm_gate_store(rs, o):
        for hd in heads:
            on = (o[hd] * _rms_scale(o[hd]) * ron_ref[:, vcol[hd]]).astype(BF16)
            o_ref[rs, vcol[hd]] = on * g_ref[rs, vcol[hd]]

    chunks = [slice(c * tc, (c + 1) * tc) for c in range(q_ref.shape[0] // tc)]
    a = in_chunk(chunks[0])
    for ci, rs in enumerate(chunks):
        o = outputs(rs, a)
        advance_state(rs)
        if ci + 1 < len(chunks):
            a = in_chunk(chunks[ci + 1])
        norm_gate_store(rs, o)


def _retention(q3, k3, v3, g3, ron):
    b, s, nq = q3.shape
    nv = v3.shape[2]
    tc = RET_ROWS
    blk = lambda w: pl.BlockSpec((None, tc, w), lambda bi, j: (bi, j, 0))
    return pl.pallas_call(
        _retention_kernel,
        grid=(b, s // tc),
        in_specs=[blk(nq), blk(nq), blk(nv), blk(nv), _const_spec(ron.shape)],
        out_specs=blk(nv),
        out_shape=jax.ShapeDtypeStruct((b, s, nv), BF16),
        scratch_shapes=[pltpu.VMEM((RET_HEADS, RET_QK, RET_V), F32),
                        pltpu.VMEM((RET_HEADS, RET_CHUNK, RET_CHUNK), BF16),
                        pltpu.VMEM((RET_HEADS, RET_CHUNK, RET_QK), BF16),
                        pltpu.VMEM((RET_HEADS, RET_CHUNK, RET_QK), BF16)],
        compiler_params=_params(("arbitrary", "arbitrary")),
        name="retention",
    )(q3, k3, v3, g3, ron)


def _rope_tables(seq, half):
    inv = ROPE_THETA ** (-np.arange(half, dtype=np.float64) / half)
    ang = np.arange(seq, dtype=np.float64)[:, None] * inv[None, :]
    return np.cos(ang).astype(np.float32), np.sin(ang).astype(np.float32)


def _lane_bcast(g, width):
    return jnp.broadcast_to(g[:, None], (g.shape[0], width))


def kernel(x, norm_mix, norm_ffn, even_w_in, gm_v_norm, gm_w_s, gm_b_s, mla_q_a_norm, mla_w_uq,
           mla_kv_a_norm, mla_w_ukv, mla_q_norm, mla_k_norm, even_w_out, odd_w_in, ret_out_norm,
           odd_w_out, ffn_w_gate, ffn_w_up, ffn_w_down):
    b, s, d = x.shape
    tokens = b * s
    x2 = x.reshape(tokens, d)
    row = lambda t: t.reshape(1, -1)

    w_in = even_w_in[0]
    wuv = w_in[:, :2 * GM_WIDTH].astype(BF16)
    wct = w_in[:, 2 * GM_WIDTH:].T.astype(BF16)
    wuq3 = mla_w_uq[0].reshape(MLA_Q_RANK, MLA_HEADS, MLA_QK)
    wuqt = jnp.pad(wuq3, ((0, 0), (0, 0), (0, LANES - MLA_QK))).reshape(MLA_Q_RANK, -1).T.astype(BF16)
    wukv3 = mla_w_ukv[0].reshape(MLA_KV_RANK, MLA_HEADS, MLA_NOPE + MLA_V)
    wukvt = jnp.concatenate([wukv3[:, :, :MLA_NOPE].reshape(MLA_KV_RANK, -1),
                             wukv3[:, :, MLA_NOPE:].reshape(MLA_KV_RANK, -1)], axis=1).T.astype(BF16)
    er = EVEN_IN_SUB
    pad_head = lambda g: jnp.pad(g, (0, LANES - MLA_QK))
    cos0, sin0 = _rope_tables(s, MLA_ROPE // 2)
    a, qt, k, vt = _even_in(
        x2, row(norm_mix[0]), wuv, wct, row(gm_v_norm[0]), gm_w_s[0], gm_b_s[0].T,
        _lane_bcast(mla_q_a_norm[0], er), wuqt, _lane_bcast(mla_kv_a_norm[0], er), wukvt,
        _lane_bcast(pad_head(mla_q_norm[0]), er), _lane_bcast(pad_head(mla_k_norm[0]), er),
        jnp.asarray(cos0.T), jnp.asarray(sin0.T), s)
    later = [ffn_w_gate, ffn_w_up, ffn_w_down, even_w_out[0], odd_w_in[0], odd_w_out[0]]
    flat = lambda w: w.reshape(-1, w.shape[-1])
    o, *cast = _attention(qt, k.reshape(b, s, MLA_HEADS * LANES), vt, b, s, [flat(w) for w in later])
    wg, wu, wd, w_eo, w_oi, w_oo = [c.reshape(w.shape) for c, w in zip(cast, later)]
    x2 = _proj_ffn([a, o.reshape(tokens, -1)], w_eo, x2, row(norm_ffn[0]), wg, wu, wd, 0)

    cos1, sin1 = _rope_tables(s, RET_QK // 2)
    rq, rk, rv, rg = _odd_in(x2, row(norm_mix[1]), w_oi, jnp.asarray(cos1), jnp.asarray(sin1), s)
    nq = RET_HEADS * RET_QK
    nv = RET_HEADS * RET_V
    og = _retention(rq.reshape(b, s, nq), rk.reshape(b, s, nq), rv.reshape(b, s, nv),
                    rg.reshape(b, s, nv), row(ret_out_norm[0]))
    x2 = _proj_ffn([og.reshape(tokens, nv)], w_oo, x2, row(norm_ffn[1]), wg, wu, wd, 1)
    return x2.reshape(b, s, d)
```

```python
import functools
import math

import jax
import jax.numpy as jnp
import numpy as np
from jax import lax
from jax.experimental import pallas as pl
from jax.experimental.pallas import tpu as pltpu

F32 = jnp.float32
BF16 = jnp.bfloat16

EPS = 1e-6
ROPE_THETA = 10000.0
CHUNK = 64

GM_GROUPS = 4
GM_GROUP_DIM = 128
GM_WIDTH = GM_GROUPS * GM_GROUP_DIM
GM_BLOCK = 128
MLA_HEADS = 8
MLA_Q_RANK = 384
MLA_KV_RANK = 256
MLA_NOPE = 64
MLA_ROPE = 32
MLA_V = 64
MLA_QK = MLA_NOPE + MLA_ROPE
RET_HEADS = 4
RET_QK = 256
RET_V = 512

LANES = 128
BF16_SUBLANES = 16
ONES_ROWS = BF16_SUBLANES
V7X_VMEM_LIMIT = 56 * 1024 * 1024

EVEN_IN_ROWS = 512
EVEN_IN_SUB = 256
ATTN_Q_ROWS = 512
ATTN_KV_ROWS = 512
ATTN_UNIT = 256
ATTN_GROUP = 4
FFN_ROWS = 512
ODD_IN_ROWS = 1024
ODD_IN_COLS = 1024
RET_ROWS = 1024
RET_CHUNK = 256
FF_CHUNKS = (768, 768, 768, 512)

NT_DIMS = (((1,), (1,)), ((), ()))


def _rms_scale(t):
    return lax.rsqrt(jnp.mean(t * t, axis=-1, keepdims=True) + EPS)


def _gelu_tanh(t):
    return 0.5 * t * (1.0 + jnp.tanh(math.sqrt(2.0 / math.pi) * (t + 0.044715 * (t * t * t))))


def _silu(t):
    return t * (1.0 / (1.0 + jnp.exp(-t)))


def _const_spec(shape):
    nd = len(shape)
    return pl.BlockSpec(shape, lambda *_: (0,) * nd, pipeline_mode=pl.Buffered(1))


def _params(sem):
    return pltpu.CompilerParams(dimension_semantics=sem, vmem_limit_bytes=V7X_VMEM_LIMIT)


def _even_in_kernel(x_ref, nmix_ref, wuv_ref, wct_ref, gvn_ref, ws_ref, bs_ref, gqa_ref, wuqt_ref,
                    gkva_ref, wukvt_ref, gq_ref, gk_ref, cos_ref, sin_ref,
                    a_ref, qt_ref, k_ref, vt_ref):
    subs = range(x_ref.shape[0] // EVEN_IN_SUB)
    proj = [_even_in_project(sub, x_ref, nmix_ref, wuv_ref, wct_ref) for sub in subs]
    heads = [_even_in_gate_and_expand(sub, proj[sub], gvn_ref, ws_ref, bs_ref, gqa_ref, wuqt_ref, gkva_ref,
                                      wukvt_ref, a_ref) for sub in subs]
    for sub in subs:
        _even_in_heads(sub, heads[sub], gq_ref, gk_ref, cos_ref, sin_ref, qt_ref, k_ref, vt_ref)


def _even_in_project(sub, x_ref, nmix_ref, wuv_ref, wct_ref):
    rows = EVEN_IN_SUB
    x = x_ref[sub * rows:(sub + 1) * rows, :]
    h = (x * _rms_scale(x) * nmix_ref[...]).astype(BF16)
    zuv = jnp.dot(h, wuv_ref[...], preferred_element_type=F32)
    zc = lax.dot_general(wct_ref[...], h, NT_DIMS, preferred_element_type=F32)
    return zuv, zc


def _even_in_gate_and_expand(sub, proj, gvn_ref, ws_ref, bs_ref, gqa_ref, wuqt_ref, gkva_ref, wukvt_ref, a_ref):
    rows = EVEN_IN_SUB
    zuv, zc = proj
    u = _gelu_tanh(zuv[:, :GM_WIDTH])
    v = _gelu_tanh(zuv[:, GM_WIDTH:])
    t_out = lax.broadcasted_iota(jnp.int32, (GM_BLOCK, GM_BLOCK), 0) // CHUNK
    t_in = lax.broadcasted_iota(jnp.int32, (GM_BLOCK, GM_BLOCK), 1) // CHUNK
    causal = t_in <= t_out
    nblk = rows // GM_BLOCK
    for g in range(GM_GROUPS):
        cs = slice(g * GM_GROUP_DIM, (g + 1) * GM_GROUP_DIM)
        vg = v[:, cs]
        vn = (vg * _rms_scale(vg) * gvn_ref[:, cs]).astype(BF16)
        wg = jnp.where(causal, ws_ref[g], 0.0).astype(BF16)
        vcat = jnp.concatenate([vn[j * GM_BLOCK:(j + 1) * GM_BLOCK] for j in range(nblk)], axis=1)
        s = jnp.dot(wg, vcat, preferred_element_type=F32) + bs_ref[:, g:g + 1]
        for j in range(nblk):
            rs = slice(j * GM_BLOCK, (j + 1) * GM_BLOCK)
            dst = slice(sub * rows + j * GM_BLOCK, sub * rows + (j + 1) * GM_BLOCK)
            a_ref[dst, cs] = (u[rs, cs] * s[:, j * GM_BLOCK:(j + 1) * GM_BLOCK]).astype(a_ref.dtype)

    o1 = MLA_Q_RANK
    o2 = o1 + MLA_KV_RANK
    cq, ckv, kpe = zc[:o1], zc[o1:o2], zc[o2:]

    def col_rms(t, n):
        return lax.rsqrt(jnp.sum(t * t, axis=0, keepdims=True) * (1.0 / n) + EPS)

    cqn = (cq * col_rms(cq, MLA_Q_RANK) * gqa_ref[...]).astype(BF16)
    ckvn = (ckv * col_rms(ckv, MLA_KV_RANK) * gkva_ref[...]).astype(BF16)
    qt = jnp.dot(wuqt_ref[...], cqn, preferred_element_type=F32)
    kvt = jnp.dot(wukvt_ref[...], ckvn, preferred_element_type=F32)
    return qt, kvt, kpe


def _even_in_heads(sub, expanded, gq_ref, gk_ref, cos_ref, sin_ref, qt_ref, k_ref, vt_ref):
    rows = EVEN_IN_SUB
    qt, kvt, kpe = expanded
    tok = slice(sub * rows, (sub + 1) * rows)
    cos, sin = cos_ref[:, tok], sin_ref[:, tok]
    half = MLA_ROPE // 2
    n1, n2 = MLA_NOPE, MLA_NOPE + half

    def rope(x1, x2):
        return x1 * cos - x2 * sin, x2 * cos + x1 * sin

    gq = gq_ref[...] * (MLA_QK ** -0.5 * math.log2(math.e))
    for hd in range(MLA_HEADS):
        t = qt[hd * LANES:(hd + 1) * LANES]
        r = lax.rsqrt(jnp.sum(t * t, axis=0, keepdims=True) * (1.0 / MLA_QK) + EPS)
        tn = t * gq
        r1, r2 = rope(tn[n1:n2], tn[n2:MLA_QK])
        base = hd * LANES
        qt_ref[base:base + n1, tok] = (tn[:n1] * r).astype(qt_ref.dtype)
        qt_ref[base + n1:base + n2, tok] = (r1 * r).astype(qt_ref.dtype)
        qt_ref[base + n2:base + MLA_QK, tok] = (r2 * r).astype(qt_ref.dtype)
        qt_ref[base + MLA_QK:base + LANES, tok] = jnp.zeros((LANES - MLA_QK, rows), qt_ref.dtype)

    gk = gk_ref[...]
    kpe_ss = jnp.sum(kpe * kpe, axis=0, keepdims=True)
    kr1, kr2 = rope(kpe[:half] * gk[n1:n2], kpe[half:] * gk[n2:MLA_QK])
    pad = jnp.zeros((LANES - MLA_QK, rows), F32)
    for hd in range(MLA_HEADS):
        t = kvt[hd * MLA_NOPE:(hd + 1) * MLA_NOPE]
        r = lax.rsqrt((jnp.sum(t * t, axis=0, keepdims=True) + kpe_ss) * (1.0 / MLA_QK) + EPS)
        kt = jnp.concatenate([t * gk[:n1] * r, kr1 * r, kr2 * r, pad], axis=0)
        k_ref[tok, hd * LANES:(hd + 1) * LANES] = kt.T.astype(k_ref.dtype)
    vt_ref[sub] = kvt[MLA_HEADS * MLA_NOPE:].astype(vt_ref.dtype)


def _even_in(x2, nmix, wuv, wct, gvn, ws, bs_t, gqa, wuqt, gkva, wukvt, gq, gk, cos_t, sin_t, seq):
    tokens, d = x2.shape
    rows = EVEN_IN_ROWS
    per_seq = seq // rows
    row_spec = lambda w: pl.BlockSpec((rows, w), lambda i: (i, 0))
    tab_spec = pl.BlockSpec((MLA_ROPE // 2, rows), lambda i: (0, i % per_seq))
    hw = MLA_HEADS * LANES
    vw = MLA_HEADS * MLA_V
    consts = [nmix, wuv, wct, gvn, ws, bs_t, gqa, wuqt, gkva, wukvt, gq, gk]
    return pl.pallas_call(
        _even_in_kernel,
        grid=(tokens // rows,),
        in_specs=[row_spec(d)] + [_const_spec(c.shape) for c in consts] + [tab_spec, tab_spec],
        out_specs=[row_spec(GM_WIDTH),
                   pl.BlockSpec((hw, rows), lambda i: (0, i)),
                   row_spec(hw),
                   pl.BlockSpec((rows // EVEN_IN_SUB, vw, EVEN_IN_SUB), lambda i: (i, 0, 0))],
        out_shape=[jax.ShapeDtypeStruct((tokens, GM_WIDTH), BF16),
                   jax.ShapeDtypeStruct((hw, tokens), BF16),
                   jax.ShapeDtypeStruct((tokens, hw), BF16),
                   jax.ShapeDtypeStruct((tokens // EVEN_IN_SUB, vw, EVEN_IN_SUB), BF16)],
        compiler_params=_params(("parallel",)),
        name="even_in",
    )(x2, *consts, cos_t, sin_t)


def _attn_kernel(*refs, n_groups, n_cast):
    qt_ref, k_ref, vt_ref = refs[:3]
    w_refs = refs[3:3 + n_cast]
    o_ref = refs[3 + n_cast]
    wb_refs = refs[4 + n_cast:4 + 2 * n_cast]
    sa_ref, sb_ref, ma_ref, mb_ref, m_ref, acc_ref = refs[4 + 2 * n_cast:]

    for w_ref, wb_ref in zip(w_refs, wb_refs):
        wb_ref[...] = w_ref[...].astype(wb_ref.dtype)

    tq = ATTN_Q_ROWS
    tk = ATTN_KV_ROWS
    vrows = vt_ref.shape[2]
    ones = jnp.ones((ONES_ROWS, vrows), BF16)
    bufs = ((sa_ref, ma_ref), (sb_ref, mb_ref))
    units = [(hh, slice(u * ATTN_UNIT, (u + 1) * ATTN_UNIT)) for hh in range(2) for u in range(tq // ATTN_UNIT)]

    def run(group):
        first = group * ATTN_GROUP
        tasks = [(c, t) for c in range(first, first + ATTN_GROUP) for t in range(c + 1)]

        def key_rows(c, j, qs):
            return qs.stop if j == c else tk

        def scores(n, hh, qs):
            c, j = tasks[n]
            dst, bmax = bufs[n % 2]
            rows = key_rows(c, j, qs)
            q0 = (c - first) * tq
            s = jnp.dot(k_ref[j * tk:j * tk + rows, hh * LANES:(hh + 1) * LANES],
                        qt_ref[hh * LANES:(hh + 1) * LANES, q0 + qs.start:q0 + qs.stop],
                        preferred_element_type=F32)
            if j == c:
                ck = lax.broadcasted_iota(jnp.int32, s.shape, 0) // CHUNK
                cq = (lax.broadcasted_iota(jnp.int32, s.shape, 1) + qs.start) // CHUNK
                s = jnp.where(ck <= cq, s, -jnp.inf)
            dst[hh, :rows, qs] = s
            bmax[hh, :, qs] = jnp.max(s, axis=0, keepdims=True)

        def update(n, hh, qs):
            c, j = tasks[n]
            src, bmax = bufs[n % 2]
            rows = key_rows(c, j, qs)
            slot = c % 2
            m = m_ref[slot, hh, :, qs]
            m_new = jnp.maximum(m, bmax[hh, :, qs])
            alpha = jnp.exp2(m - m_new)
            p = jnp.exp2(src[hh, :rows, qs] - m_new).astype(BF16)
            m_ref[slot, hh, :, qs] = m_new
            pv = None
            for ch in range(rows // vrows):
                vt = vt_ref[j * (tk // vrows) + ch, hh * MLA_V:(hh + 1) * MLA_V, :]
                part = jnp.dot(jnp.concatenate([vt, ones], axis=0), p[ch * vrows:(ch + 1) * vrows],
                               preferred_element_type=F32)
                pv = part if pv is None else pv + part
            acc_ref[slot, hh, :, qs] = alpha * acc_ref[slot, hh, :, qs] + pv

        def start_block(c):
            m_ref[c % 2] = jnp.full(m_ref.shape[1:], -jnp.inf, F32)
            acc_ref[c % 2] = jnp.zeros(acc_ref.shape[1:], F32)

        def finish_block(c):
            acc = acc_ref[c % 2]
            ot = jnp.concatenate([acc[hh, :MLA_V] * (1.0 / acc[hh, MLA_V:MLA_V + 1]) for hh in range(2)],
                                 axis=0)
            o_ref[(c - first) * tq:(c - first + 1) * tq, :] = ot.T.astype(o_ref.dtype)

        start_block(first)
        for hh, qs in units:
            scores(0, hh, qs)
        for n, (c, j) in enumerate(tasks):
            has_next = n + 1 < len(tasks)
            if has_next and tasks[n + 1][1] == 0:
                start_block(tasks[n + 1][0])
            for hh, qs in units:
                if has_next:
                    scores(n + 1, hh, qs)
                update(n, hh, qs)
            if j == c:
                finish_block(c)

    for group in range(n_groups):
        pl.when(pl.program_id(2) == group)(functools.partial(run, group))


def _attention(qt, k3, vt, b, s, cast_weights):
    assert ATTN_Q_ROWS == ATTN_KV_ROWS and ATTN_UNIT % vt.shape[2] == 0
    tq = ATTN_Q_ROWS
    pairs = MLA_HEADS // 2
    vchunks, _, vrows = vt.shape
    rows = ATTN_GROUP * tq
    n_groups = s // rows
    steps = b * pairs * n_groups
    step = lambda bi, p, g: (bi * pairs + p) * n_groups + g

    def cast_spec(w):
        share = next(sh for sh in (1, 2, 4, 8) if w.shape[0] % (steps // sh * BF16_SUBLANES) == 0)
        return pl.BlockSpec((w.shape[0] // (steps // share), w.shape[1]),
                            lambda bi, p, g: (step(bi, p, g) // share, 0))

    cast_specs = [cast_spec(w) for w in cast_weights]
    return pl.pallas_call(
        functools.partial(_attn_kernel, n_groups=n_groups, n_cast=len(cast_weights)),
        grid=(b, pairs, n_groups),
        in_specs=[pl.BlockSpec((2 * LANES, rows), lambda bi, p, g: (p, bi * n_groups + g)),
                  pl.BlockSpec((None, s, 2 * LANES), lambda bi, p, g: (bi, 0, p)),
                  pl.BlockSpec((vchunks // b, 2 * MLA_V, vrows), lambda bi, p, g: (bi, p, 0))] + cast_specs,
        out_specs=[pl.BlockSpec((None, rows, LANES), lambda bi, p, g: (bi, g, p))] + cast_specs,
        out_shape=[jax.ShapeDtypeStruct((b, s, MLA_HEADS * MLA_V), BF16)]
        + [jax.ShapeDtypeStruct(w.shape, BF16) for w in cast_weights],
        scratch_shapes=[pltpu.VMEM((2, ATTN_KV_ROWS, tq), F32),
                        pltpu.VMEM((2, ATTN_KV_ROWS, tq), F32),
                        pltpu.VMEM((2, 1, tq), F32),
                        pltpu.VMEM((2, 1, tq), F32),
                        pltpu.VMEM((2, 2, 1, tq), F32),
                        pltpu.VMEM((2, 2, MLA_V + ONES_ROWS, tq), F32)],
        compiler_params=_params(("arbitrary", "arbitrary", "arbitrary")),
        name="attention",
    )(qt, k3, vt, *cast_weights)


def _proj_ffn_kernel(*refs, n_in, gated):
    in_refs = refs[:n_in]
    gate_refs = refs[n_in:2 * n_in] if gated else [None] * n_in
    wout_ref, x_ref, nffn_ref, wg_ref, wu_ref, wd_ref, out_ref = refs[n_in * (2 if gated else 1):]
    mix = None
    off = 0
    for r, gate in zip(in_refs, gate_refs):
        kdim = r.shape[1]
        lhs = r[...] if gate is None else r[...] * gate[...]
        part = jnp.dot(lhs, wout_ref[off:off + kdim, :], preferred_element_type=F32)
        mix = part if mix is None else mix + part
        off += kdim
    x1 = x_ref[...] + mix
    h = (x1 * _rms_scale(x1) * nffn_ref[...]).astype(BF16)
    acc = x1
    c0 = 0
    for c in FF_CHUNKS:
        g = jnp.dot(h, wg_ref[:, c0:c0 + c], preferred_element_type=F32)
        u = jnp.dot(h, wu_ref[:, c0:c0 + c], preferred_element_type=F32)
        act = (_silu(g) * u).astype(BF16)
        acc = acc + jnp.dot(act, wd_ref[c0:c0 + c, :], preferred_element_type=F32)
        c0 += c
    out_ref[...] = acc


def _proj_ffn(mix_ins, wout, x2, nffn, wg, wu, wd, layer, gates=()):
    tokens, d = x2.shape
    rows = FFN_ROWS
    assert sum(FF_CHUNKS) == wg.shape[2] and len(gates) in (0, len(mix_ins))
    row_spec = lambda w: pl.BlockSpec((rows, w), lambda i: (i, 0))
    layer_spec = lambda w: pl.BlockSpec((None,) + w.shape[1:], lambda i: (layer, 0, 0),
                                        pipeline_mode=pl.Buffered(1))
    return pl.pallas_call(
        functools.partial(_proj_ffn_kernel, n_in=len(mix_ins), gated=bool(gates)),
        grid=(tokens // rows,),
        in_specs=[row_spec(m.shape[1]) for m in (*mix_ins, *gates)]
        + [_const_spec(wout.shape), row_spec(d), _const_spec(nffn.shape),
           layer_spec(wg), layer_spec(wu), layer_spec(wd)],
        out_specs=row_spec(d),
        out_shape=jax.ShapeDtypeStruct((tokens, d), F32),
        compiler_params=_params(("parallel",)),
        name="proj_ffn",
    )(*mix_ins, *gates, wout, x2, nffn, wg, wu, wd)


def _odd_in_kernel(x_ref, nmix_ref, win_ref, cos_ref, sin_ref, q_ref, k_ref, v_ref, g_ref):
    x = x_ref[...]
    h = (x * _rms_scale(x) * nmix_ref[...]).astype(BF16)
    cos, sin = cos_ref[...], sin_ref[...]
    nq = RET_HEADS * RET_QK
    nv = RET_HEADS * RET_V
    half = RET_QK // 2

    def project(col0, width):
        return jnp.dot(h, win_ref[:, col0:col0 + width], preferred_element_type=F32)

    def rope_store(dst, col0, scale):
        t_all = project(col0, nq)
        for hd in range(RET_HEADS):
            t1 = t_all[:, hd * RET_QK:hd * RET_QK + half]
            t2 = t_all[:, hd * RET_QK + half:(hd + 1) * RET_QK]
            dst[:, hd * RET_QK:hd * RET_QK + half] = ((t1 * cos - t2 * sin) * scale).astype(dst.dtype)
            dst[:, hd * RET_QK + half:(hd + 1) * RET_QK] = ((t2 * cos + t1 * sin) * scale).astype(dst.dtype)

    rope_store(q_ref, 0, 1.0)
    rope_store(k_ref, nq, RET_QK ** -0.5)
    for c0 in range(0, nv, ODD_IN_COLS):
        g_ref[:, c0:c0 + ODD_IN_COLS] = _silu(project(2 * nq + nv + c0, ODD_IN_COLS)).astype(g_ref.dtype)
    for c0 in range(0, nv, ODD_IN_COLS):
        v_ref[:, c0:c0 + ODD_IN_COLS] = project(2 * nq + c0, ODD_IN_COLS).astype(v_ref.dtype)


def _odd_in(x2, nmix, win, cos, sin, seq):
    tokens, d = x2.shape
    rows = ODD_IN_ROWS
    per_seq = seq // rows
    nq = RET_HEADS * RET_QK
    nv = RET_HEADS * RET_V
    row_spec = lambda w: pl.BlockSpec((rows, w), lambda i: (i, 0))
    tab_spec = pl.BlockSpec((rows, RET_QK // 2), lambda i: (i % per_seq, 0))
    return pl.pallas_call(
        _odd_in_kernel,
        grid=(tokens // rows,),
        in_specs=[row_spec(d), _const_spec(nmix.shape), _const_spec(win.shape), tab_spec, tab_spec],
        out_specs=[row_spec(nq), row_spec(nq), row_spec(nv), row_spec(nv)],
        out_shape=[jax.ShapeDtypeStruct((tokens, nq), BF16), jax.ShapeDtypeStruct((tokens, nq), BF16),
                   jax.ShapeDtypeStruct((tokens, nv), BF16), jax.ShapeDtypeStruct((tokens, nv), BF16)],
        compiler_params=_params(("parallel",)),
        name="odd_in",
    )(x2, nmix, win, cos, sin)


def _retention_kernel(q_ref, k_ref, v_ref, ron_ref, o_ref, state_ref, decay_ref, zeta_ref, xi_ref):
    tc = RET_CHUNK
    j = pl.program_id(1)

    def log_gamma(hd):
        return jnp.log(jnp.full((1, 1), 1.0 - 2.0 ** (-5.0 - hd), F32))

    @pl.when(j == 0)
    def _():
        state_ref[...] = jnp.zeros_like(state_ref)
        diff = (lax.broadcasted_iota(jnp.int32, (tc, tc), 0)
                - lax.broadcasted_iota(jnp.int32, (tc, tc), 1))
        dpos = jnp.maximum(diff, 0).astype(F32)
        kpos = lax.broadcasted_iota(jnp.int32, (tc, RET_QK), 0).astype(F32)
        for hd in range(RET_HEADS):
            decay_ref[hd] = jnp.where(diff >= 0, jnp.exp(log_gamma(hd) * dpos), 0.0).astype(decay_ref.dtype)
            zeta_ref[hd] = jnp.exp(log_gamma(hd) * (tc - 1.0 - kpos)).astype(zeta_ref.dtype)
            xi_ref[hd] = jnp.exp(log_gamma(hd) * (kpos + 1.0)).astype(xi_ref.dtype)

    chunk_decay = [jnp.exp(log_gamma(hd) * float(tc)) for hd in range(RET_HEADS)]
    heads = range(RET_HEADS)
    qcol = [slice(hd * RET_QK, (hd + 1) * RET_QK) for hd in heads]
    vcol = [slice(hd * RET_V, (hd + 1) * RET_V) for hd in heads]
    def in_chunk(rs):
        return [lax.dot_general(q_ref[rs, qcol[hd]], k_ref[rs, qcol[hd]], NT_DIMS,
                                preferred_element_type=F32).astype(BF16) * decay_ref[hd] for hd in heads]

    def outputs(rs, a):
        return [jnp.dot(jnp.concatenate([a[hd], q_ref[rs, qcol[hd]] * xi_ref[hd]], axis=1),
                        jnp.concatenate([v_ref[rs, vcol[hd]], state_ref[hd].astype(BF16)], axis=0),
                        preferred_element_type=F32) for hd in heads]

    def advance_state(rs):
        for hd in heads:
            state_ref[hd] = state_ref[hd] * chunk_decay[hd] + lax.dot_general(
                k_ref[rs, qcol[hd]] * zeta_ref[hd], v_ref[rs, vcol[hd]], (((0,), (0,)), ((), ())),
                preferred_element_type=F32)

    def norm_store(rs, o):
        for hd in heads:
            o_ref[rs, vcol[hd]] = (o[hd] * _rms_scale(o[hd]) * ron_ref[:, vcol[hd]]).astype(o_ref.dtype)

    chunks = [slice(c * tc, (c + 1) * tc) for c in range(q_ref.shape[0] // tc)]
    a = in_chunk(chunks[0])
    for ci, rs in enumerate(chunks):
        o = outputs(rs, a)
        advance_state(rs)
        if ci + 1 < len(chunks):
            a = in_chunk(chunks[ci + 1])
        norm_store(rs, o)


def _retention(q3, k3, v3, ron):
    b, s, nq = q3.shape
    nv = v3.shape[2]
    tc = RET_ROWS
    blk = lambda w: pl.BlockSpec((None, tc, w), lambda bi, j: (bi, j, 0))
    return pl.pallas_call(
        _retention_kernel,
        grid=(b, s // tc),
        in_specs=[blk(nq), blk(nq), blk(nv), _const_spec(ron.shape)],
        out_specs=blk(nv),
        out_shape=jax.ShapeDtypeStruct((b, s, nv), BF16),
        scratch_shapes=[pltpu.VMEM((RET_HEADS, RET_QK, RET_V), F32),
                        pltpu.VMEM((RET_HEADS, RET_CHUNK, RET_CHUNK), BF16),
                        pltpu.VMEM((RET_HEADS, RET_CHUNK, RET_QK), BF16),
                        pltpu.VMEM((RET_HEADS, RET_CHUNK, RET_QK), BF16)],
        compiler_params=_params(("arbitrary", "arbitrary")),
        name="retention",
    )(q3, k3, v3, ron)


def _rope_tables(seq, half):
    inv = ROPE_THETA ** (-np.arange(half, dtype=np.float64) / half)
    ang = np.arange(seq, dtype=np.float64)[:, None] * inv[None, :]
    return np.cos(ang).astype(np.float32), np.sin(ang).astype(np.float32)


def _lane_bcast(g, width):
    return jnp.broadcast_to(g[:, None], (g.shape[0], width))


def kernel(x, norm_mix, norm_ffn, even_w_in, gm_v_norm, gm_w_s, gm_b_s, mla_q_a_norm, mla_w_uq,
           mla_kv_a_norm, mla_w_ukv, mla_q_norm, mla_k_norm, even_w_out, odd_w_in, ret_out_norm,
           odd_w_out, ffn_w_gate, ffn_w_up, ffn_w_down):
    b, s, d = x.shape
    tokens = b * s
    x2 = x.reshape(tokens, d)
    row = lambda t: t.reshape(1, -1)

    w_in = even_w_in[0]
    wuv = w_in[:, :2 * GM_WIDTH].astype(BF16)
    wct = w_in[:, 2 * GM_WIDTH:].T.astype(BF16)
    wuq3 = mla_w_uq[0].reshape(MLA_Q_RANK, MLA_HEADS, MLA_QK)
    wuqt = jnp.pad(wuq3, ((0, 0), (0, 0), (0, LANES - MLA_QK))).reshape(MLA_Q_RANK, -1).T.astype(BF16)
    wukv3 = mla_w_ukv[0].reshape(MLA_KV_RANK, MLA_HEADS, MLA_NOPE + MLA_V)
    wukvt = jnp.concatenate([wukv3[:, :, :MLA_NOPE].reshape(MLA_KV_RANK, -1),
                             wukv3[:, :, MLA_NOPE:].reshape(MLA_KV_RANK, -1)], axis=1).T.astype(BF16)
    er = EVEN_IN_SUB
    pad_head = lambda g: jnp.pad(g, (0, LANES - MLA_QK))
    cos0, sin0 = _rope_tables(s, MLA_ROPE // 2)
    a, qt, k, vt = _even_in(
        x2, row(norm_mix[0]), wuv, wct, row(gm_v_norm[0]), gm_w_s[0], gm_b_s[0].T,
        _lane_bcast(mla_q_a_norm[0], er), wuqt, _lane_bcast(mla_kv_a_norm[0], er), wukvt,
        _lane_bcast(pad_head(mla_q_norm[0]), er), _lane_bcast(pad_head(mla_k_norm[0]), er),
        jnp.asarray(cos0.T), jnp.asarray(sin0.T), s)
    later = [ffn_w_gate, ffn_w_up, ffn_w_down, even_w_out[0], odd_w_in[0], odd_w_out[0]]
    flat = lambda w: w.reshape(-1, w.shape[-1])
    o, *cast = _attention(qt, k.reshape(b, s, MLA_HEADS * LANES), vt, b, s, [flat(w) for w in later])
    wg, wu, wd, w_eo, w_oi, w_oo = [c.reshape(w.shape) for c, w in zip(cast, later)]
    x2 = _proj_ffn([a, o.reshape(tokens, -1)], w_eo, x2, row(norm_ffn[0]), wg, wu, wd, 0)

    cos1, sin1 = _rope_tables(s, RET_QK // 2)
    rq, rk, rv, rg = _odd_in(x2, row(norm_mix[1]), w_oi, jnp.asarray(cos1), jnp.asarray(sin1), s)
    nq = RET_HEADS * RET_QK
    nv = RET_HEADS * RET_V
    on = _retention(rq.reshape(b, s, nq), rk.reshape(b, s, nq), rv.reshape(b, s, nv), row(ret_out_norm[0]))
    x2 = _proj_ffn([on.reshape(tokens, nv)], w_oo, x2, row(norm_ffn[1]), wg, wu, wd, 1, gates=[rg])
    return x2.reshape(b, s, d)
```

```python
import functools
import math

import jax
import jax.numpy as jnp
import numpy as np
from jax import lax
from jax.experimental import pallas as pl
from jax.experimental.pallas import tpu as pltpu

F32 = jnp.float32
BF16 = jnp.bfloat16

EPS = 1e-6
ROPE_THETA = 10000.0
CHUNK = 64

GM_GROUPS = 4
GM_GROUP_DIM = 128
GM_WIDTH = GM_GROUPS * GM_GROUP_DIM
GM_BLOCK = 128
MLA_HEADS = 8
MLA_Q_RANK = 384
MLA_KV_RANK = 256
MLA_NOPE = 64
MLA_ROPE = 32
MLA_V = 64
MLA_QK = MLA_NOPE + MLA_ROPE
RET_HEADS = 4
RET_QK = 256
RET_V = 512

LANES = 128
BF16_SUBLANES = 16
ONES_ROWS = BF16_SUBLANES
V7X_VMEM_LIMIT = 56 * 1024 * 1024

EVEN_IN_ROWS = 512
EVEN_IN_SUB = 256
ATTN_Q_ROWS = 512
ATTN_KV_ROWS = 512
ATTN_UNIT = 256
ATTN_GROUP = 4
FFN_ROWS = 512
ODD_IN_ROWS = 1024
ODD_IN_COLS = 1024
RET_ROWS = 1024
RET_CHUNK = 256
FF_CHUNKS = (768, 768, 768, 512)

NT_DIMS = (((1,), (1,)), ((), ()))


def _rms_scale(t):
    return lax.rsqrt(jnp.mean(t * t, axis=-1, keepdims=True) + EPS)


def _gelu_tanh(t):
    return 0.5 * t * (1.0 + jnp.tanh(math.sqrt(2.0 / math.pi) * (t + 0.044715 * (t * t * t))))


def _silu(t):
    return t * (1.0 / (1.0 + jnp.exp(-t)))


def _const_spec(shape):
    nd = len(shape)
    return pl.BlockSpec(shape, lambda *_: (0,) * nd, pipeline_mode=pl.Buffered(1))


def _params(sem):
    return pltpu.CompilerParams(dimension_semantics=sem, vmem_limit_bytes=V7X_VMEM_LIMIT)


def _even_in_kernel(x_ref, nmix_ref, wuv_ref, wct_ref, gvn_ref, ws_ref, bs_ref, gains_ref, wuqt_ref,
                    wukvt_ref, cos_ref, sin_ref, a_ref, qt_ref, k_ref, vt_ref):
    g1 = MLA_Q_RANK
    g2 = g1 + MLA_KV_RANK
    gqa_ref, gkva_ref = gains_ref.at[0:g1], gains_ref.at[g1:g2]
    gq_ref, gk_ref = gains_ref.at[g2:g2 + LANES], gains_ref.at[g2 + LANES:g2 + 2 * LANES]
    subs = range(x_ref.shape[0] // EVEN_IN_SUB)
    proj = [_even_in_project(sub, x_ref, nmix_ref, wuv_ref, wct_ref) for sub in subs]
    heads = [_even_in_gate_and_expand(sub, proj[sub], gvn_ref, ws_ref, bs_ref, gqa_ref, wuqt_ref, gkva_ref,
                                      wukvt_ref, a_ref) for sub in subs]
    for sub in subs:
        _even_in_heads(sub, heads[sub], gq_ref, gk_ref, cos_ref, sin_ref, qt_ref, k_ref, vt_ref)


def _even_in_project(sub, x_ref, nmix_ref, wuv_ref, wct_ref):
    rows = EVEN_IN_SUB
    x = x_ref[sub * rows:(sub + 1) * rows, :]
    h = (x * _rms_scale(x) * nmix_ref[...]).astype(BF16)
    zuv = jnp.dot(h, wuv_ref[...], preferred_element_type=F32)
    zc = lax.dot_general(wct_ref[...], h, NT_DIMS, preferred_element_type=F32)
    return zuv, zc


def _even_in_gate_and_expand(sub, proj, gvn_ref, ws_ref, bs_ref, gqa_ref, wuqt_ref, gkva_ref, wukvt_ref, a_ref):
    rows = EVEN_IN_SUB
    zuv, zc = proj
    u = _gelu_tanh(zuv[:, :GM_WIDTH])
    v = _gelu_tanh(zuv[:, GM_WIDTH:])
    t_out = lax.broadcasted_iota(jnp.int32, (GM_BLOCK, GM_BLOCK), 0) // CHUNK
    t_in = lax.broadcasted_iota(jnp.int32, (GM_BLOCK, GM_BLOCK), 1) // CHUNK
    causal = t_in <= t_out
    nblk = rows // GM_BLOCK
    for g in range(GM_GROUPS):
        cs = slice(g * GM_GROUP_DIM, (g + 1) * GM_GROUP_DIM)
        vg = v[:, cs]
        vn = (vg * _rms_scale(vg) * gvn_ref[:, cs]).astype(BF16)
        wg = jnp.where(causal, ws_ref[g], 0.0).astype(BF16)
        vcat = jnp.concatenate([vn[j * GM_BLOCK:(j + 1) * GM_BLOCK] for j in range(nblk)], axis=1)
        s = jnp.dot(wg, vcat, preferred_element_type=F32) + bs_ref[:, g:g + 1]
        for j in range(nblk):
            rs = slice(j * GM_BLOCK, (j + 1) * GM_BLOCK)
            dst = slice(sub * rows + j * GM_BLOCK, sub * rows + (j + 1) * GM_BLOCK)
            a_ref[dst, cs] = (u[rs, cs] * s[:, j * GM_BLOCK:(j + 1) * GM_BLOCK]).astype(a_ref.dtype)

    o1 = MLA_Q_RANK
    o2 = o1 + MLA_KV_RANK
    cq, ckv, kpe = zc[:o1], zc[o1:o2], zc[o2:]

    def col_rms(t, n):
        return lax.rsqrt(jnp.sum(t * t, axis=0, keepdims=True) * (1.0 / n) + EPS)

    cqn = (cq * col_rms(cq, MLA_Q_RANK) * gqa_ref[...]).astype(BF16)
    ckvn = (ckv * col_rms(ckv, MLA_KV_RANK) * gkva_ref[...]).astype(BF16)
    qt = jnp.dot(wuqt_ref[...], cqn, preferred_element_type=F32)
    kvt = jnp.dot(wukvt_ref[...], ckvn, preferred_element_type=F32)
    return qt, kvt, kpe


def _even_in_heads(sub, expanded, gq_ref, gk_ref, cos_ref, sin_ref, qt_ref, k_ref, vt_ref):
    rows = EVEN_IN_SUB
    qt, kvt, kpe = expanded
    tok = slice(sub * rows, (sub + 1) * rows)
    cos, sin = cos_ref[:, tok], sin_ref[:, tok]
    half = MLA_ROPE // 2
    n1, n2 = MLA_NOPE, MLA_NOPE + half

    def rope(x1, x2):
        return x1 * cos - x2 * sin, x2 * cos + x1 * sin

    gq = gq_ref[...] * (MLA_QK ** -0.5 * math.log2(math.e))
    for hd in range(MLA_HEADS):
        t = qt[hd * LANES:(hd + 1) * LANES]
        r = lax.rsqrt(jnp.sum(t * t, axis=0, keepdims=True) * (1.0 / MLA_QK) + EPS)
        tn = t * gq
        r1, r2 = rope(tn[n1:n2], tn[n2:MLA_QK])
        base = hd * LANES
        qt_ref[base:base + n1, tok] = (tn[:n1] * r).astype(qt_ref.dtype)
        qt_ref[base + n1:base + n2, tok] = (r1 * r).astype(qt_ref.dtype)
        qt_ref[base + n2:base + MLA_QK, tok] = (r2 * r).astype(qt_ref.dtype)
        qt_ref[base + MLA_QK:base + LANES, tok] = jnp.zeros((LANES - MLA_QK, rows), qt_ref.dtype)

    gk = gk_ref[...]
    kpe_ss = jnp.sum(kpe * kpe, axis=0, keepdims=True)
    kr1, kr2 = rope(kpe[:half] * gk[n1:n2], kpe[half:] * gk[n2:MLA_QK])
    pad = jnp.zeros((LANES - MLA_QK, rows), F32)
    for hd in range(MLA_HEADS):
        t = kvt[hd * MLA_NOPE:(hd + 1) * MLA_NOPE]
        r = lax.rsqrt((jnp.sum(t * t, axis=0, keepdims=True) + kpe_ss) * (1.0 / MLA_QK) + EPS)
        kt = jnp.concatenate([t * gk[:n1] * r, kr1 * r, kr2 * r, pad], axis=0)
        k_ref[tok, hd * LANES:(hd + 1) * LANES] = kt.T.astype(k_ref.dtype)
    vt_ref[sub] = kvt[MLA_HEADS * MLA_NOPE:].astype(vt_ref.dtype)


def _even_in(x2, nmix, wuv, wct, gvn, ws, bs_t, gains, wuqt, wukvt, cos_t, sin_t, seq):
    tokens, d = x2.shape
    rows = EVEN_IN_ROWS
    per_seq = seq // rows
    row_spec = lambda w: pl.BlockSpec((rows, w), lambda i: (i, 0))
    tab_spec = pl.BlockSpec((MLA_ROPE // 2, rows), lambda i: (0, i % per_seq))
    hw = MLA_HEADS * LANES
    vw = MLA_HEADS * MLA_V
    consts = [nmix, wuv, wct, gvn, ws, bs_t, gains, wuqt, wukvt]
    return pl.pallas_call(
        _even_in_kernel,
        grid=(tokens // rows,),
        in_specs=[row_spec(d)] + [_const_spec(c.shape) for c in consts] + [tab_spec, tab_spec],
        out_specs=[row_spec(GM_WIDTH),
                   pl.BlockSpec((hw, rows), lambda i: (0, i)),
                   row_spec(hw),
                   pl.BlockSpec((rows // EVEN_IN_SUB, vw, EVEN_IN_SUB), lambda i: (i, 0, 0))],
        out_shape=[jax.ShapeDtypeStruct((tokens, GM_WIDTH), BF16),
                   jax.ShapeDtypeStruct((hw, tokens), BF16),
                   jax.ShapeDtypeStruct((tokens, hw), BF16),
                   jax.ShapeDtypeStruct((tokens // EVEN_IN_SUB, vw, EVEN_IN_SUB), BF16)],
        compiler_params=_params(("parallel",)),
        name="even_in",
    )(x2, *consts, cos_t, sin_t)


def _attn_kernel(*refs, n_groups, n_cast):
    qt_ref, k_ref, vt_ref = refs[:3]
    w_refs = refs[3:3 + n_cast]
    o_ref = refs[3 + n_cast]
    wb_refs = refs[4 + n_cast:4 + 2 * n_cast]
    sa_ref, sb_ref, ma_ref, mb_ref, m_ref, acc_ref = refs[4 + 2 * n_cast:]

    for w_ref, wb_ref in zip(w_refs, wb_refs):
        wb_ref[...] = w_ref[...].astype(wb_ref.dtype)

    tq = ATTN_Q_ROWS
    tk = ATTN_KV_ROWS
    vrows = vt_ref.shape[2]
    ones = jnp.ones((ONES_ROWS, vrows), BF16)
    bufs = ((sa_ref, ma_ref), (sb_ref, mb_ref))
    units = [(hh, slice(u * ATTN_UNIT, (u + 1) * ATTN_UNIT)) for hh in range(2) for u in range(tq // ATTN_UNIT)]

    def run(group):
        first = group * ATTN_GROUP
        tasks = [(c, t) for c in range(first, first + ATTN_GROUP) for t in range(c + 1)]

        def key_rows(c, j, qs):
            return qs.stop if j == c else tk

        def scores(n, hh, qs):
            c, j = tasks[n]
            dst, bmax = bufs[n % 2]
            rows = key_rows(c, j, qs)
            q0 = (c - first) * tq
            s = jnp.dot(k_ref[j * tk:j * tk + rows, hh * LANES:(hh + 1) * LANES],
                        qt_ref[hh * LANES:(hh + 1) * LANES, q0 + qs.start:q0 + qs.stop],
                        preferred_element_type=F32)
            if j == c:
                ck = lax.broadcasted_iota(jnp.int32, s.shape, 0) // CHUNK
                cq = (lax.broadcasted_iota(jnp.int32, s.shape, 1) + qs.start) // CHUNK
                s = jnp.where(ck <= cq, s, -jnp.inf)
            dst[hh, :rows, qs] = s
            bmax[hh, :, qs] = jnp.max(s, axis=0, keepdims=True)

        def update(n, hh, qs):
            c, j = tasks[n]
            src, bmax = bufs[n % 2]
            rows = key_rows(c, j, qs)
            slot = c % 2
            m = m_ref[slot, hh, :, qs]
            m_new = jnp.maximum(m, bmax[hh, :, qs])
            alpha = jnp.exp2(m - m_new)
            p = jnp.exp2(src[hh, :rows, qs] - m_new).astype(BF16)
            m_ref[slot, hh, :, qs] = m_new
            pv = None
            for ch in range(rows // vrows):
                vt = vt_ref[j * (tk // vrows) + ch, hh * MLA_V:(hh + 1) * MLA_V, :]
                part = jnp.dot(jnp.concatenate([vt, ones], axis=0), p[ch * vrows:(ch + 1) * vrows],
                               preferred_element_type=F32)
                pv = part if pv is None else pv + part
            acc_ref[slot, hh, :, qs] = alpha * acc_ref[slot, hh, :, qs] + pv

        def start_block(c):
            m_ref[c % 2] = jnp.full(m_ref.shape[1:], -jnp.inf, F32)
            acc_ref[c % 2] = jnp.zeros(acc_ref.shape[1:], F32)

        def finish_block(c):
            acc = acc_ref[c % 2]
            ot = jnp.concatenate([acc[hh, :MLA_V] * (1.0 / acc[hh, MLA_V:MLA_V + 1]) for hh in range(2)],
                                 axis=0)
            o_ref[(c - first) * tq:(c - first + 1) * tq, :] = ot.T.astype(o_ref.dtype)

        start_block(first)
        for hh, qs in units:
            scores(0, hh, qs)
        for n, (c, j) in enumerate(tasks):
            has_next = n + 1 < len(tasks)
            if has_next and tasks[n + 1][1] == 0:
                start_block(tasks[n + 1][0])
            for hh, qs in units:
                if has_next:
                    scores(n + 1, hh, qs)
                update(n, hh, qs)
            if j == c:
                finish_block(c)

    for group in range(n_groups):
        pl.when(pl.program_id(2) == group)(functools.partial(run, group))


def _attention(qt, k3, vt, b, s, cast_weights):
    assert ATTN_Q_ROWS == ATTN_KV_ROWS and ATTN_UNIT % vt.shape[2] == 0
    tq = ATTN_Q_ROWS
    pairs = MLA_HEADS // 2
    vchunks, _, vrows = vt.shape
    rows = ATTN_GROUP * tq
    n_groups = s // rows
    steps = b * pairs * n_groups
    step = lambda bi, p, g: (bi * pairs + p) * n_groups + g

    def cast_spec(w):
        share = next(sh for sh in (1, 2, 4, 8) if w.shape[0] % (steps // sh * BF16_SUBLANES) == 0)
        return pl.BlockSpec((w.shape[0] // (steps // share), w.shape[1]),
                            lambda bi, p, g: (step(bi, p, g) // share, 0))

    cast_specs = [cast_spec(w) for w in cast_weights]
    return pl.pallas_call(
        functools.partial(_attn_kernel, n_groups=n_groups, n_cast=len(cast_weights)),
        grid=(b, pairs, n_groups),
        in_specs=[pl.BlockSpec((2 * LANES, rows), lambda bi, p, g: (p, bi * n_groups + g)),
                  pl.BlockSpec((None, s, 2 * LANES), lambda bi, p, g: (bi, 0, p)),
                  pl.BlockSpec((vchunks // b, 2 * MLA_V, vrows), lambda bi, p, g: (bi, p, 0))] + cast_specs,
        out_specs=[pl.BlockSpec((None, rows, LANES), lambda bi, p, g: (bi, g, p))] + cast_specs,
        out_shape=[jax.ShapeDtypeStruct((b, s, MLA_HEADS * MLA_V), BF16)]
        + [jax.ShapeDtypeStruct(w.shape, BF16) for w in cast_weights],
        scratch_shapes=[pltpu.VMEM((2, ATTN_KV_ROWS, tq), F32),
                        pltpu.VMEM((2, ATTN_KV_ROWS, tq), F32),
                        pltpu.VMEM((2, 1, tq), F32),
                        pltpu.VMEM((2, 1, tq), F32),
                        pltpu.VMEM((2, 2, 1, tq), F32),
                        pltpu.VMEM((2, 2, MLA_V + ONES_ROWS, tq), F32)],
        compiler_params=_params(("arbitrary", "arbitrary", "arbitrary")),
        name="attention",
    )(qt, k3, vt, *cast_weights)


def _proj_ffn_kernel(*refs, n_in, gated):
    in_refs = refs[:n_in]
    gate_refs = refs[n_in:2 * n_in] if gated else [None] * n_in
    wout_ref, x_ref, nffn_ref, wg_ref, wu_ref, wd_ref, out_ref = refs[n_in * (2 if gated else 1):]
    mix = None
    off = 0
    for r, gate in zip(in_refs, gate_refs):
        kdim = r.shape[1]
        lhs = r[...] if gate is None else r[...] * gate[...]
        part = jnp.dot(lhs, wout_ref[off:off + kdim, :], preferred_element_type=F32)
        mix = part if mix is None else mix + part
        off += kdim
    x1 = x_ref[...] + mix
    h = (x1 * _rms_scale(x1) * nffn_ref[...]).astype(BF16)
    acc = x1
    c0 = 0
    for c in FF_CHUNKS:
        g = jnp.dot(h, wg_ref[:, c0:c0 + c], preferred_element_type=F32)
        u = jnp.dot(h, wu_ref[:, c0:c0 + c], preferred_element_type=F32)
        act = (_silu(g) * u).astype(BF16)
        acc = acc + jnp.dot(act, wd_ref[c0:c0 + c, :], preferred_element_type=F32)
        c0 += c
    out_ref[...] = acc


def _proj_ffn(mix_ins, wout, x2, nffn, wg, wu, wd, layer, gates=()):
    tokens, d = x2.shape
    rows = FFN_ROWS
    assert sum(FF_CHUNKS) == wg.shape[2] and len(gates) in (0, len(mix_ins))
    row_spec = lambda w: pl.BlockSpec((rows, w), lambda i: (i, 0))
    layer_spec = lambda w: pl.BlockSpec((None,) + w.shape[1:], lambda i: (layer, 0, 0),
                                        pipeline_mode=pl.Buffered(1))
    return pl.pallas_call(
        functools.partial(_proj_ffn_kernel, n_in=len(mix_ins), gated=bool(gates)),
        grid=(tokens // rows,),
        in_specs=[row_spec(m.shape[1]) for m in (*mix_ins, *gates)]
        + [_const_spec(wout.shape), row_spec(d), _const_spec(nffn.shape),
           layer_spec(wg), layer_spec(wu), layer_spec(wd)],
        out_specs=row_spec(d),
        out_shape=jax.ShapeDtypeStruct((tokens, d), F32),
        compiler_params=_params(("parallel",)),
        name="proj_ffn",
    )(*mix_ins, *gates, wout, x2, nffn, wg, wu, wd)


def _odd_in_kernel(x_ref, nmix_ref, win_ref, cos_ref, sin_ref, q_ref, k_ref, v_ref, g_ref):
    x = x_ref[...]
    h = (x * _rms_scale(x) * nmix_ref[...]).astype(BF16)
    cos, sin = cos_ref[...], sin_ref[...]
    nq = RET_HEADS * RET_QK
    nv = RET_HEADS * RET_V
    half = RET_QK // 2

    def project(col0, width):
        return jnp.dot(h, win_ref[:, col0:col0 + width], preferred_element_type=F32)

    def rope_store(dst, col0, scale):
        t_all = project(col0, nq)
        for hd in range(RET_HEADS):
            t1 = t_all[:, hd * RET_QK:hd * RET_QK + half]
            t2 = t_all[:, hd * RET_QK + half:(hd + 1) * RET_QK]
            dst[:, hd * RET_QK:hd * RET_QK + half] = ((t1 * cos - t2 * sin) * scale).astype(dst.dtype)
            dst[:, hd * RET_QK + half:(hd + 1) * RET_QK] = ((t2 * cos + t1 * sin) * scale).astype(dst.dtype)

    rope_store(q_ref, 0, 1.0)
    rope_store(k_ref, nq, RET_QK ** -0.5)
    for c0 in range(0, nv, ODD_IN_COLS):
        g_ref[:, c0:c0 + ODD_IN_COLS] = _silu(project(2 * nq + nv + c0, ODD_IN_COLS)).astype(g_ref.dtype)
    for c0 in range(0, nv, ODD_IN_COLS):
        v_ref[:, c0:c0 + ODD_IN_COLS] = project(2 * nq + c0, ODD_IN_COLS).astype(v_ref.dtype)


def _odd_in(x2, nmix, win, cos, sin, seq):
    tokens, d = x2.shape
    rows = ODD_IN_ROWS
    per_seq = seq // rows
    nq = RET_HEADS * RET_QK
    nv = RET_HEADS * RET_V
    row_spec = lambda w: pl.BlockSpec((rows, w), lambda i: (i, 0))
    tab_spec = pl.BlockSpec((rows, RET_QK // 2), lambda i: (i % per_seq, 0))
    return pl.pallas_call(
        _odd_in_kernel,
        grid=(tokens // rows,),
        in_specs=[row_spec(d), _const_spec(nmix.shape), _const_spec(win.shape), tab_spec, tab_spec],
        out_specs=[row_spec(nq), row_spec(nq), row_spec(nv), row_spec(nv)],
        out_shape=[jax.ShapeDtypeStruct((tokens, nq), BF16), jax.ShapeDtypeStruct((tokens, nq), BF16),
                   jax.ShapeDtypeStruct((tokens, nv), BF16), jax.ShapeDtypeStruct((tokens, nv), BF16)],
        compiler_params=_params(("parallel",)),
        name="odd_in",
    )(x2, nmix, win, cos, sin)


def _retention_kernel(q_ref, k_ref, v_ref, ron_ref, o_ref, state_ref, decay_ref, zeta_ref, xi_ref):
    tc = RET_CHUNK
    j = pl.program_id(1)

    def log_gamma(hd):
        return jnp.log(jnp.full((1, 1), 1.0 - 2.0 ** (-5.0 - hd), F32))

    @pl.when(j == 0)
    def _():
        state_ref[...] = jnp.zeros_like(state_ref)
        diff = (lax.broadcasted_iota(jnp.int32, (tc, tc), 0)
                - lax.broadcasted_iota(jnp.int32, (tc, tc), 1))
        dpos = jnp.maximum(diff, 0).astype(F32)
        kpos = lax.broadcasted_iota(jnp.int32, (tc, RET_QK), 0).astype(F32)
        for hd in range(RET_HEADS):
            decay_ref[hd] = jnp.where(diff >= 0, jnp.exp(log_gamma(hd) * dpos), 0.0).astype(decay_ref.dtype)
            zeta_ref[hd] = jnp.exp(log_gamma(hd) * (tc - 1.0 - kpos)).astype(zeta_ref.dtype)
            xi_ref[hd] = jnp.exp(log_gamma(hd) * (kpos + 1.0)).astype(xi_ref.dtype)

    chunk_decay = [jnp.exp(log_gamma(hd) * float(tc)) for hd in range(RET_HEADS)]
    heads = range(RET_HEADS)
    qcol = [slice(hd * RET_QK, (hd + 1) * RET_QK) for hd in heads]
    vcol = [slice(hd * RET_V, (hd + 1) * RET_V) for hd in heads]
    def in_chunk(rs):
        return [lax.dot_general(q_ref[rs, qcol[hd]], k_ref[rs, qcol[hd]], NT_DIMS,
                                preferred_element_type=F32).astype(BF16) * decay_ref[hd] for hd in heads]

    def outputs(rs, a):
        return [jnp.dot(jnp.concatenate([a[hd], q_ref[rs, qcol[hd]] * xi_ref[hd]], axis=1),
                        jnp.concatenate([v_ref[rs, vcol[hd]], state_ref[hd].astype(BF16)], axis=0),
                        preferred_element_type=F32) for hd in heads]

    def advance_state(rs):
        for hd in heads:
            state_ref[hd] = state_ref[hd] * chunk_decay[hd] + lax.dot_general(
                k_ref[rs, qcol[hd]] * zeta_ref[hd], v_ref[rs, vcol[hd]], (((0,), (0,)), ((), ())),
                preferred_element_type=F32)

    def norm_store(rs, o):
        for hd in heads:
            o_ref[rs, vcol[hd]] = (o[hd] * _rms_scale(o[hd]) * ron_ref[:, vcol[hd]]).astype(o_ref.dtype)

    chunks = [slice(c * tc, (c + 1) * tc) for c in range(q_ref.shape[0] // tc)]
    a = in_chunk(chunks[0])
    for ci, rs in enumerate(chunks):
        o = outputs(rs, a)
        advance_state(rs)
        if ci + 1 < len(chunks):
            a = in_chunk(chunks[ci + 1])
        norm_store(rs, o)


def _retention(q3, k3, v3, ron):
    b, s, nq = q3.shape
    nv = v3.shape[2]
    tc = RET_ROWS
    blk = lambda w: pl.BlockSpec((None, tc, w), lambda bi, j: (bi, j, 0))
    return pl.pallas_call(
        _retention_kernel,
        grid=(b, s // tc),
        in_specs=[blk(nq), blk(nq), blk(nv), _const_spec(ron.shape)],
        out_specs=blk(nv),
        out_shape=jax.ShapeDtypeStruct((b, s, nv), BF16),
        scratch_shapes=[pltpu.VMEM((RET_HEADS, RET_QK, RET_V), F32),
                        pltpu.VMEM((RET_HEADS, RET_CHUNK, RET_CHUNK), BF16),
                        pltpu.VMEM((RET_HEADS, RET_CHUNK, RET_QK), BF16),
                        pltpu.VMEM((RET_HEADS, RET_CHUNK, RET_QK), BF16)],
        compiler_params=_params(("arbitrary", "arbitrary")),
        name="retention",
    )(q3, k3, v3, ron)


def _rope_tables(seq, half):
    inv = ROPE_THETA ** (-np.arange(half, dtype=np.float64) / half)
    ang = np.arange(seq, dtype=np.float64)[:, None] * inv[None, :]
    return np.cos(ang).astype(np.float32), np.sin(ang).astype(np.float32)


def _lane_bcast(g, width):
    return jnp.broadcast_to(g[:, None], (g.shape[0], width))


def kernel(x, norm_mix, norm_ffn, even_w_in, gm_v_norm, gm_w_s, gm_b_s, mla_q_a_norm, mla_w_uq,
           mla_kv_a_norm, mla_w_ukv, mla_q_norm, mla_k_norm, even_w_out, odd_w_in, ret_out_norm,
           odd_w_out, ffn_w_gate, ffn_w_up, ffn_w_down):
    b, s, d = x.shape
    tokens = b * s
    x2 = x.reshape(tokens, d)
    row = lambda t: t.reshape(1, -1)

    w_in = even_w_in[0]
    wuv = w_in[:, :2 * GM_WIDTH].astype(BF16)
    wct = w_in[:, 2 * GM_WIDTH:].T.astype(BF16)
    wuq3 = mla_w_uq[0].reshape(MLA_Q_RANK, MLA_HEADS, MLA_QK)
    wuqt = jnp.pad(wuq3, ((0, 0), (0, 0), (0, LANES - MLA_QK))).reshape(MLA_Q_RANK, -1).T.astype(BF16)
    wukv3 = mla_w_ukv[0].reshape(MLA_KV_RANK, MLA_HEADS, MLA_NOPE + MLA_V)
    wukvt = jnp.concatenate([wukv3[:, :, :MLA_NOPE].reshape(MLA_KV_RANK, -1),
                             wukv3[:, :, MLA_NOPE:].reshape(MLA_KV_RANK, -1)], axis=1).T.astype(BF16)
    er = EVEN_IN_SUB
    pad_head = lambda g: jnp.pad(g, (0, LANES - MLA_QK))
    cos0, sin0 = _rope_tables(s, MLA_ROPE // 2)
    gains = _lane_bcast(jnp.concatenate([mla_q_a_norm[0], mla_kv_a_norm[0], pad_head(mla_q_norm[0]),
                                         pad_head(mla_k_norm[0])]), er)
    a, qt, k, vt = _even_in(
        x2, row(norm_mix[0]), wuv, wct, row(gm_v_norm[0]), gm_w_s[0], gm_b_s[0].T, gains, wuqt, wukvt,
        jnp.asarray(cos0.T), jnp.asarray(sin0.T), s)
    later = [ffn_w_gate, ffn_w_up, ffn_w_down, even_w_out[0], odd_w_in[0], odd_w_out[0]]
    flat = lambda w: w.reshape(-1, w.shape[-1])
    o, *cast = _attention(qt, k.reshape(b, s, MLA_HEADS * LANES), vt, b, s, [flat(w) for w in later])
    wg, wu, wd, w_eo, w_oi, w_oo = [c.reshape(w.shape) for c, w in zip(cast, later)]
    x2 = _proj_ffn([a, o.reshape(tokens, -1)], w_eo, x2, row(norm_ffn[0]), wg, wu, wd, 0)

    cos1, sin1 = _rope_tables(s, RET_QK // 2)
    rq, rk, rv, rg = _odd_in(x2, row(norm_mix[1]), w_oi, jnp.asarray(cos1), jnp.asarray(sin1), s)
    nq = RET_HEADS * RET_QK
    nv = RET_HEADS * RET_V
    on = _retention(rq.reshape(b, s, nq), rk.reshape(b, s, nq), rv.reshape(b, s, nv), row(ret_out_norm[0]))
    x2 = _proj_ffn([on.reshape(tokens, nv)], w_oo, x2, row(norm_ffn[1]), wg, wu, wd, 1, gates=[rg])
    return x2.reshape(b, s, d)
```
